```python
import math
import jax, jax.numpy as jnp
from jax import lax
import numpy as np

D_MODEL = 2048
BATCH = 8
SEQ = 4096
DEPTH = 4

HEAD_DIM = 64
N_HEADS_A = D_MODEL // HEAD_DIM
N_KV_A = N_HEADS_A // 8
GROUP_A = N_HEADS_A // N_KV_A
N_HEADS_B = D_MODEL // HEAD_DIM
WINDOW = 128
BLOCK = 128
D_FF = 4 * D_MODEL
N_MIXERS = 2
N_A_LAYERS = (DEPTH + 1) // 2
N_B_LAYERS = DEPTH // 2
RMS_EPS = 1e-5

kernel_name = "hybrid_swa_sink_alibi_stickbreaking_sqrelu"


def rmsnorm(x, gain):
    xf = x.astype(jnp.float32)
    y = xf * lax.rsqrt(jnp.mean(xf * xf, axis=-1, keepdims=True) + RMS_EPS)
    return (y * gain.astype(jnp.float32)).astype(x.dtype)


def alibi_slopes(n_heads):
    return jnp.power(2.0, -8.0 * (jnp.arange(n_heads, dtype=jnp.float32) + 1.0) / n_heads)


def sliding_window_sink_attention(xn, w_qkv, w_o, sinks):
    b, s, _ = xn.shape
    nb = s // BLOCK
    qkv = xn @ w_qkv
    q_w, k_w = N_HEADS_A * HEAD_DIM, N_KV_A * HEAD_DIM
    q = qkv[..., :q_w].reshape(b, nb, BLOCK, N_KV_A, GROUP_A, HEAD_DIM)
    k = qkv[..., q_w:q_w + k_w].reshape(b, nb, BLOCK, N_KV_A, HEAD_DIM)
    v = qkv[..., q_w + k_w:].reshape(b, nb, BLOCK, N_KV_A, HEAD_DIM)
    pad = ((0, 0), (1, 0), (0, 0), (0, 0), (0, 0))
    k_win = jnp.concatenate([jnp.pad(k, pad)[:, :-1], k], axis=2)
    v_win = jnp.concatenate([jnp.pad(v, pad)[:, :-1], v], axis=2)
    scale = 1.0 / math.sqrt(HEAD_DIM)
    scores = jnp.einsum('bnqhgd,bnkhd->bnhgqk', q, k_win).astype(jnp.float32) * scale
    dist = (jnp.arange(BLOCK)[:, None] + BLOCK) - jnp.arange(2 * BLOCK)[None, :]
    key_abs = (jnp.arange(nb)[:, None] - 1) * BLOCK + jnp.arange(2 * BLOCK)[None, :]
    valid = ((dist >= 0) & (dist < WINDOW))[None, :, :] & (key_abs >= 0)[:, None, :]
    slopes = alibi_slopes(N_HEADS_A).reshape(N_KV_A, GROUP_A)
    scores = scores - slopes[None, None, :, :, None, None] * dist.astype(jnp.float32)
    scores = jnp.where(valid[None, :, None, None], scores, -jnp.inf)
    sink = sinks.astype(jnp.float32).reshape(N_KV_A, GROUP_A)[None, None, :, :, None, None]
    m = jnp.maximum(jnp.max(scores, axis=-1, keepdims=True), sink)
    p = jnp.exp(scores - m)
    p = p / (jnp.sum(p, axis=-1, keepdims=True) + jnp.exp(sink - m))
    out = jnp.einsum('bnhgqk,bnkhd->bnqhgd', p.astype(v.dtype), v_win)
    return out.reshape(b, s, N_HEADS_A * HEAD_DIM) @ w_o


def stick_breaking_attention(xn, w_qkv, w_o):
    b, s, _ = xn.shape
    nb = s // BLOCK
    qkv = xn @ w_qkv
    w = N_HEADS_B * HEAD_DIM
    q = qkv[..., :w].reshape(b, nb, BLOCK, N_HEADS_B, HEAD_DIM)
    k = qkv[..., w:2 * w].reshape(b, s, N_HEADS_B, HEAD_DIM)
    v = qkv[..., 2 * w:].reshape(b, s, N_HEADS_B, HEAD_DIM)
    q_blocks = jnp.moveaxis(q, 1, 0)
    starts = jnp.arange(nb, dtype=jnp.int32) * BLOCK
    scale = 1.0 / math.sqrt(HEAD_DIM)
    key_pos = jnp.arange(s, dtype=jnp.int32)

    def one_block(args):
        qb, start = args
        z = jnp.einsum('bqhd,bkhd->bhqk', qb, k).astype(jnp.float32) * scale
        t = start + jnp.arange(BLOCK, dtype=jnp.int32)
        before = (key_pos[None, :] < t[:, None])[None, None]
        log_beta = jax.nn.log_sigmoid(z)
        log_1m_beta = jnp.where(before, jax.nn.log_sigmoid(-z), 0.0)
        suffix = lax.cumsum(log_1m_beta, axis=3, reverse=True) - log_1m_beta
        a = jnp.where(before, jnp.exp(log_beta + suffix), 0.0)
        return jnp.einsum('bhqk,bkhd->bqhd', a.astype(v.dtype), v)

    out = lax.map(one_block, (q_blocks, starts))
    out = jnp.moveaxis(out, 0, 1).reshape(b, s, N_HEADS_B * HEAD_DIM)
    return out @ w_o


def squared_relu_mlp(xn, w_in, w_out):
    h = jax.nn.relu(xn @ w_in)
    return (h * h) @ w_out


def _fwd_setup_inputs(seed: int = 0) -> dict:
    key = jax.random.key(seed)
    ks = jax.random.split(key, 12)
    d = D_MODEL
    qkv_a = (N_HEADS_A + 2 * N_KV_A) * HEAD_DIM
    qkv_b = 3 * N_HEADS_B * HEAD_DIM
    x = jax.random.normal(ks[0], (BATCH, SEQ, d), jnp.float32)
    a_w_qkv = jax.random.normal(ks[1], (N_A_LAYERS, d, qkv_a), jnp.float32) * d ** -0.5
    a_w_o = jax.random.normal(ks[2], (N_A_LAYERS, N_HEADS_A * HEAD_DIM, d), jnp.float32) * (N_HEADS_A * HEAD_DIM) ** -0.5
    a_sinks = jax.random.normal(ks[3], (N_A_LAYERS, N_HEADS_A), jnp.float32) * 0.5
    b_w_qkv = jax.random.normal(ks[4], (N_B_LAYERS, d, qkv_b), jnp.float32) * d ** -0.5
    b_w_o = jax.random.normal(ks[5], (N_B_LAYERS, N_HEADS_B * HEAD_DIM, d), jnp.float32) * (N_HEADS_B * HEAD_DIM) ** -0.5
    norm_mix = 1.0 + 0.02 * jax.random.normal(ks[6], (DEPTH, d), jnp.float32)
    norm_mlp = 1.0 + 0.02 * jax.random.normal(ks[7], (DEPTH, d), jnp.float32)
    mlp_w_in = jax.random.normal(ks[8], (DEPTH, d, D_FF), jnp.float32) * d ** -0.5
    mlp_w_out = jax.random.normal(ks[9], (DEPTH, D_FF, d), jnp.float32) * D_FF ** -0.5
    final_norm = 1.0 + 0.02 * jax.random.normal(ks[10], (d,), jnp.float32)
    return {"x": x, "a_w_qkv": a_w_qkv, "a_w_o": a_w_o, "a_sinks": a_sinks,
            "b_w_qkv": b_w_qkv, "b_w_o": b_w_o, "norm_mix": norm_mix,
            "norm_mlp": norm_mlp, "mlp_w_in": mlp_w_in, "mlp_w_out": mlp_w_out,
            "final_norm": final_norm}


def _fwd_reference(x, a_w_qkv, a_w_o, a_sinks, b_w_qkv, b_w_o, norm_mix, norm_mlp,
              mlp_w_in, mlp_w_out, final_norm):
    for i in range(DEPTH):
        h = rmsnorm(x, norm_mix[i])
        j = i // N_MIXERS
        if i % N_MIXERS == 0:
            x = x + sliding_window_sink_attention(h, a_w_qkv[j], a_w_o[j], a_sinks[j])
        else:
            x = x + stick_breaking_attention(h, b_w_qkv[j], b_w_o[j])
        h = rmsnorm(x, norm_mlp[i])
        x = x + squared_relu_mlp(h, mlp_w_in[i], mlp_w_out[i])
    return rmsnorm(x, final_norm)


import jax as _jax
import jax.numpy as _jnp

TWIN_FORMAT = 'train_step'
FWD_PARAMS = ['x', 'a_w_qkv', 'a_w_o', 'a_sinks', 'b_w_qkv', 'b_w_o', 'norm_mix', 'norm_mlp', 'mlp_w_in', 'mlp_w_out', 'final_norm']
TWIN_WEIGHTS = ['a_w_qkv', 'a_w_o', 'a_sinks', 'b_w_qkv', 'b_w_o', 'norm_mix', 'norm_mlp', 'mlp_w_in', 'mlp_w_out', 'final_norm']
TWIN_DIFF_INPUT = 'x'
TWIN_INPUTS = ['x', 'a_w_qkv', 'a_w_o', 'a_sinks', 'b_w_qkv', 'b_w_o', 'norm_mix', 'norm_mlp', 'mlp_w_in', 'mlp_w_out', 'final_norm', 'loss_target', 'm_a_w_qkv', 'm_a_w_o', 'm_a_sinks', 'm_b_w_qkv', 'm_b_w_o', 'm_norm_mix', 'm_norm_mlp', 'm_mlp_w_in', 'm_mlp_w_out', 'm_final_norm', 'v_a_w_qkv', 'v_a_w_o', 'v_a_sinks', 'v_b_w_qkv', 'v_b_w_o', 'v_norm_mix', 'v_norm_mlp', 'v_mlp_w_in', 'v_mlp_w_out', 'v_final_norm']
TWIN_OUTPUTS = ['loss', 'grad_x', 'grad_a_w_qkv', 'grad_a_w_o', 'grad_a_sinks', 'grad_b_w_qkv', 'grad_b_w_o', 'grad_norm_mix', 'grad_norm_mlp', 'grad_mlp_w_in', 'grad_mlp_w_out', 'grad_final_norm', 'delta_a_w_qkv', 'delta_a_w_o', 'delta_a_sinks', 'delta_b_w_qkv', 'delta_b_w_o', 'delta_norm_mix', 'delta_norm_mlp', 'delta_mlp_w_in', 'delta_mlp_w_out', 'delta_final_norm', 'new_m_a_w_qkv', 'new_m_a_w_o', 'new_m_a_sinks', 'new_m_b_w_qkv', 'new_m_b_w_o', 'new_m_norm_mix', 'new_m_norm_mlp', 'new_m_mlp_w_in', 'new_m_mlp_w_out', 'new_m_final_norm', 'new_v_a_w_qkv', 'new_v_a_w_o', 'new_v_a_sinks', 'new_v_b_w_qkv', 'new_v_b_w_o', 'new_v_norm_mix', 'new_v_norm_mlp', 'new_v_mlp_w_in', 'new_v_mlp_w_out', 'new_v_final_norm']
TWIN_LEAF_KINDS = {'loss': 'loss', 'grad_x': 'grad_x', 'grad_a_w_qkv': 'grad_w', 'grad_a_w_o': 'grad_w', 'grad_a_sinks': 'grad_w', 'grad_b_w_qkv': 'grad_w', 'grad_b_w_o': 'grad_w', 'grad_norm_mix': 'grad_w', 'grad_norm_mlp': 'grad_w', 'grad_mlp_w_in': 'grad_w', 'grad_mlp_w_out': 'grad_w', 'grad_final_norm': 'grad_w', 'delta_a_w_qkv': 'delta_w', 'delta_a_w_o': 'delta_w', 'delta_a_sinks': 'delta_w', 'delta_b_w_qkv': 'delta_w', 'delta_b_w_o': 'delta_w', 'delta_norm_mix': 'delta_w', 'delta_norm_mlp': 'delta_w', 'delta_mlp_w_in': 'delta_w', 'delta_mlp_w_out': 'delta_w', 'delta_final_norm': 'delta_w', 'new_m_a_w_qkv': 'new_m', 'new_m_a_w_o': 'new_m', 'new_m_a_sinks': 'new_m', 'new_m_b_w_qkv': 'new_m', 'new_m_b_w_o': 'new_m', 'new_m_norm_mix': 'new_m', 'new_m_norm_mlp': 'new_m', 'new_m_mlp_w_in': 'new_m', 'new_m_mlp_w_out': 'new_m', 'new_m_final_norm': 'new_m', 'new_v_a_w_qkv': 'new_v', 'new_v_a_w_o': 'new_v', 'new_v_a_sinks': 'new_v', 'new_v_b_w_qkv': 'new_v', 'new_v_b_w_o': 'new_v', 'new_v_norm_mix': 'new_v', 'new_v_norm_mlp': 'new_v', 'new_v_mlp_w_in': 'new_v', 'new_v_mlp_w_out': 'new_v', 'new_v_final_norm': 'new_v'}


def _forward(args):
    return _fwd_reference(*[args[k] for k in FWD_PARAMS])


def _output_shape():
    def fwd():
        inp = _fwd_setup_inputs(0)
        return _fwd_reference(*[inp[k] for k in FWD_PARAMS])
    out = _jax.eval_shape(fwd)
    return out.shape, out.dtype

N_MICROBATCH = 1
ADAM_LR = 0.001
ADAM_B1 = 0.9
ADAM_B2 = 0.999
ADAM_EPS = 1e-08
ADAM_WD = 0.01
ADAM_STEP = 10
PER_EXAMPLE_BATCH_AXIS = {'x': 0, 'loss_target': 0}
SHARED_INPUTS = []
_WEIGHT_DTYPES = {'a_w_qkv': _jnp.float32, 'a_w_o': _jnp.float32, 'a_sinks': _jnp.float32, 'b_w_qkv': _jnp.float32, 'b_w_o': _jnp.float32, 'norm_mix': _jnp.float32, 'norm_mlp': _jnp.float32, 'mlp_w_in': _jnp.float32, 'mlp_w_out': _jnp.float32, 'final_norm': _jnp.float32}
MOMENT_SCALE = {'a_w_qkv': 3.783831e-02, 'a_w_o': 3.377494e-02, 'a_sinks': 4.971786e-02, 'b_w_qkv': 2.977175e-02, 'b_w_o': 4.753587e-02, 'norm_mix': 4.589741e-02, 'norm_mlp': 6.912141e-02, 'mlp_w_in': 3.420001e-02, 'mlp_w_out': 7.969327e-02, 'final_norm': 1.664164e+01}


def _to_microbatches(a, axis):
    t = _jnp.moveaxis(a, axis, 0)
    t = t.reshape((N_MICROBATCH, t.shape[0] // N_MICROBATCH) + t.shape[1:])
    return _jnp.moveaxis(t, 1, axis + 1)


def setup_inputs(seed: int = 0) -> dict:
    inp = _fwd_setup_inputs(seed)
    key = _jax.random.fold_in(_jax.random.key(seed), 7919)
    shape, _ = _output_shape()
    out = dict(inp)
    out["loss_target"] = _jax.random.normal(_jax.random.fold_in(key, 0), shape, _jnp.float32)
    for i, name in enumerate(TWIN_WEIGHTS):
        w = inp[name].astype(_jnp.float32)
        if MOMENT_SCALE is None:
            s = _jnp.sqrt(_jnp.mean(_jnp.square(w)) + 1e-30)
        else:
            s = MOMENT_SCALE[name]
        km, kv = _jax.random.split(_jax.random.fold_in(key, i + 1))
        out[name] = w
        out["m_" + name] = s * _jax.random.normal(km, w.shape, _jnp.float32)
        out["v_" + name] = (s * s) * _jax.random.uniform(kv, w.shape, _jnp.float32, 0.5, 1.5)
    if N_MICROBATCH > 1:
        for name, axis in PER_EXAMPLE_BATCH_AXIS.items():
            out[name] = _to_microbatches(out[name], axis)
    return {'x': out['x'], 'a_w_qkv': out['a_w_qkv'], 'a_w_o': out['a_w_o'], 'a_sinks': out['a_sinks'], 'b_w_qkv': out['b_w_qkv'], 'b_w_o': out['b_w_o'], 'norm_mix': out['norm_mix'], 'norm_mlp': out['norm_mlp'], 'mlp_w_in': out['mlp_w_in'], 'mlp_w_out': out['mlp_w_out'], 'final_norm': out['final_norm'], 'loss_target': out['loss_target'], 'm_a_w_qkv': out['m_a_w_qkv'], 'm_a_w_o': out['m_a_w_o'], 'm_a_sinks': out['m_a_sinks'], 'm_b_w_qkv': out['m_b_w_qkv'], 'm_b_w_o': out['m_b_w_o'], 'm_norm_mix': out['m_norm_mix'], 'm_norm_mlp': out['m_norm_mlp'], 'm_mlp_w_in': out['m_mlp_w_in'], 'm_mlp_w_out': out['m_mlp_w_out'], 'm_final_norm': out['m_final_norm'], 'v_a_w_qkv': out['v_a_w_qkv'], 'v_a_w_o': out['v_a_w_o'], 'v_a_sinks': out['v_a_sinks'], 'v_b_w_qkv': out['v_b_w_qkv'], 'v_b_w_o': out['v_b_w_o'], 'v_norm_mix': out['v_norm_mix'], 'v_norm_mlp': out['v_norm_mlp'], 'v_mlp_w_in': out['v_mlp_w_in'], 'v_mlp_w_out': out['v_mlp_w_out'], 'v_final_norm': out['v_final_norm']}


def _loss(weights, diff, rest, loss_target):
    with _jax.named_scope("forward"):
        args = {**rest, TWIN_DIFF_INPUT: diff, **{k: w.astype(_WEIGHT_DTYPES[k]) for k, w in weights.items()}}
        y = _forward(args)
    with _jax.named_scope("loss_head"):
        err = _jnp.square(y.astype(_jnp.float32) - loss_target)
        return 0.5 * _jnp.sum(_jnp.mean(err, axis=-1)) if err.ndim else 0.5 * err


def _adamw(w, g, m, v):
    m = ADAM_B1 * m + (1.0 - ADAM_B1) * g
    v = ADAM_B2 * v + (1.0 - ADAM_B2) * _jnp.square(g)
    m_hat = m / (1.0 - ADAM_B1 ** ADAM_STEP)
    v_hat = v / (1.0 - ADAM_B2 ** ADAM_STEP)
    delta = -ADAM_LR * (m_hat / (_jnp.sqrt(v_hat) + ADAM_EPS) + ADAM_WD * w)
    return delta, m, v


def reference(x, a_w_qkv, a_w_o, a_sinks, b_w_qkv, b_w_o, norm_mix, norm_mlp, mlp_w_in, mlp_w_out, final_norm, loss_target, m_a_w_qkv, m_a_w_o, m_a_sinks, m_b_w_qkv, m_b_w_o, m_norm_mix, m_norm_mlp, m_mlp_w_in, m_mlp_w_out, m_final_norm, v_a_w_qkv, v_a_w_o, v_a_sinks, v_b_w_qkv, v_b_w_o, v_norm_mix, v_norm_mlp, v_mlp_w_in, v_mlp_w_out, v_final_norm):
    given = dict(x=x, a_w_qkv=a_w_qkv, a_w_o=a_w_o, a_sinks=a_sinks, b_w_qkv=b_w_qkv, b_w_o=b_w_o, norm_mix=norm_mix, norm_mlp=norm_mlp, mlp_w_in=mlp_w_in, mlp_w_out=mlp_w_out, final_norm=final_norm, loss_target=loss_target, m_a_w_qkv=m_a_w_qkv, m_a_w_o=m_a_w_o, m_a_sinks=m_a_sinks, m_b_w_qkv=m_b_w_qkv, m_b_w_o=m_b_w_o, m_norm_mix=m_norm_mix, m_norm_mlp=m_norm_mlp, m_mlp_w_in=m_mlp_w_in, m_mlp_w_out=m_mlp_w_out, m_final_norm=m_final_norm, v_a_w_qkv=v_a_w_qkv, v_a_w_o=v_a_w_o, v_a_sinks=v_a_sinks, v_b_w_qkv=v_b_w_qkv, v_b_w_o=v_b_w_o, v_norm_mix=v_norm_mix, v_norm_mlp=v_norm_mlp, v_mlp_w_in=v_mlp_w_in, v_mlp_w_out=v_mlp_w_out, v_final_norm=v_final_norm)
    weights = {n: given[n] for n in TWIN_WEIGHTS}
    shared = {n: given[n] for n in SHARED_INPUTS}
    per_example = {n: given[n] for n in ['x']}
    grad_fn = _jax.value_and_grad(_loss, argnums=(0, 1))

    def one_microbatch(ex, loss_target):
        ex = dict(ex)
        diff = ex.pop(TWIN_DIFF_INPUT)
        return grad_fn(weights, diff, {**shared, **ex}, loss_target)

    if N_MICROBATCH == 1:
        loss, (grad_w, grad_x) = one_microbatch(per_example, given["loss_target"])
    else:
        def body(carry, xs):
            loss_sum, grad_sum = carry
            l_k, (gw_k, gx_k) = one_microbatch(xs[0], xs[1])
            with _jax.named_scope("update"):
                return (loss_sum + l_k, _jax.tree.map(_jnp.add, grad_sum, gw_k)), gx_k

        init = (_jnp.zeros((), _jnp.float32), _jax.tree.map(_jnp.zeros_like, weights))
        (loss, grad_w), grad_x = _jax.lax.scan(body, init, (per_example, given["loss_target"]))
    with _jax.named_scope("update"):
        delta_w, new_m, new_v = {}, {}, {}
        for n in TWIN_WEIGHTS:
            delta_w[n], new_m[n], new_v[n] = _adamw(weights[n], grad_w[n], given["m_" + n], given["v_" + n])
    return (loss, grad_x, *[grad_w[n] for n in TWIN_WEIGHTS], *[delta_w[n] for n in TWIN_WEIGHTS],
            *[new_m[n] for n in TWIN_WEIGHTS], *[new_v[n] for n in TWIN_WEIGHTS])
```

```python
import functools
import math

import jax
import jax.numpy as jnp
from jax import lax
from jax.experimental import pallas as pl
from jax.experimental.pallas import tpu as pltpu

F32, BF16 = jnp.float32, jnp.bfloat16
T, D, DFF = 4096, 2048, 8192
HD, BLK = 64, 128
NQ_A, NKV_A, GROUP_A = 32, 4, 8
NH_B = 32
NDEV = 8
RMS_EPS = 1e-5
SCALE = 0.125
NEG = -1e30
VMEM_LIMIT = 56 * 1024 * 1024
LR, B1, B2, EPS, WD, STEP = 0.001, 0.9, 0.999, 1e-08, 0.01, 10
MESH = pl.DeviceIdType.MESH


def _cp(sem, vmem=VMEM_LIMIT):
    return pltpu.CompilerParams(dimension_semantics=sem, vmem_limit_bytes=vmem)


def _matmul(name, a, b, extras, *, grid, a_spec, b_spec, extra_specs, out_shapes, out_specs,
            contract, epilogue, acc_shape):
    nk = grid[2]
    ne, no = len(extras), len(out_shapes)

    def body(a_ref, b_ref, *rest):
        ex, outs, acc = rest[:ne], rest[ne:ne + no], rest[ne + no]
        kk = pl.program_id(2)

        @pl.when(kk == 0)
        def _():
            acc[...] = jnp.zeros_like(acc)

        acc[...] += lax.dot_general(a_ref[...].astype(BF16), b_ref[...].astype(BF16),
                                    (contract, ((), ())), preferred_element_type=F32)

        @pl.when(kk == nk - 1)
        def _():
            res = epilogue(acc[...], *[r[...] for r in ex])
            for r, o in zip(outs, res):
                r[...] = o.astype(r.dtype)

    return pl.pallas_call(
        body, grid=grid, in_specs=[a_spec, b_spec, *extra_specs], out_specs=out_specs,
        out_shape=out_shapes, scratch_shapes=[pltpu.VMEM(acc_shape, F32)],
        compiler_params=_cp(("parallel", "parallel", "arbitrary")), name=name)(a, b, *extras)


def _ident(acc):
    return (acc,)


def mm_nn(name, a, w, *, out_dtype, tm=1024, tn=1024, tk=512, extras=(), epilogue=_ident, n_out=1,
          out_dtypes=None):
    m, k = a.shape
    n = w.shape[1]
    tm, tn, tk = min(tm, m), min(tn, n), min(tk, k)
    out_dtypes = out_dtypes or (out_dtype,) * n_out
    mn = pl.BlockSpec((tm, tn), lambda i, j, kk: (i, j))
    return _matmul(name, a, w, extras, grid=(m // tm, n // tn, k // tk),
                   a_spec=pl.BlockSpec((tm, tk), lambda i, j, kk: (i, kk)),
                   b_spec=pl.BlockSpec((tk, tn), lambda i, j, kk: (kk, j)),
                   extra_specs=[mn] * len(extras),
                   out_shapes=[jax.ShapeDtypeStruct((m, n), dt) for dt in out_dtypes],
                   out_specs=[mn] * len(out_dtypes), contract=((1,), (0,)), epilogue=epilogue,
                   acc_shape=(tm, tn))


def mm_nt(name, a, w, *, out_dtype, tm=1024, tn=1024, tk=512, extras=(), epilogue=_ident):
    m, n = a.shape
    k = w.shape[0]
    tm, to, tc = min(tm, m), min(tn, k), min(tk, n)
    mo = pl.BlockSpec((tm, to), lambda i, j, kk: (i, j))
    return _matmul(name, a, w, extras, grid=(m // tm, k // to, n // tc),
                   a_spec=pl.BlockSpec((tm, tc), lambda i, j, kk: (i, kk)),
                   b_spec=pl.BlockSpec((to, tc), lambda i, j, kk: (j, kk)),
                   extra_specs=[mo] * len(extras),
                   out_shapes=[jax.ShapeDtypeStruct((m, k), out_dtype)],
                   out_specs=[mo], contract=((1,), (1,)), epilogue=epilogue, acc_shape=(tm, to))[0]


def mm_tn(name, a, b, *, out_dtype=BF16, tm=1024, tn=1024, tk=512):
    t, k = a.shape
    n = b.shape[1]
    tm, tn, tk = min(tm, k), min(tn, n), min(tk, t)
    return _matmul(name, a, b, (), grid=(k // tm, n // tn, t // tk),
                   a_spec=pl.BlockSpec((tk, tm), lambda i, j, kk: (kk, i)),
                   b_spec=pl.BlockSpec((tk, tn), lambda i, j, kk: (kk, j)),
                   extra_specs=[], out_shapes=[jax.ShapeDtypeStruct((k, n), out_dtype)],
                   out_specs=[pl.BlockSpec((tm, tn), lambda i, j, kk: (i, j))],
                   contract=((0,), (0,)), epilogue=_ident, acc_shape=(tm, tn))[0]


ROWS = 256


def rmsnorm_fwd(name, x, gain):
    def body(x_ref, g_ref, h_ref):
        xf = x_ref[...]
        rstd = lax.rsqrt(jnp.mean(xf * xf, axis=-1, keepdims=True) + RMS_EPS)
        h_ref[...] = (xf * rstd * g_ref[...]).astype(BF16)

    row = pl.BlockSpec((ROWS, D), lambda i: (i, 0))
    return pl.pallas_call(body, grid=(T // ROWS,), in_specs=[row, pl.BlockSpec((1, D), lambda i: (0, 0))],
                          out_specs=row, out_shape=jax.ShapeDtypeStruct((T, D), BF16),
                          compiler_params=_cp(("parallel",)), name=name)(x, gain)


def _rms_bwd_math(dh, xf, gain):
    rstd = lax.rsqrt(jnp.mean(xf * xf, axis=-1, keepdims=True) + RMS_EPS)
    xhat = xf * rstd
    dgain = jnp.sum(dh * xhat, axis=0, keepdims=True)
    dxhat = dh * gain
    dx = rstd * (dxhat - xhat * jnp.mean(dxhat * xhat, axis=-1, keepdims=True))
    return dx, dgain


def rmsnorm_bwd(name, dh, x, gain, dres):
    def body(dh_ref, x_ref, g_ref, dres_ref, dx_ref, dxb_ref, dg_ref):
        dx, dgain = _rms_bwd_math(dh_ref[...], x_ref[...], g_ref[...])
        dx = dx + dres_ref[...]
        dx_ref[...] = dx
        dxb_ref[...] = dx.astype(BF16)

        @pl.when(pl.program_id(0) == 0)
        def _():
            dg_ref[...] = jnp.zeros_like(dg_ref)

        dg_ref[...] += dgain

    row = pl.BlockSpec((ROWS, D), lambda i: (i, 0))
    vec = pl.BlockSpec((1, D), lambda i: (0, 0))
    return pl.pallas_call(
        body, grid=(T // ROWS,), in_specs=[row, row, vec, row], out_specs=[row, row, vec],
        out_shape=[jax.ShapeDtypeStruct((T, D), F32), jax.ShapeDtypeStruct((T, D), BF16),
                   jax.ShapeDtypeStruct((1, D), F32)],
        compiler_params=_cp(("arbitrary",)), name=name)(dh, x, gain, dres)


def loss_head(name, x, gain, target):
    def body(x_ref, g_ref, t_ref, dx_ref, dxb_ref, dg_ref, loss_ref):
        xf, gain = x_ref[...], g_ref[...]
        rstd = lax.rsqrt(jnp.mean(xf * xf, axis=-1, keepdims=True) + RMS_EPS)
        err = xf * rstd * gain - t_ref[...]
        part = 0.5 * jnp.sum(jnp.mean(err * err, axis=-1, keepdims=True), axis=0, keepdims=True)
        dx, dgain = _rms_bwd_math(err * (1.0 / D), xf, gain)
        dx_ref[...] = dx
        dxb_ref[...] = dx.astype(BF16)

        @pl.when(pl.program_id(0) == 0)
        def _():
            dg_ref[...] = jnp.zeros_like(dg_ref)
            loss_ref[...] = jnp.zeros_like(loss_ref)

        dg_ref[...] += dgain
        loss_ref[...] += jnp.broadcast_to(part, loss_ref.shape)

    row = pl.BlockSpec((ROWS, D), lambda i: (i, 0))
    vec = pl.BlockSpec((1, D), lambda i: (0, 0))
    return pl.pallas_call(
        body, grid=(T // ROWS,), in_specs=[row, vec, row],
        out_specs=[row, row, vec, pl.BlockSpec((1, 128), lambda i: (0, 0))],
        out_shape=[jax.ShapeDtypeStruct((T, D), F32), jax.ShapeDtypeStruct((T, D), BF16),
                   jax.ShapeDtypeStruct((1, D), F32), jax.ShapeDtypeStruct((1, 128), F32)],
        compiler_params=_cp(("arbitrary",)), name=name)(x, gain, target)


def _a_common(n, g, s_ref, q_ref, kc_ref, kp_ref):
    row = lax.broadcasted_iota(jnp.int32, (GROUP_A * BLK, BLK), 0)
    col = lax.broadcasted_iota(jnp.int32, (GROUP_A * BLK, BLK), 1)
    qi = row & (BLK - 1)
    dist_c = (qi - col).astype(F32)
    valid_c = qi >= col
    valid_p = jnp.logical_and(col > qi, n > 0)
    hidx = lax.broadcasted_iota(jnp.int32, (GROUP_A * BLK, 1), 0) >> 7
    slope = jnp.exp((-math.log(2.0) / 4.0) * (hidx + (GROUP_A * g + 1)).astype(F32))
    sink = jnp.zeros((GROUP_A * BLK, 1), F32)
    for i in range(GROUP_A):
        sink = jnp.where(hidx == i, s_ref[GROUP_A * g + i], sink)
    qg = jnp.concatenate([q_ref[:, (GROUP_A * g + i) * HD:(GROUP_A * g + i + 1) * HD]
                          for i in range(GROUP_A)], axis=0)
    kc = kc_ref[:, g * HD:(g + 1) * HD]
    kp = kp_ref[:, g * HD:(g + 1) * HD]
    nt = (((1,), (1,)), ((), ()))
    sc = lax.dot_general(qg, kc, nt, preferred_element_type=F32) * SCALE - slope * dist_c
    sp = lax.dot_general(qg, kp, nt, preferred_element_type=F32) * SCALE - slope * (dist_c + float(BLK))
    sc = jnp.where(valid_c, sc, NEG)
    sp = jnp.where(valid_p, sp, NEG)
    m = jnp.maximum(jnp.maximum(jnp.max(sc, axis=-1, keepdims=True), jnp.max(sp, axis=-1, keepdims=True)), sink)
    pc = jnp.exp(sc - m)
    pp = jnp.exp(sp - m)
    ps = jnp.exp(sink - m)
    den = jnp.sum(pc, axis=-1, keepdims=True) + jnp.sum(pp, axis=-1, keepdims=True) + ps
    return qg, kc, kp, pc, pp, ps, den, hidx


def attn_a_fwd(name, qkv, sinks):
    kcol, vcol = NQ_A * HD // (NKV_A * HD), NQ_A * HD // (NKV_A * HD) + 1

    def body(s_ref, q_ref, kc_ref, kp_ref, vc_ref, vp_ref, o_ref):
        n = pl.program_id(0)
        for g in range(NKV_A):
            qg, kc, kp, pc, pp, ps, den, _ = _a_common(n, g, s_ref, q_ref, kc_ref, kp_ref)
            vc = vc_ref[:, g * HD:(g + 1) * HD]
            vp = vp_ref[:, g * HD:(g + 1) * HD]
            og = (jnp.dot(pc.astype(BF16), vc, preferred_element_type=F32)
                  + jnp.dot(pp.astype(BF16), vp, preferred_element_type=F32)) / den
            for i in range(GROUP_A):
                h = GROUP_A * g + i
                o_ref[:, h * HD:(h + 1) * HD] = og[i * BLK:(i + 1) * BLK].astype(BF16)

    kvw = NKV_A * HD
    prev = lambda n: jnp.maximum(n - 1, 0)
    return pl.pallas_call(
        body, grid=(T // BLK,),
        in_specs=[pl.BlockSpec(memory_space=pltpu.SMEM),
                  pl.BlockSpec((BLK, NQ_A * HD), lambda n: (n, 0)),
                  pl.BlockSpec((BLK, kvw), lambda n: (n, kcol)),
                  pl.BlockSpec((BLK, kvw), lambda n: (prev(n), kcol)),
                  pl.BlockSpec((BLK, kvw), lambda n: (n, vcol)),
                  pl.BlockSpec((BLK, kvw), lambda n: (prev(n), vcol))],
        out_specs=pl.BlockSpec((BLK, NQ_A * HD), lambda n: (n, 0)),
        out_shape=jax.ShapeDtypeStruct((T, NQ_A * HD), BF16),
        compiler_params=_cp(("parallel",)), name=name)(sinks, qkv, qkv, qkv, qkv, qkv)


def attn_a_bwd(name, qkv, sinks, o, do):
    kcol, vcol = NQ_A * HD // (NKV_A * HD), NQ_A * HD // (NKV_A * HD) + 1
    nb = T // BLK
    kvw = NKV_A * HD

    def body(s_ref, q_ref, kc_ref, kp_ref, vc_ref, vp_ref, o_ref, do_ref,
             dq_ref, dk_ref, dv_ref, ds_ref, ck_ref, cv_ref):
        n = pl.program_id(0)

        @pl.when(n == 0)
        def _():
            ds_ref[...] = jnp.zeros_like(ds_ref)
            ck_ref[...] = jnp.zeros_like(ck_ref)
            cv_ref[...] = jnp.zeros_like(cv_ref)

        @pl.when(n == nb)
        def _():
            dk_ref[...] = ck_ref[...].astype(BF16)
            dv_ref[...] = cv_ref[...].astype(BF16)

        @pl.when(n < nb)
        def _():
            lane = lax.broadcasted_iota(jnp.int32, (1, 128), 1)
            dsink_row = jnp.zeros((1, 128), F32)
            tn = (((0,), (0,)), ((), ()))
            nt = (((1,), (1,)), ((), ()))
            for g in range(NKV_A):
                qg, kc, kp, pc, pp, ps, den, hidx = _a_common(n, g, s_ref, q_ref, kc_ref, kp_ref)
                vc = vc_ref[:, g * HD:(g + 1) * HD]
                vp = vp_ref[:, g * HD:(g + 1) * HD]
                cols = [slice((GROUP_A * g + i) * HD, (GROUP_A * g + i + 1) * HD) for i in range(GROUP_A)]
                og = jnp.concatenate([o_ref[:, c] for c in cols], axis=0)
                dog = jnp.concatenate([do_ref[:, c] for c in cols], axis=0)
                delta = jnp.sum(og.astype(F32) * dog.astype(F32), axis=-1, keepdims=True)
                inv = 1.0 / den
                p_c, p_p = pc * inv, pp * inv
                dsc = p_c * (lax.dot_general(dog, vc, nt, preferred_element_type=F32) - delta)
                dsp = p_p * (lax.dot_general(dog, vp, nt, preferred_element_type=F32) - delta)
                dsk = -(ps * inv) * delta
                for i in range(GROUP_A):
                    tot = jnp.sum(jnp.where(hidx == i, dsk, 0.0), axis=0, keepdims=True)
                    dsink_row = dsink_row + jnp.where(lane == GROUP_A * g + i, tot, 0.0)
                dscb, dspb = (dsc * SCALE).astype(BF16), (dsp * SCALE).astype(BF16)
                dqg = (jnp.dot(dscb, kc, preferred_element_type=F32)
                       + jnp.dot(dspb, kp, preferred_element_type=F32))
                for i in range(GROUP_A):
                    dq_ref[:, cols[i]] = dqg[i * BLK:(i + 1) * BLK].astype(BF16)
                gs = slice(g * HD, (g + 1) * HD)
                dk_ref[:, gs] = (ck_ref[:, gs] + lax.dot_general(dspb, qg, tn, preferred_element_type=F32)).astype(BF16)
                dv_ref[:, gs] = (cv_ref[:, gs] + lax.dot_general(p_p.astype(BF16), dog, tn, preferred_element_type=F32)).astype(BF16)
                ck_ref[:, gs] = lax.dot_general(dscb, qg, tn, preferred_element_type=F32)
                cv_ref[:, gs] = lax.dot_general(p_c.astype(BF16), dog, tn, preferred_element_type=F32)
            ds_ref[...] += jnp.broadcast_to(dsink_row, ds_ref.shape)

    cur = lambda n: jnp.minimum(n, nb - 1)
    prev = lambda n: jnp.maximum(jnp.minimum(n, nb - 1) - 1, 0)
    outp = lambda n: jnp.maximum(n - 1, 0)
    qspec = pl.BlockSpec((BLK, NQ_A * HD), lambda n: (cur(n), 0))
    return pl.pallas_call(
        body, grid=(nb + 1,),
        in_specs=[pl.BlockSpec(memory_space=pltpu.SMEM), qspec,
                  pl.BlockSpec((BLK, kvw), lambda n: (cur(n), kcol)),
                  pl.BlockSpec((BLK, kvw), lambda n: (prev(n), kcol)),
                  pl.BlockSpec((BLK, kvw), lambda n: (cur(n), vcol)),
                  pl.BlockSpec((BLK, kvw), lambda n: (prev(n), vcol)),
                  qspec, qspec],
        out_specs=[qspec, pl.BlockSpec((BLK, kvw), lambda n: (outp(n), 0)),
                   pl.BlockSpec((BLK, kvw), lambda n: (outp(n), 0)),
                   pl.BlockSpec((8, 128), lambda n: (0, 0))],
        out_shape=[jax.ShapeDtypeStruct((T, NQ_A * HD), BF16), jax.ShapeDtypeStruct((T, kvw), BF16),
                   jax.ShapeDtypeStruct((T, kvw), BF16), jax.ShapeDtypeStruct((8, 128), F32)],
        scratch_shapes=[pltpu.VMEM((BLK, kvw), F32), pltpu.VMEM((BLK, kvw), F32)],
        compiler_params=_cp(("arbitrary",)), name=name)(sinks, qkv, qkv, qkv, qkv, qkv, o, do)


def _b_logs(z):
    l1p = jnp.log(1.0 + jnp.exp(-jnp.abs(z)))
    return jnp.minimum(z, 0.0) - l1p, -jnp.maximum(z, 0.0) - l1p


def _split_dot(x, m):
    hi = x.astype(BF16)
    lo = (x - hi.astype(F32)).astype(BF16)
    return jnp.dot(hi, m, preferred_element_type=F32) + jnp.dot(lo, m, preferred_element_type=F32)


def _b_tile(j, diag, qh, k_ref, hs, c_t, mgt, tri):
    kj = k_ref[pl.ds(pl.multiple_of(j * BLK, BLK), BLK), hs]
    z = lax.dot_general(qh, kj, (((1,), (1,)), ((), ())), preferred_element_type=F32) * SCALE
    lb, lm = _b_logs(z)
    if diag:
        lm = jnp.where(tri, lm, 0.0)
    a = jnp.exp(lb + c_t + _split_dot(lm, mgt))
    if diag:
        a = jnp.where(tri, a, 0.0)
    return z, a, c_t + jnp.sum(lm, axis=-1, keepdims=True)


def _b_masks():
    r = lax.broadcasted_iota(jnp.int32, (BLK, BLK), 0)
    c = lax.broadcasted_iota(jnp.int32, (BLK, BLK), 1)
    return (r > c).astype(BF16), (r < c).astype(BF16), c < r


def attn_b_fwd(name, qkv):
    npair = NH_B // 2

    def body(q_ref, k_ref, v_ref, o_ref):
        n = pl.program_id(1)
        mgt, _, tri = _b_masks()
        for h in range(2):
            hs = slice(h * HD, (h + 1) * HD)
            qh = q_ref[:, hs]

            def step(j, carry, diag):
                c_t, acc = carry
                _, a, c_t = _b_tile(j, diag, qh, k_ref, hs, c_t, mgt, tri)
                vj = v_ref[pl.ds(pl.multiple_of(j * BLK, BLK), BLK), hs]
                return c_t, acc + jnp.dot(a.astype(BF16), vj, preferred_element_type=F32)

            carry = step(n, (jnp.zeros((BLK, 1), F32), jnp.zeros((BLK, HD), F32)), True)
            carry = lax.fori_loop(0, n, lambda i, cr: step(n - 1 - i, cr, False), carry)
            o_ref[:, hs] = carry[1].astype(BF16)

    return pl.pallas_call(
        body, grid=(npair, T // BLK),
        in_specs=[pl.BlockSpec((BLK, 2 * HD), lambda p, n: (n, p)),
                  pl.BlockSpec((T, 2 * HD), lambda p, n: (0, npair + p)),
                  pl.BlockSpec((T, 2 * HD), lambda p, n: (0, 2 * npair + p))],
        out_specs=pl.BlockSpec((BLK, 2 * HD), lambda p, n: (n, p)),
        out_shape=jax.ShapeDtypeStruct((T, NH_B * HD), BF16),
        compiler_params=_cp(("parallel", "arbitrary")), name=name)(qkv, qkv, qkv)


def attn_b_bwd(name, qkv, do):
    npair = NH_B // 2
    nb = T // BLK

    def body(q_ref, do_ref, k_ref, v_ref, dq_ref, dk_ref, dv_ref, a_s, z_s):
        n = pl.program_id(1)

        @pl.when(n == 0)
        def _():
            dk_ref[...] = jnp.zeros_like(dk_ref)
            dv_ref[...] = jnp.zeros_like(dv_ref)

        mgt, mlt, tri = _b_masks()
        nt = (((1,), (1,)), ((), ()))
        tn = (((0,), (0,)), ((), ()))
        for h in range(2):
            hs = slice(h * HD, (h + 1) * HD)
            qh = q_ref[:, hs]
            doh = do_ref[:, hs]

            def sweep1(j, c_t, diag):
                z, a, c_t = _b_tile(j, diag, qh, k_ref, hs, c_t, mgt, tri)
                a_s[j] = a
                z_s[j] = z
                return c_t

            c0 = sweep1(n, jnp.zeros((BLK, 1), F32), True)
            lax.fori_loop(0, n, lambda i, c: sweep1(n - 1 - i, c, False), c0)

            def sweep2(j, carry, diag):
                cg, dq = carry
                rows = pl.ds(pl.multiple_of(j * BLK, BLK), BLK)
                kj, vj = k_ref[rows, hs], v_ref[rows, hs]
                a, z = a_s[j], z_s[j]
                g = a * lax.dot_general(doh, vj, nt, preferred_element_type=F32)
                big_g = cg + _split_dot(g, mlt)
                e = jnp.exp(-jnp.abs(z))
                inv = 1.0 / (1.0 + e)
                sig = jnp.where(z >= 0, inv, e * inv)
                dz = g * (1.0 - sig) - big_g * sig
                if diag:
                    dz = jnp.where(tri, dz, 0.0)
                dzb = (dz * SCALE).astype(BF16)
                dk_ref[rows, hs] += lax.dot_general(dzb, qh, tn, preferred_element_type=F32)
                dv_ref[rows, hs] += lax.dot_general(a.astype(BF16), doh, tn, preferred_element_type=F32)
                return (cg + jnp.sum(g, axis=-1, keepdims=True),
                        dq + jnp.dot(dzb, kj, preferred_element_type=F32))

            carry = (jnp.zeros((BLK, 1), F32), jnp.zeros((BLK, HD), F32))
            carry = lax.fori_loop(0, n, lambda j, cr: sweep2(j, cr, False), carry)
            carry = sweep2(n, carry, True)
            dq_ref[:, hs] = carry[1].astype(BF16)

    tile = pl.BlockSpec((BLK, 2 * HD), lambda p, n: (n, p))
    full = lambda off: pl.BlockSpec((T, 2 * HD), lambda p, n: (0, off + p))
    return pl.pallas_call(
        body, grid=(npair, nb),
        in_specs=[tile, tile, full(npair), full(2 * npair)],
        out_specs=[tile, full(0), full(0)],
        out_shape=[jax.ShapeDtypeStruct((T, NH_B * HD), BF16), jax.ShapeDtypeStruct((T, NH_B * HD), F32),
                   jax.ShapeDtypeStruct((T, NH_B * HD), F32)],
        scratch_shapes=[pltpu.VMEM((nb, BLK, BLK), F32), pltpu.VMEM((nb, BLK, BLK), F32)],
        compiler_params=_cp(("parallel", "arbitrary")), name=name)(qkv, do, qkv, qkv)


def exchange(name, ins, gather):
    nt = len(ins)

    def body(*refs):
        src, dst = refs[:nt], refs[nt:2 * nt]
        send_sems, recv_sems, local_sems = refs[2 * nt:]
        x, y, c = lax.axis_index("x"), lax.axis_index("y"), lax.axis_index("c")
        me = 4 * x + 2 * y + c
        peers = []
        for bits in range(1, NDEV):
            px = 1 - x if bits & 4 else x
            py = 1 - y if bits & 2 else y
            pc = 1 - c if bits & 1 else c
            peers.append((px, py, pc))
        local, remote = [], []
        for t in range(nt):
            own = src[t] if gather else src[t].at[me]
            lc = pltpu.make_async_copy(own, dst[t].at[me], local_sems.at[t])
            lc.start()
            local.append(lc)
            for k, (px, py, pc) in enumerate(peers):
                peer = 4 * px + 2 * py + pc
                cp = pltpu.make_async_remote_copy(
                    src_ref=src[t] if gather else src[t].at[peer], dst_ref=dst[t].at[me],
                    send_sem=send_sems.at[t * 7 + k], recv_sem=recv_sems.at[t * 7 + k],
                    device_id=(px, py, pc), device_id_type=MESH)
                cp.start()
                remote.append((cp, t, k, peer))
        for cp, t, k, peer in remote:
            pltpu.make_async_remote_copy(
                src_ref=src[t] if gather else src[t].at[peer], dst_ref=dst[t].at[peer],
                send_sem=send_sems.at[t * 7 + k], recv_sem=recv_sems.at[t * 7 + k],
                device_id=peers[k], device_id_type=MESH).wait_recv()
        for cp, _, _, _ in remote:
            cp.wait_send()
        for lc in local:
            lc.wait()

    hbm = pl.BlockSpec(memory_space=pltpu.HBM)
    out_shape = [jax.ShapeDtypeStruct((NDEV,) + (a.shape if gather else a.shape[1:]), a.dtype) for a in ins]
    return pl.pallas_call(
        body, in_specs=[hbm] * nt, out_specs=[hbm] * nt, out_shape=out_shape,
        scratch_shapes=[pltpu.SemaphoreType.DMA((7 * nt,)), pltpu.SemaphoreType.DMA((7 * nt,)),
                        pltpu.SemaphoreType.DMA((nt,))],
        name=name)(*ins)


def _adamw_math(w, g, m, v):
    m = B1 * m + (1.0 - B1) * g
    v = B2 * v + (1.0 - B2) * (g * g)
    m_hat = m / (1.0 - B1 ** STEP)
    v_hat = v / (1.0 - B2 ** STEP)
    delta = -LR * (m_hat / (jnp.sqrt(v_hat) + EPS) + WD * w)
    return delta, m, v


def adamw(name, parts, w, m, v, rows):
    r, c = w.shape

    def body(p_ref, w_ref, m_ref, v_ref, g_ref, d_ref, nm_ref, nv_ref):
        g = p_ref[0].astype(F32)
        for s in range(1, NDEV):
            g = g + p_ref[s].astype(F32)
        delta, nm, nv = _adamw_math(w_ref[...], g, m_ref[...], v_ref[...])
        g_ref[...] = g
        d_ref[...] = delta
        nm_ref[...] = nm
        nv_ref[...] = nv

    blk = pl.BlockSpec((rows, c), lambda i: (i, 0))
    return pl.pallas_call(
        body, grid=(r // rows,),
        in_specs=[pl.BlockSpec((NDEV, rows, c), lambda i: (0, i, 0)), blk, blk, blk],
        out_specs=[blk] * 4, out_shape=[jax.ShapeDtypeStruct((r, c), F32)] * 4,
        compiler_params=_cp(("parallel",)), name=name)(parts, w, m, v)


def _unshard_cols(g):
    return jnp.transpose(g, (1, 0, 2)).reshape(g.shape[1], -1)


def _shard_cols(w):
    k, n = w.shape
    return jnp.transpose(w.reshape(k, NDEV, n // NDEV), (1, 0, 2))


def kernel(x, a_w_qkv, a_w_o, a_sinks, b_w_qkv, b_w_o, norm_mix, norm_mlp, mlp_w_in, mlp_w_out, final_norm, loss_target, m_a_w_qkv, m_a_w_o, m_a_sinks, m_b_w_qkv, m_b_w_o, m_norm_mix, m_norm_mlp, m_mlp_w_in, m_mlp_w_out, m_final_norm, v_a_w_qkv, v_a_w_o, v_a_sinks, v_b_w_qkv, v_b_w_o, v_norm_mix, v_norm_mlp, v_mlp_w_in, v_mlp_w_out, v_final_norm):
    depth = 4
    xs = x[0]

    weights = []
    for i in range(depth):
        j = i // 2
        wq, wo = (a_w_qkv[j], a_w_o[j]) if i % 2 == 0 else (b_w_qkv[j], b_w_o[j])
        got = exchange(f"gather_w{i}", [wq.astype(BF16), wo.astype(BF16), mlp_w_in[i].astype(BF16),
                                        mlp_w_out[i].astype(BF16)], gather=True)
        weights.append((_unshard_cols(got[0]), got[1].reshape(D, D), _unshard_cols(got[2]),
                        got[3].reshape(DFF, D)))

    relu2 = lambda acc: (acc, jnp.square(jnp.maximum(acc, 0.0)))
    add = lambda acc, res: (acc + res,)

    saved = []
    for i in range(depth):
        w_qkv, w_o, w_in, w_out = weights[i]
        x1 = xs
        h1 = rmsnorm_fwd(f"norm_mix{i}", x1, norm_mix[i][None])
        qkv = mm_nn(f"qkv{i}", h1, w_qkv, out_dtype=BF16, tn=512 if i % 2 == 0 else 768)[0]
        if i % 2 == 0:
            o = attn_a_fwd(f"attn_a{i}", qkv, a_sinks[i // 2])
        else:
            o = attn_b_fwd(f"attn_b{i}", qkv)
        x2 = mm_nn(f"proj{i}", o, w_o, out_dtype=F32, extras=(x1,), epilogue=add)[0]
        h2 = rmsnorm_fwd(f"norm_mlp{i}", x2, norm_mlp[i][None])
        u, act = mm_nn(f"mlp_in{i}", h2, w_in, out_dtype=BF16, epilogue=relu2, n_out=2)
        xs = mm_nn(f"mlp_out{i}", act, w_out, out_dtype=F32, extras=(x2,), epilogue=add)[0]
        saved.append((x1, h1, qkv, o, x2, h2, u, act))

    dx, dxb, d_final, loss_part = loss_head("loss_head", xs, final_norm[None], loss_target[0])
    loss = lax.psum(loss_part[0, 0], ("x", "y", "c"))

    d_mix, d_mlp, d_sinks = [None] * depth, [None] * depth, [None] * 2
    recv = [None] * depth
    for i in reversed(range(depth)):
        w_qkv, w_o, w_in, w_out = weights[i]
        x1, h1, qkv, o, x2, h2, u, act = saved[i]
        g_out = mm_tn(f"g_mlp_out{i}", act, dxb)
        du = mm_nt(f"d_act{i}", dxb, w_out, out_dtype=BF16, extras=(u,),
                   epilogue=lambda acc, uu: (acc * (2.0 * jnp.maximum(uu.astype(F32), 0.0)),))
        g_in = mm_tn(f"g_mlp_in{i}", h2, du)
        dh2 = mm_nt(f"d_h2{i}", du, w_in, out_dtype=F32)
        dx, dxb, d_mlp[i] = rmsnorm_bwd(f"norm_mlp_bwd{i}", dh2, x2, norm_mlp[i][None], dx)
        g_o = mm_tn(f"g_proj{i}", o, dxb)
        do = mm_nt(f"d_o{i}", dxb, w_o, out_dtype=BF16)
        if i % 2 == 0:
            dq, dk, dv, ds = attn_a_bwd(f"attn_a_bwd{i}", qkv, a_sinks[i // 2], o, do)
            d_sinks[i // 2] = ds[0, :NQ_A]
            dqkv = jnp.concatenate([dq, dk, dv], axis=1)
        else:
            dq, dk, dv = attn_b_bwd(f"attn_b_bwd{i}", qkv, do)
            dqkv = jnp.concatenate([dq, dk.astype(BF16), dv.astype(BF16)], axis=1)
        g_qkv = mm_tn(f"g_qkv{i}", h1, dqkv, tn=512 if i % 2 == 0 else 768)
        dh1 = mm_nt(f"d_h1{i}", dqkv, w_qkv, out_dtype=F32)
        dx, dxb, d_mix[i] = rmsnorm_bwd(f"norm_mix_bwd{i}", dh1, x1, norm_mix[i][None], dx)
        recv[i] = exchange(f"scatter_g{i}", [_shard_cols(g_qkv), g_o.reshape(NDEV, D // NDEV, D),
                                             _shard_cols(g_in), g_out.reshape(NDEV, DFF // NDEV, D)],
                           gather=False)

    small = jnp.zeros((16, D), F32)
    small = small.at[0:4].set(jnp.concatenate(d_mix, axis=0)).at[4:8].set(jnp.concatenate(d_mlp, axis=0))
    small = small.at[8].set(d_final[0]).at[9, :2 * NQ_A].set(jnp.concatenate(d_sinks))
    small_parts = exchange("gather_small", [small], gather=True)[0]
    pack = lambda mix, mlp, fin, snk: (jnp.zeros((16, D), F32).at[0:4].set(mix).at[4:8].set(mlp)
                                       .at[8].set(fin).at[9, :2 * NQ_A].set(snk.reshape(-1)))
    sm = adamw("adamw_small", small_parts, pack(norm_mix, norm_mlp, final_norm, a_sinks),
               pack(m_norm_mix, m_norm_mlp, m_final_norm, m_a_sinks),
               pack(v_norm_mix, v_norm_mlp, v_final_norm, v_a_sinks), rows=16)
    unpack = lambda a: (a[0:4], a[4:8], a[8], a[9, :2 * NQ_A].reshape(2, NQ_A))

    def layer_update(i, t, w, m, v, rows):
        return adamw(f"adamw{i}_{t}", recv[i][t], w, m, v, rows)

    res = {}
    for i in range(depth):
        j = i // 2
        if i % 2 == 0:
            res[("qkv", i)] = layer_update(i, 0, a_w_qkv[j], m_a_w_qkv[j], v_a_w_qkv[j], 256)
            res[("o", i)] = layer_update(i, 1, a_w_o[j], m_a_w_o[j], v_a_w_o[j], 128)
        else:
            res[("qkv", i)] = layer_update(i, 0, b_w_qkv[j], m_b_w_qkv[j], v_b_w_qkv[j], 256)
            res[("o", i)] = layer_update(i, 1, b_w_o[j], m_b_w_o[j], v_b_w_o[j], 128)
        res[("in", i)] = layer_update(i, 2, mlp_w_in[i], m_mlp_w_in[i], v_mlp_w_in[i], 256)
        res[("out", i)] = layer_update(i, 3, mlp_w_out[i], m_mlp_w_out[i], v_mlp_w_out[i], 128)

    outs = []
    for q in range(4):
        mix, mlp, fin, snk = unpack(sm[q])
        outs.append([
            jnp.stack([res[("qkv", 0)][q], res[("qkv", 2)][q]]),
            jnp.stack([res[("o", 0)][q], res[("o", 2)][q]]),
            snk,
            jnp.stack([res[("qkv", 1)][q], res[("qkv", 3)][q]]),
            jnp.stack([res[("o", 1)][q], res[("o", 3)][q]]),
            mix, mlp,
            jnp.stack([res[("in", i)][q] for i in range(depth)]),
            jnp.stack([res[("out", i)][q] for i in range(depth)]),
            fin,
        ])
    return (loss, dx[None], *outs[0], *outs[1], *outs[2], *outs[3])
```

```python
import functools
import math

import jax
import jax.numpy as jnp
from jax import lax
from jax.experimental import pallas as pl
from jax.experimental.pallas import tpu as pltpu

F32, BF16 = jnp.float32, jnp.bfloat16
T, D, DFF = 4096, 2048, 8192
HD, BLK = 64, 128
NQ_A, NKV_A, GROUP_A = 32, 4, 8
NH_B = 32
NDEV = 8
RMS_EPS = 1e-5
SCALE = 0.125
NEG = -1e30
VMEM_LIMIT = 56 * 1024 * 1024
LR, B1, B2, EPS, WD, STEP = 0.001, 0.9, 0.999, 1e-08, 0.01, 10
MESH = pl.DeviceIdType.MESH


def _cp(sem, vmem=VMEM_LIMIT):
    return pltpu.CompilerParams(dimension_semantics=sem, vmem_limit_bytes=vmem)


class Comm:
    def __init__(self, ins, gather):
        self.ins, self.nt = list(ins), len(ins)
        self.gather = list(gather) if isinstance(gather, (list, tuple)) else [gather] * self.nt
        self.out_shape = [jax.ShapeDtypeStruct((NDEV,) + (a.shape if g else a.shape[1:]), a.dtype)
                          for a, g in zip(self.ins, self.gather)]
        self.scratch = [pltpu.SemaphoreType.DMA((7 * self.nt,)), pltpu.SemaphoreType.DMA((7 * self.nt,)),
                        pltpu.SemaphoreType.DMA((self.nt,))]

    def _copies(self, src, dst, sems):
        send_sems, recv_sems, local_sems = sems
        x, y, c = lax.axis_index("x"), lax.axis_index("y"), lax.axis_index("c")
        me = 4 * x + 2 * y + c
        local, sends, recvs = [], [], []
        for t in range(self.nt):
            own = src[t] if self.gather[t] else src[t].at[me]
            local.append(pltpu.make_async_copy(own, dst[t].at[me], local_sems.at[t]))
            for k in range(NDEV - 1):
                bits = k + 1
                px = 1 - x if bits & 4 else x
                py = 1 - y if bits & 2 else y
                pc = 1 - c if bits & 1 else c
                peer = 4 * px + 2 * py + pc
                out = src[t] if self.gather[t] else src[t].at[peer]
                pair = dict(send_sem=send_sems.at[t * 7 + k], recv_sem=recv_sems.at[t * 7 + k],
                            device_id=(px, py, pc), device_id_type=MESH)
                sends.append(pltpu.make_async_remote_copy(src_ref=out, dst_ref=dst[t].at[me], **pair))
                recvs.append(pltpu.make_async_remote_copy(src_ref=out, dst_ref=dst[t].at[peer], **pair))
        return local, sends, recvs

    def start(self, src, dst, sems):
        local, sends, _ = self._copies(src, dst, sems)
        for cp in local + sends:
            cp.start()

    def wait(self, src, dst, sems):
        local, sends, recvs = self._copies(src, dst, sems)
        for cp in recvs:
            cp.wait_recv()
        for cp in sends:
            cp.wait_send()
        for cp in local:
            cp.wait()


def _pcall(name, body, *, grid, in_specs, out_specs, out_shape, scratch_shapes=(), sem, args, comm=None):
    in_specs, out_specs, out_shape = list(in_specs), list(out_specs), list(out_shape)
    scratch_shapes = list(scratch_shapes)
    if comm is None:
        outs = pl.pallas_call(body, grid=grid, in_specs=in_specs, out_specs=out_specs, out_shape=out_shape,
                              scratch_shapes=scratch_shapes, compiler_params=_cp(sem), name=name)(*args)
        return list(outs), []
    ni, no, ns, nt = len(in_specs), len(out_specs), len(scratch_shapes), comm.nt
    hbm = pl.BlockSpec(memory_space=pltpu.HBM)

    def carrier(*refs):
        ins, cin = refs[:ni], refs[ni:ni + nt]
        outs, cout = refs[ni + nt:ni + nt + no], refs[ni + nt + no:ni + 2 * nt + no]
        scr, sems = refs[ni + 2 * nt + no:ni + 2 * nt + no + ns], refs[ni + 2 * nt + no + ns:]
        first = functools.reduce(jnp.logical_and, [pl.program_id(d) == 0 for d in range(len(grid))])
        last = functools.reduce(jnp.logical_and, [pl.program_id(d) == grid[d] - 1 for d in range(len(grid))])

        @pl.when(first)
        def _():
            comm.start(cin, cout, sems)

        body(*ins, *outs, *scr)

        @pl.when(last)
        def _():
            comm.wait(cin, cout, sems)

    outs = pl.pallas_call(
        carrier, grid=grid, in_specs=in_specs + [hbm] * nt, out_specs=out_specs + [hbm] * nt,
        out_shape=out_shape + comm.out_shape, scratch_shapes=scratch_shapes + comm.scratch,
        compiler_params=_cp(("arbitrary",) * len(grid)), name=name)(*args, *comm.ins)
    return list(outs[:no]), list(outs[no:])


def exchange(name, ins, gather):
    comm = Comm(ins, gather)
    hbm = pl.BlockSpec(memory_space=pltpu.HBM)

    def body(*refs):
        src, dst, sems = refs[:comm.nt], refs[comm.nt:2 * comm.nt], refs[2 * comm.nt:]
        comm.start(src, dst, sems)
        comm.wait(src, dst, sems)

    return pl.pallas_call(body, in_specs=[hbm] * comm.nt, out_specs=[hbm] * comm.nt, out_shape=comm.out_shape,
                          scratch_shapes=comm.scratch, name=name)(*ins)


def _matmul(name, a, b, extras, *, grid, a_spec, b_spec, extra_specs, out_shapes, out_specs,
            contract, epilogue, acc_shape, comm=None):
    nk = grid[2]
    ne, no = len(extras), len(out_shapes)

    def body(a_ref, b_ref, *rest):
        ex, outs, acc = rest[:ne], rest[ne:ne + no], rest[ne + no]
        kk = pl.program_id(2)

        @pl.when(kk == 0)
        def _():
            acc[...] = jnp.zeros_like(acc)

        acc[...] += lax.dot_general(a_ref[...].astype(BF16), b_ref[...].astype(BF16),
                                    (contract, ((), ())), preferred_element_type=F32)

        @pl.when(kk == nk - 1)
        def _():
            res = epilogue(acc[...], *[r[...] for r in ex])
            for r, o in zip(outs, res):
                r[...] = o.astype(r.dtype)

    return _pcall(name, body, grid=grid, in_specs=[a_spec, b_spec, *extra_specs], out_specs=out_specs,
                  out_shape=out_shapes, scratch_shapes=[pltpu.VMEM(acc_shape, F32)],
                  sem=("parallel", "parallel", "arbitrary"), args=(a, b, *extras), comm=comm)


def _ident(acc):
    return (acc,)


def mm_nn(name, a, w, *, out_dtype, tm=1024, tn=1024, tk=512, extras=(), epilogue=_ident, n_out=1,
          out_dtypes=None, comm=None):
    m, k = a.shape
    n = w.shape[1]
    tm, tn, tk = min(tm, m), min(tn, n), min(tk, k)
    out_dtypes = out_dtypes or (out_dtype,) * n_out
    mn = pl.BlockSpec((tm, tn), lambda i, j, kk: (i, j))
    return _matmul(name, a, w, extras, grid=(m // tm, n // tn, k // tk),
                   a_spec=pl.BlockSpec((tm, tk), lambda i, j, kk: (i, kk)),
                   b_spec=pl.BlockSpec((tk, tn), lambda i, j, kk: (kk, j)),
                   extra_specs=[mn] * len(extras),
                   out_shapes=[jax.ShapeDtypeStruct((m, n), dt) for dt in out_dtypes],
                   out_specs=[mn] * len(out_dtypes), contract=((1,), (0,)), epilogue=epilogue,
                   acc_shape=(tm, tn), comm=comm)


def mm_nt(name, a, w, *, out_dtype, tm=1024, tn=1024, tk=512, extras=(), epilogue=_ident, comm=None):
    m, n = a.shape
    k = w.shape[0]
    tm, to, tc = min(tm, m), min(tn, k), min(tk, n)
    mo = pl.BlockSpec((tm, to), lambda i, j, kk: (i, j))
    outs, got = _matmul(name, a, w, extras, grid=(m // tm, k // to, n // tc),
                        a_spec=pl.BlockSpec((tm, tc), lambda i, j, kk: (i, kk)),
                        b_spec=pl.BlockSpec((to, tc), lambda i, j, kk: (j, kk)),
                        extra_specs=[mo] * len(extras),
                        out_shapes=[jax.ShapeDtypeStruct((m, k), out_dtype)],
                        out_specs=[mo], contract=((1,), (1,)), epilogue=epilogue, acc_shape=(tm, to),
                        comm=comm)
    return outs[0], got


def mm_tn(name, a, b, *, out_dtype=BF16, tm=1024, tn=1024, tk=512, comm=None):
    t, k = a.shape
    n = b.shape[1]
    tm, tn, tk = min(tm, k), min(tn, n), min(tk, t)
    outs, got = _matmul(name, a, b, (), grid=(k // tm, n // tn, t // tk),
                        a_spec=pl.BlockSpec((tk, tm), lambda i, j, kk: (kk, i)),
                        b_spec=pl.BlockSpec((tk, tn), lambda i, j, kk: (kk, j)),
                        extra_specs=[], out_shapes=[jax.ShapeDtypeStruct((k, n), out_dtype)],
                        out_specs=[pl.BlockSpec((tm, tn), lambda i, j, kk: (i, j))],
                        contract=((0,), (0,)), epilogue=_ident, acc_shape=(tm, tn), comm=comm)
    return outs[0], got


ROWS = 256


def rmsnorm_fwd(name, x, gain):
    def body(x_ref, g_ref, h_ref):
        xf = x_ref[...]
        rstd = lax.rsqrt(jnp.mean(xf * xf, axis=-1, keepdims=True) + RMS_EPS)
        h_ref[...] = (xf * rstd * g_ref[...]).astype(BF16)

    row = pl.BlockSpec((ROWS, D), lambda i: (i, 0))
    return pl.pallas_call(body, grid=(T // ROWS,), in_specs=[row, pl.BlockSpec((1, D), lambda i: (0, 0))],
                          out_specs=row, out_shape=jax.ShapeDtypeStruct((T, D), BF16),
                          compiler_params=_cp(("parallel",)), name=name)(x, gain)


def _rms_bwd_math(dh, xf, gain):
    rstd = lax.rsqrt(jnp.mean(xf * xf, axis=-1, keepdims=True) + RMS_EPS)
    xhat = xf * rstd
    dgain = jnp.sum(dh * xhat, axis=0, keepdims=True)
    dxhat = dh * gain
    dx = rstd * (dxhat - xhat * jnp.mean(dxhat * xhat, axis=-1, keepdims=True))
    return dx, dgain


def rmsnorm_bwd(name, dh, x, gain, dres):
    def body(dh_ref, x_ref, g_ref, dres_ref, dx_ref, dxb_ref, dg_ref):
        dx, dgain = _rms_bwd_math(dh_ref[...], x_ref[...], g_ref[...])
        dx = dx + dres_ref[...]
        dx_ref[...] = dx
        dxb_ref[...] = dx.astype(BF16)

        @pl.when(pl.program_id(0) == 0)
        def _():
            dg_ref[...] = jnp.zeros_like(dg_ref)

        dg_ref[...] += dgain

    row = pl.BlockSpec((ROWS, D), lambda i: (i, 0))
    vec = pl.BlockSpec((1, D), lambda i: (0, 0))
    return pl.pallas_call(
        body, grid=(T // ROWS,), in_specs=[row, row, vec, row], out_specs=[row, row, vec],
        out_shape=[jax.ShapeDtypeStruct((T, D), F32), jax.ShapeDtypeStruct((T, D), BF16),
                   jax.ShapeDtypeStruct((1, D), F32)],
        compiler_params=_cp(("arbitrary",)), name=name)(dh, x, gain, dres)


def loss_head(name, x, gain, target):
    def body(x_ref, g_ref, t_ref, dx_ref, dxb_ref, dg_ref, loss_ref):
        xf, gain = x_ref[...], g_ref[...]
        rstd = lax.rsqrt(jnp.mean(xf * xf, axis=-1, keepdims=True) + RMS_EPS)
        err = xf * rstd * gain - t_ref[...]
        part = 0.5 * jnp.sum(jnp.mean(err * err, axis=-1, keepdims=True), axis=0, keepdims=True)
        dx, dgain = _rms_bwd_math(err * (1.0 / D), xf, gain)
        dx_ref[...] = dx
        dxb_ref[...] = dx.astype(BF16)

        @pl.when(pl.program_id(0) == 0)
        def _():
            dg_ref[...] = jnp.zeros_like(dg_ref)
            loss_ref[...] = jnp.zeros_like(loss_ref)

        dg_ref[...] += dgain
        loss_ref[...] += jnp.broadcast_to(part, loss_ref.shape)

    row = pl.BlockSpec((ROWS, D), lambda i: (i, 0))
    vec = pl.BlockSpec((1, D), lambda i: (0, 0))
    return pl.pallas_call(
        body, grid=(T // ROWS,), in_specs=[row, vec, row],
        out_specs=[row, row, vec, pl.BlockSpec((1, 128), lambda i: (0, 0))],
        out_shape=[jax.ShapeDtypeStruct((T, D), F32), jax.ShapeDtypeStruct((T, D), BF16),
                   jax.ShapeDtypeStruct((1, D), F32), jax.ShapeDtypeStruct((1, 128), F32)],
        compiler_params=_cp(("arbitrary",)), name=name)(x, gain, target)


def _a_common(n, g, s_ref, q_ref, kc_ref, kp_ref):
    row = lax.broadcasted_iota(jnp.int32, (GROUP_A * BLK, BLK), 0)
    col = lax.broadcasted_iota(jnp.int32, (GROUP_A * BLK, BLK), 1)
    qi = row & (BLK - 1)
    dist_c = (qi - col).astype(F32)
    valid_c = qi >= col
    valid_p = jnp.logical_and(col > qi, n > 0)
    hidx = lax.broadcasted_iota(jnp.int32, (GROUP_A * BLK, 1), 0) >> 7
    slope = jnp.exp((-math.log(2.0) / 4.0) * (hidx + (GROUP_A * g + 1)).astype(F32))
    sink = jnp.zeros((GROUP_A * BLK, 1), F32)
    for i in range(GROUP_A):
        sink = jnp.where(hidx == i, s_ref[GROUP_A * g + i], sink)
    qg = jnp.concatenate([q_ref[:, (GROUP_A * g + i) * HD:(GROUP_A * g + i + 1) * HD]
                          for i in range(GROUP_A)], axis=0)
    kc = kc_ref[:, g * HD:(g + 1) * HD]
    kp = kp_ref[:, g * HD:(g + 1) * HD]
    nt = (((1,), (1,)), ((), ()))
    sc = lax.dot_general(qg, kc, nt, preferred_element_type=F32) * SCALE - slope * dist_c
    sp = lax.dot_general(qg, kp, nt, preferred_element_type=F32) * SCALE - slope * (dist_c + float(BLK))
    sc = jnp.where(valid_c, sc, NEG)
    sp = jnp.where(valid_p, sp, NEG)
    m = jnp.maximum(jnp.maximum(jnp.max(sc, axis=-1, keepdims=True), jnp.max(sp, axis=-1, keepdims=True)), sink)
    pc = jnp.exp(sc - m)
    pp = jnp.exp(sp - m)
    ps = jnp.exp(sink - m)
    den = jnp.sum(pc, axis=-1, keepdims=True) + jnp.sum(pp, axis=-1, keepdims=True) + ps
    return qg, kc, kp, pc, pp, ps, den, hidx


def attn_a_fwd(name, qkv, sinks, comm=None):
    kcol, vcol = NQ_A * HD // (NKV_A * HD), NQ_A * HD // (NKV_A * HD) + 1

    def body(s_ref, q_ref, kc_ref, kp_ref, vc_ref, vp_ref, o_ref):
        n = pl.program_id(0)
        for g in range(NKV_A):
            qg, kc, kp, pc, pp, ps, den, _ = _a_common(n, g, s_ref, q_ref, kc_ref, kp_ref)
            vc = vc_ref[:, g * HD:(g + 1) * HD]
            vp = vp_ref[:, g * HD:(g + 1) * HD]
            og = (jnp.dot(pc.astype(BF16), vc, preferred_element_type=F32)
                  + jnp.dot(pp.astype(BF16), vp, preferred_element_type=F32)) / den
            for i in range(GROUP_A):
                h = GROUP_A * g + i
                o_ref[:, h * HD:(h + 1) * HD] = og[i * BLK:(i + 1) * BLK].astype(BF16)

    kvw = NKV_A * HD
    prev = lambda n: jnp.maximum(n - 1, 0)
    outs, got = _pcall(
        name, body, grid=(T // BLK,),
        in_specs=[pl.BlockSpec(memory_space=pltpu.SMEM),
                  pl.BlockSpec((BLK, NQ_A * HD), lambda n: (n, 0)),
                  pl.BlockSpec((BLK, kvw), lambda n: (n, kcol)),
                  pl.BlockSpec((BLK, kvw), lambda n: (prev(n), kcol)),
                  pl.BlockSpec((BLK, kvw), lambda n: (n, vcol)),
                  pl.BlockSpec((BLK, kvw), lambda n: (prev(n), vcol))],
        out_specs=[pl.BlockSpec((BLK, NQ_A * HD), lambda n: (n, 0))],
        out_shape=[jax.ShapeDtypeStruct((T, NQ_A * HD), BF16)],
        sem=("parallel",), args=(sinks, qkv, qkv, qkv, qkv, qkv), comm=comm)
    return outs[0], got


def attn_a_bwd(name, qkv, sinks, o, do, comm=None):
    kcol, vcol = NQ_A * HD // (NKV_A * HD), NQ_A * HD // (NKV_A * HD) + 1
    nb = T // BLK
    kvw = NKV_A * HD

    def body(s_ref, q_ref, kc_ref, kp_ref, vc_ref, vp_ref, o_ref, do_ref,
             dq_ref, dk_ref, dv_ref, ds_ref, ck_ref, cv_ref):
        n = pl.program_id(0)

        @pl.when(n == 0)
        def _():
            ds_ref[...] = jnp.zeros_like(ds_ref)
            ck_ref[...] = jnp.zeros_like(ck_ref)
            cv_ref[...] = jnp.zeros_like(cv_ref)

        @pl.when(n == nb)
        def _():
            dk_ref[...] = ck_ref[...].astype(BF16)
            dv_ref[...] = cv_ref[...].astype(BF16)

        @pl.when(n < nb)
        def _():
            lane = lax.broadcasted_iota(jnp.int32, (1, 128), 1)
            dsink_row = jnp.zeros((1, 128), F32)
            tn = (((0,), (0,)), ((), ()))
            nt = (((1,), (1,)), ((), ()))
            for g in range(NKV_A):
                qg, kc, kp, pc, pp, ps, den, hidx = _a_common(n, g, s_ref, q_ref, kc_ref, kp_ref)
                vc = vc_ref[:, g * HD:(g + 1) * HD]
                vp = vp_ref[:, g * HD:(g + 1) * HD]
                cols = [slice((GROUP_A * g + i) * HD, (GROUP_A * g + i + 1) * HD) for i in range(GROUP_A)]
                og = jnp.concatenate([o_ref[:, c] for c in cols], axis=0)
                dog = jnp.concatenate([do_ref[:, c] for c in cols], axis=0)
                delta = jnp.sum(og.astype(F32) * dog.astype(F32), axis=-1, keepdims=True)
                inv = 1.0 / den
                p_c, p_p = pc * inv, pp * inv
                dsc = p_c * (lax.dot_general(dog, vc, nt, preferred_element_type=F32) - delta)
                dsp = p_p * (lax.dot_general(dog, vp, nt, preferred_element_type=F32) - delta)
                dsk = -(ps * inv) * delta
                for i in range(GROUP_A):
                    tot = jnp.sum(jnp.where(hidx == i, dsk, 0.0), axis=0, keepdims=True)
                    dsink_row = dsink_row + jnp.where(lane == GROUP_A * g + i, tot, 0.0)
                dscb, dspb = (dsc * SCALE).astype(BF16), (dsp * SCALE).astype(BF16)
                dqg = (jnp.dot(dscb, kc, preferred_element_type=F32)
                       + jnp.dot(dspb, kp, preferred_element_type=F32))
                for i in range(GROUP_A):
                    dq_ref[:, cols[i]] = dqg[i * BLK:(i + 1) * BLK].astype(BF16)
                gs = slice(g * HD, (g + 1) * HD)
                dk_ref[:, gs] = (ck_ref[:, gs] + lax.dot_general(dspb, qg, tn, preferred_element_type=F32)).astype(BF16)
                dv_ref[:, gs] = (cv_ref[:, gs] + lax.dot_general(p_p.astype(BF16), dog, tn, preferred_element_type=F32)).astype(BF16)
                ck_ref[:, gs] = lax.dot_general(dscb, qg, tn, preferred_element_type=F32)
                cv_ref[:, gs] = lax.dot_general(p_c.astype(BF16), dog, tn, preferred_element_type=F32)
            ds_ref[...] += jnp.broadcast_to(dsink_row, ds_ref.shape)

    cur = lambda n: jnp.minimum(n, nb - 1)
    prev = lambda n: jnp.maximum(jnp.minimum(n, nb - 1) - 1, 0)
    outp = lambda n: jnp.maximum(n - 1, 0)
    qspec = pl.BlockSpec((BLK, NQ_A * HD), lambda n: (cur(n), 0))
    return _pcall(
        name, body, grid=(nb + 1,),
        in_specs=[pl.BlockSpec(memory_space=pltpu.SMEM), qspec,
                  pl.BlockSpec((BLK, kvw), lambda n: (cur(n), kcol)),
                  pl.BlockSpec((BLK, kvw), lambda n: (prev(n), kcol)),
                  pl.BlockSpec((BLK, kvw), lambda n: (cur(n), vcol)),
                  pl.BlockSpec((BLK, kvw), lambda n: (prev(n), vcol)),
                  qspec, qspec],
        out_specs=[qspec, pl.BlockSpec((BLK, kvw), lambda n: (outp(n), 0)),
                   pl.BlockSpec((BLK, kvw), lambda n: (outp(n), 0)),
                   pl.BlockSpec((8, 128), lambda n: (0, 0))],
        out_shape=[jax.ShapeDtypeStruct((T, NQ_A * HD), BF16), jax.ShapeDtypeStruct((T, kvw), BF16),
                   jax.ShapeDtypeStruct((T, kvw), BF16), jax.ShapeDtypeStruct((8, 128), F32)],
        scratch_shapes=[pltpu.VMEM((BLK, kvw), F32), pltpu.VMEM((BLK, kvw), F32)],
        sem=("arbitrary",), args=(sinks, qkv, qkv, qkv, qkv, qkv, o, do), comm=comm)


def _b_logs(z):
    l1p = jnp.log(1.0 + jnp.exp(-jnp.abs(z)))
    return jnp.minimum(z, 0.0) - l1p, -jnp.maximum(z, 0.0) - l1p


def _split_dot(x, m):
    hi = x.astype(BF16)
    lo = (x - hi.astype(F32)).astype(BF16)
    return jnp.dot(hi, m, preferred_element_type=F32) + jnp.dot(lo, m, preferred_element_type=F32)


def _b_tile(j, diag, qh, k_ref, hs, c_t, mgt, tri):
    kj = k_ref[pl.ds(pl.multiple_of(j * BLK, BLK), BLK), hs]
    z = lax.dot_general(qh, kj, (((1,), (1,)), ((), ())), preferred_element_type=F32) * SCALE
    lb, lm = _b_logs(z)
    if diag:
        lm = jnp.where(tri, lm, 0.0)
    a = jnp.exp(lb + c_t + _split_dot(lm, mgt))
    if diag:
        a = jnp.where(tri, a, 0.0)
    return z, a, c_t + jnp.sum(lm, axis=-1, keepdims=True)


B_CUT = 105.0
B_HEADS = (slice(0, HD), slice(HD, 2 * HD))


def _b_masks():
    r = lax.broadcasted_iota(jnp.int32, (BLK, BLK), 0)
    c = lax.broadcasted_iota(jnp.int32, (BLK, BLK), 1)
    return (r > c).astype(BF16), (r < c).astype(BF16), c < r


def _b_rows(j):
    return pl.ds(pl.multiple_of(j * BLK, BLK), BLK)


def _b_live(c0, c1):
    return jnp.maximum(jnp.max(c0), jnp.max(c1)) > -B_CUT


def attn_b_fwd(name, qkv, comm=None):
    npair = NH_B // 2

    def body(q_ref, k_ref, v_ref, o_ref):
        n = pl.program_id(1)
        mgt, _, tri = _b_masks()
        qs = [q_ref[:, hs] for hs in B_HEADS]

        def step(j, st, diag):
            out = []
            for h, hs in enumerate(B_HEADS):
                c_t, acc = st[h]
                _, a, c_t = _b_tile(j, diag, qs[h], k_ref, hs, c_t, mgt, tri)
                out.append((c_t, acc + jnp.dot(a.astype(BF16), v_ref[_b_rows(j), hs], preferred_element_type=F32)))
            return tuple(out)

        zero = (jnp.zeros((BLK, 1), F32), jnp.zeros((BLK, HD), F32))
        st = step(n, (zero, zero), True)
        _, st = lax.while_loop(lambda cr: jnp.logical_and(cr[0] >= 0, _b_live(cr[1][0][0], cr[1][1][0])),
                               lambda cr: (cr[0] - 1, step(cr[0], cr[1], False)), (n - 1, st))
        for h, hs in enumerate(B_HEADS):
            o_ref[:, hs] = st[h][1].astype(BF16)

    outs, got = _pcall(
        name, body, grid=(npair, T // BLK),
        in_specs=[pl.BlockSpec((BLK, 2 * HD), lambda p, n: (n, p)),
                  pl.BlockSpec((T, 2 * HD), lambda p, n: (0, npair + p)),
                  pl.BlockSpec((T, 2 * HD), lambda p, n: (0, 2 * npair + p))],
        out_specs=[pl.BlockSpec((BLK, 2 * HD), lambda p, n: (n, p))],
        out_shape=[jax.ShapeDtypeStruct((T, NH_B * HD), BF16)],
        sem=("parallel", "arbitrary"), args=(qkv, qkv, qkv), comm=comm)
    return outs[0], got


def attn_b_bwd(name, qkv, do, comm=None):
    npair = NH_B // 2
    nb = T // BLK

    def body(q_ref, do_ref, k_ref, v_ref, dq_ref, dk_ref, dv_ref, a_s, z_s):
        n = pl.program_id(1)

        @pl.when(n == 0)
        def _():
            dk_ref[...] = jnp.zeros_like(dk_ref)
            dv_ref[...] = jnp.zeros_like(dv_ref)

        mgt, mlt, tri = _b_masks()
        nt = (((1,), (1,)), ((), ()))
        tn = (((0,), (0,)), ((), ()))
        qs = [q_ref[:, hs] for hs in B_HEADS]
        dos = [do_ref[:, hs] for hs in B_HEADS]

        def sweep1(j, cs, diag):
            out = []
            for h, hs in enumerate(B_HEADS):
                z, a, c_t = _b_tile(j, diag, qs[h], k_ref, hs, cs[h], mgt, tri)
                a_s[h, j] = a
                z_s[h, j] = z
                out.append(c_t)
            return tuple(out)

        cs = sweep1(n, (jnp.zeros((BLK, 1), F32),) * 2, True)
        j_lo, _ = lax.while_loop(lambda cr: jnp.logical_and(cr[0] >= 0, _b_live(cr[1][0], cr[1][1])),
                                 lambda cr: (cr[0] - 1, sweep1(cr[0], cr[1], False)), (n - 1, cs))
        j_lo = j_lo + 1

        def sweep2(j, st, diag):
            rows = _b_rows(j)
            out = []
            for h, hs in enumerate(B_HEADS):
                cg, dq = st[h]
                kj, vj = k_ref[rows, hs], v_ref[rows, hs]
                a, z = a_s[h, j], z_s[h, j]
                g = a * lax.dot_general(dos[h], vj, nt, preferred_element_type=F32)
                big_g = cg + _split_dot(g, mlt)
                e = jnp.exp(-jnp.abs(z))
                inv = 1.0 / (1.0 + e)
                sig = jnp.where(z >= 0, inv, e * inv)
                dz = g * (1.0 - sig) - big_g * sig
                if diag:
                    dz = jnp.where(tri, dz, 0.0)
                dzb = (dz * SCALE).astype(BF16)
                dk_ref[rows, hs] += lax.dot_general(dzb, qs[h], tn, preferred_element_type=F32)
                dv_ref[rows, hs] += lax.dot_general(a.astype(BF16), dos[h], tn, preferred_element_type=F32)
                out.append((cg + jnp.sum(g, axis=-1, keepdims=True),
                            dq + jnp.dot(dzb, kj, preferred_element_type=F32)))
            return tuple(out)

        zero = (jnp.zeros((BLK, 1), F32), jnp.zeros((BLK, HD), F32))
        st = lax.fori_loop(j_lo, n, lambda j, cr: sweep2(j, cr, False), (zero, zero))
        st = sweep2(n, st, True)
        for h, hs in enumerate(B_HEADS):
            dq_ref[:, hs] = st[h][1].astype(BF16)

    tile = pl.BlockSpec((BLK, 2 * HD), lambda p, n: (n, p))
    full = lambda off: pl.BlockSpec((T, 2 * HD), lambda p, n: (0, off + p))
    return _pcall(
        name, body, grid=(npair, nb),
        in_specs=[tile, tile, full(npair), full(2 * npair)],
        out_specs=[tile, full(0), full(0)],
        out_shape=[jax.ShapeDtypeStruct((T, NH_B * HD), BF16), jax.ShapeDtypeStruct((T, NH_B * HD), F32),
                   jax.ShapeDtypeStruct((T, NH_B * HD), F32)],
        scratch_shapes=[pltpu.VMEM((2, nb, BLK, BLK), F32), pltpu.VMEM((2, nb, BLK, BLK), F32)],
        sem=("parallel", "arbitrary"), args=(qkv, do, qkv, qkv), comm=comm)


def _adamw_math(w, g, m, v):
    m = B1 * m + (1.0 - B1) * g
    v = B2 * v + (1.0 - B2) * (g * g)
    m_hat = m / (1.0 - B1 ** STEP)
    v_hat = v / (1.0 - B2 ** STEP)
    delta = -LR * (m_hat / (jnp.sqrt(v_hat) + EPS) + WD * w)
    return delta, m, v


def adamw(name, parts, w, m, v, rows):
    r, c = w.shape

    def body(p_ref, w_ref, m_ref, v_ref, g_ref, d_ref, nm_ref, nv_ref):
        g = p_ref[0].astype(F32)
        for s in range(1, NDEV):
            g = g + p_ref[s].astype(F32)
        delta, nm, nv = _adamw_math(w_ref[...], g, m_ref[...], v_ref[...])
        g_ref[...] = g
        d_ref[...] = delta
        nm_ref[...] = nm
        nv_ref[...] = nv

    blk = pl.BlockSpec((rows, c), lambda i: (i, 0))
    return pl.pallas_call(
        body, grid=(r // rows,),
        in_specs=[pl.BlockSpec((NDEV, rows, c), lambda i: (0, i, 0)), blk, blk, blk],
        out_specs=[blk] * 4, out_shape=[jax.ShapeDtypeStruct((r, c), F32)] * 4,
        compiler_params=_cp(("parallel",)), name=name)(parts, w, m, v)


def _unshard_cols(g):
    return jnp.transpose(g, (1, 0, 2)).reshape(g.shape[1], -1)


def _shard_cols(w):
    k, n = w.shape
    return jnp.transpose(w.reshape(k, NDEV, n // NDEV), (1, 0, 2))


FWD_CARRY = {
    ("qkv", 0): [(0, 1)], ("attn", 0): [(0, 2)], ("mlp_in", 0): [(0, 3)], ("mlp_out", 0): [(1, 0)],
    ("qkv", 1): [(1, 1)], ("attn", 1): [(1, 2), (1, 3), (2, 0), (2, 1), (2, 2), (2, 3)],
    ("mlp_in", 1): [(3, 0)], ("mlp_out", 1): [(3, 1)],
    ("attn", 2): [(3, 2)], ("mlp_in", 2): [(3, 3)],
}
BWD_CARRY = {
    ("attn_bwd", 3): [(3, 3), (3, 2), (3, 1)],
    ("d_act", 2): [(3, 0)], ("d_h2", 2): [(2, 3)], ("attn_bwd", 2): [(2, 2)], ("d_h1", 2): [(2, 1)],
    ("d_act", 1): [(2, 0)], ("attn_bwd", 1): [(1, 3), (1, 2), (1, 1)],
    ("d_act", 0): [(1, 0)], ("d_h2", 0): [(0, 3)], ("attn_bwd", 0): [(0, 2)], ("d_h1", 0): [(0, 1)],
}


def kernel(x, a_w_qkv, a_w_o, a_sinks, b_w_qkv, b_w_o, norm_mix, norm_mlp, mlp_w_in, mlp_w_out, final_norm, loss_target, m_a_w_qkv, m_a_w_o, m_a_sinks, m_b_w_qkv, m_b_w_o, m_norm_mix, m_norm_mlp, m_mlp_w_in, m_mlp_w_out, m_final_norm, v_a_w_qkv, v_a_w_o, v_a_sinks, v_b_w_qkv, v_b_w_o, v_norm_mix, v_norm_mlp, v_mlp_w_in, v_mlp_w_out, v_final_norm):
    depth = 4
    xs = x[0]

    shards, full, sendbuf, recv = {}, {}, {}, {}
    for i in range(depth):
        j = i // 2
        wq, wo = (a_w_qkv[j], a_w_o[j]) if i % 2 == 0 else (b_w_qkv[j], b_w_o[j])
        for t, w in enumerate((wq, wo, mlp_w_in[i], mlp_w_out[i])):
            shards[(i, t)] = w.astype(BF16)

    def land_weights(keys, slabs):
        for (i, t), g in zip(keys, slabs):
            full[(i, t)] = _unshard_cols(g) if t in (0, 2) else g.reshape(-1, D)

    def land_grads(keys, slabs):
        recv.update(zip(keys, slabs))

    def carried(table, payload, gather, land, kind, i, fn):
        keys = table.get((kind, i), [])
        out, got = fn(Comm([payload[k] for k in keys], gather) if keys else None)
        land(keys, got)
        return out

    fwd = functools.partial(carried, FWD_CARRY, shards, True, land_weights)
    bwd = functools.partial(carried, BWD_CARRY, sendbuf, False, land_grads)

    relu2 = lambda acc: (acc, jnp.square(jnp.maximum(acc, 0.0)))
    add = lambda acc, res: (acc + res,)
    drelu2 = lambda acc, uu: (acc * (2.0 * jnp.maximum(uu.astype(F32), 0.0)),)

    land_weights([(0, 0)], exchange("gather_first", [shards[(0, 0)]], True))
    saved = []
    for i in range(depth):
        x1 = xs
        h1 = rmsnorm_fwd(f"norm_mix{i}", x1, norm_mix[i][None])
        (qkv,) = fwd("qkv", i, lambda c: mm_nn(f"qkv{i}", h1, full[(i, 0)], out_dtype=BF16,
                                                tn=512 if i % 2 == 0 else 768, comm=c))
        if i % 2 == 0:
            o = fwd("attn", i, lambda c: attn_a_fwd(f"attn_a{i}", qkv, a_sinks[i // 2], comm=c))
        else:
            o = fwd("attn", i, lambda c: attn_b_fwd(f"attn_b{i}", qkv, comm=c))
        (x2,) = mm_nn(f"proj{i}", o, full[(i, 1)], out_dtype=F32, extras=(x1,), epilogue=add)[0]
        h2 = rmsnorm_fwd(f"norm_mlp{i}", x2, norm_mlp[i][None])
        u, act = fwd("mlp_in", i, lambda c: mm_nn(f"mlp_in{i}", h2, full[(i, 2)], out_dtype=BF16,
                                                   epilogue=relu2, n_out=2, comm=c))
        (xs,) = fwd("mlp_out", i, lambda c: mm_nn(f"mlp_out{i}", act, full[(i, 3)], out_dtype=F32,
                                                   extras=(x2,), epilogue=add, comm=c))
        saved.append((x1, h1, qkv, o, x2, h2, u, act))

    dx, dxb, d_final, loss_part = loss_head("loss_head", xs, final_norm[None], loss_target[0])
    loss = lax.psum(loss_part[0, 0], ("x", "y", "c"))

    d_mix, d_mlp, d_sinks = [None] * depth, [None] * depth, [None] * 2
    for i in reversed(range(depth)):
        x1, h1, qkv, o, x2, h2, u, act = saved[i]
        sendbuf[(i, 3)] = mm_tn(f"g_mlp_out{i}", act, dxb)[0].reshape(NDEV, DFF // NDEV, D)
        du = bwd("d_act", i, lambda c: mm_nt(f"d_act{i}", dxb, full[(i, 3)], out_dtype=BF16, extras=(u,),
                                             epilogue=drelu2, comm=c))
        sendbuf[(i, 2)] = _shard_cols(mm_tn(f"g_mlp_in{i}", h2, du)[0])
        dh2 = bwd("d_h2", i, lambda c: mm_nt(f"d_h2{i}", du, full[(i, 2)], out_dtype=F32, comm=c))
        dx, dxb, d_mlp[i] = rmsnorm_bwd(f"norm_mlp_bwd{i}", dh2, x2, norm_mlp[i][None], dx)
        sendbuf[(i, 1)] = mm_tn(f"g_proj{i}", o, dxb)[0].reshape(NDEV, D // NDEV, D)
        do = mm_nt(f"d_o{i}", dxb, full[(i, 1)], out_dtype=BF16)[0]
        if i % 2 == 0:
            dq, dk, dv, ds = bwd("attn_bwd", i, lambda c: attn_a_bwd(f"attn_a_bwd{i}", qkv, a_sinks[i // 2],
                                                                       o, do, comm=c))
            d_sinks[i // 2] = ds[0, :NQ_A]
            dqkv = jnp.concatenate([dq, dk, dv], axis=1)
        else:
            dq, dk, dv = bwd("attn_bwd", i, lambda c: attn_b_bwd(f"attn_b_bwd{i}", qkv, do, comm=c))
            dqkv = jnp.concatenate([dq, dk.astype(BF16), dv.astype(BF16)], axis=1)
        sendbuf[(i, 0)] = _shard_cols(mm_tn(f"g_qkv{i}", h1, dqkv, tn=512 if i % 2 == 0 else 768)[0])
        dh1 = bwd("d_h1", i, lambda c: mm_nt(f"d_h1{i}", dqkv, full[(i, 0)], out_dtype=F32, comm=c))
        dx, dxb, d_mix[i] = rmsnorm_bwd(f"norm_mix_bwd{i}", dh1, x1, norm_mix[i][None], dx)

    small = jnp.zeros((16, D), F32)
    small = small.at[0:4].set(jnp.concatenate(d_mix, axis=0)).at[4:8].set(jnp.concatenate(d_mlp, axis=0))
    small = small.at[8].set(d_final[0]).at[9, :2 * NQ_A].set(jnp.concatenate(d_sinks))
    recv[(0, 0)], small_parts = exchange("exchange_last", [sendbuf[(0, 0)], small], [False, True])
    assert len(recv) == 4 * depth and len(full) == 4 * depth
    pack = lambda mix, mlp, fin, snk: (jnp.zeros((16, D), F32).at[0:4].set(mix).at[4:8].set(mlp)
                                       .at[8].set(fin).at[9, :2 * NQ_A].set(snk.reshape(-1)))
    sm = adamw("adamw_small", small_parts, pack(norm_mix, norm_mlp, final_norm, a_sinks),
               pack(m_norm_mix, m_norm_mlp, m_final_norm, m_a_sinks),
               pack(v_norm_mix, v_norm_mlp, v_final_norm, v_a_sinks), rows=16)
    unpack = lambda a: (a[0:4], a[4:8], a[8], a[9, :2 * NQ_A].reshape(2, NQ_A))

    def layer_update(i, t, w, m, v, rows):
        return adamw(f"adamw{i}_{t}", recv[(i, t)], w, m, v, rows)

    res = {}
    for i in range(depth):
        j = i // 2
        if i % 2 == 0:
            res[("qkv", i)] = layer_update(i, 0, a_w_qkv[j], m_a_w_qkv[j], v_a_w_qkv[j], 256)
            res[("o", i)] = layer_update(i, 1, a_w_o[j], m_a_w_o[j], v_a_w_o[j], 128)
        else:
            res[("qkv", i)] = layer_update(i, 0, b_w_qkv[j], m_b_w_qkv[j], v_b_w_qkv[j], 256)
            res[("o", i)] = layer_update(i, 1, b_w_o[j], m_b_w_o[j], v_b_w_o[j], 128)
        res[("in", i)] = layer_update(i, 2, mlp_w_in[i], m_mlp_w_in[i], v_mlp_w_in[i], 256)
        res[("out", i)] = layer_update(i, 3, mlp_w_out[i], m_mlp_w_out[i], v_mlp_w_out[i], 128)

    outs = []
    for q in range(4):
        mix, mlp, fin, snk = unpack(sm[q])
        outs.append([
            jnp.stack([res[("qkv", 0)][q], res[("qkv", 2)][q]]),
            jnp.stack([res[("o", 0)][q], res[("o", 2)][q]]),
            snk,
            jnp.stack([res[("qkv", 1)][q], res[("qkv", 3)][q]]),
            jnp.stack([res[("o", 1)][q], res[("o", 3)][q]]),
            mix, mlp,
            jnp.stack([res[("in", i)][q] for i in range(depth)]),
            jnp.stack([res[("out", i)][q] for i in range(depth)]),
            fin,
        ])
    return (loss, dx[None], *outs[0], *outs[1], *outs[2], *outs[3])
```

```python
import functools
import math

import jax
import jax.numpy as jnp
from jax import lax
from jax.experimental import pallas as pl
from jax.experimental.pallas import tpu as pltpu

F32, BF16 = jnp.float32, jnp.bfloat16
T, D, DFF = 4096, 2048, 8192
HD, BLK = 64, 128
NQ_A, NKV_A, GROUP_A = 32, 4, 8
NH_B = 32
NDEV = 8
RMS_EPS = 1e-5
SCALE = 0.125
NEG = -1e30
VMEM_LIMIT = 56 * 1024 * 1024
LR, B1, B2, EPS, WD, STEP = 0.001, 0.9, 0.999, 1e-08, 0.01, 10
MESH = pl.DeviceIdType.MESH


def _cp(sem, vmem=VMEM_LIMIT):
    return pltpu.CompilerParams(dimension_semantics=sem, vmem_limit_bytes=vmem)


class Comm:
    def __init__(self, ins, gather):
        self.ins, self.nt = list(ins), len(ins)
        self.gather = list(gather) if isinstance(gather, (list, tuple)) else [gather] * self.nt
        self.out_shape = [jax.ShapeDtypeStruct((NDEV,) + (a.shape if g else a.shape[1:]), a.dtype)
                          for a, g in zip(self.ins, self.gather)]
        self.scratch = [pltpu.SemaphoreType.DMA((7 * self.nt,)), pltpu.SemaphoreType.DMA((7 * self.nt,)),
                        pltpu.SemaphoreType.DMA((self.nt,))]

    def _copies(self, src, dst, sems, *kinds):
        send_sems, recv_sems, local_sems = sems
        x, y, c = lax.axis_index("x"), lax.axis_index("y"), lax.axis_index("c")
        me, sib = (x, y, c), (x, y, 1 - c)
        chips = [(1 - x, y), (x, 1 - y), (1 - x, 1 - y)]
        p = {kind: [] for kind in kinds}
        for t in range(self.nt):
            def slot(d, t=t):
                return dst[t].at[4 * d[0] + 2 * d[1] + d[2]]

            def add(kind, k, a, b, to, t=t):
                if kind in p:
                    p[kind].append(pltpu.make_async_remote_copy(
                        src_ref=a, dst_ref=b, send_sem=send_sems.at[t * 7 + k],
                        recv_sem=recv_sems.at[t * 7 + k], device_id=to, device_id_type=MESH))

            own = src[t] if self.gather[t] else src[t].at[4 * x + 2 * y + c]
            if "local" in p:
                p["local"].append(pltpu.make_async_copy(own, slot(me), local_sems.at[t]))
            if self.gather[t]:
                add("first", 0, own, slot(me), sib)
                add("first_in", 0, own, slot(sib), sib)
                for j, chip in enumerate(chips):
                    there = (*chip, c)
                    add("first", 1 + j, own, slot(me), there)
                    add("chip_in", 1 + j, own, slot(there), there)
                    add("relay", 4 + j, slot(there), slot(there), sib)
                    add("relay_in", 4 + j, own, slot((*chip, 1 - c)), sib)
            else:
                for k in range(NDEV - 1):
                    bits = k + 1
                    peer = (1 - x if bits & 4 else x, 1 - y if bits & 2 else y, 1 - c if bits & 1 else c)
                    out = src[t].at[4 * peer[0] + 2 * peer[1] + peer[2]]
                    add("first", k, out, slot(me), peer)
                    add("first_in", k, out, slot(peer), peer)
        return [p[kind] for kind in kinds]

    def start(self, src, dst, sems):
        local, first = self._copies(src, dst, sems, "local", "first")
        for cp in local + first:
            cp.start()

    def relay(self, src, dst, sems):
        chip_in, relay = self._copies(src, dst, sems, "chip_in", "relay")
        for arrived, onward in zip(chip_in, relay):
            arrived.wait_recv()
            onward.start()

    def finish(self, src, dst, sems):
        first_in, relay_in, first, relay, local = self._copies(
            src, dst, sems, "first_in", "relay_in", "first", "relay", "local")
        for cp in first_in + relay_in:
            cp.wait_recv()
        for cp in first + relay:
            cp.wait_send()
        for cp in local:
            cp.wait()


def _pcall(name, body, *, grid, in_specs, out_specs, out_shape, scratch_shapes=(), sem, args, comm=None):
    in_specs, out_specs, out_shape = list(in_specs), list(out_specs), list(out_shape)
    scratch_shapes = list(scratch_shapes)
    if comm is None:
        outs = pl.pallas_call(body, grid=grid, in_specs=in_specs, out_specs=out_specs, out_shape=out_shape,
                              scratch_shapes=scratch_shapes, compiler_params=_cp(sem), name=name)(*args)
        return list(outs), []
    ni, no, ns, nt = len(in_specs), len(out_specs), len(scratch_shapes), comm.nt
    steps = math.prod(grid)
    hbm = pl.BlockSpec(memory_space=pltpu.HBM)

    def carrier(*refs):
        ins, cin = refs[:ni], refs[ni:ni + nt]
        outs, cout = refs[ni + nt:ni + nt + no], refs[ni + nt + no:ni + 2 * nt + no]
        scr, sems = refs[ni + 2 * nt + no:ni + 2 * nt + no + ns], refs[ni + 2 * nt + no + ns:]
        step = functools.reduce(lambda acc, d: acc * grid[d] + pl.program_id(d), range(len(grid)), 0)

        @pl.when(step == 0)
        def _():
            comm.start(cin, cout, sems)

        body(*ins, *outs, *scr)

        @pl.when(step == (7 * steps) // 8)
        def _():
            comm.relay(cin, cout, sems)

        @pl.when(step == steps - 1)
        def _():
            comm.finish(cin, cout, sems)

    outs = pl.pallas_call(
        carrier, grid=grid, in_specs=in_specs + [hbm] * nt, out_specs=out_specs + [hbm] * nt,
        out_shape=out_shape + comm.out_shape, scratch_shapes=scratch_shapes + comm.scratch,
        compiler_params=_cp(("arbitrary",) * len(grid)), name=name)(*args, *comm.ins)
    return list(outs[:no]), list(outs[no:])


def exchange(name, ins, gather):
    comm = Comm(ins, gather)
    hbm = pl.BlockSpec(memory_space=pltpu.HBM)

    def body(*refs):
        src, dst, sems = refs[:comm.nt], refs[comm.nt:2 * comm.nt], refs[2 * comm.nt:]
        comm.start(src, dst, sems)
        comm.relay(src, dst, sems)
        comm.finish(src, dst, sems)

    return pl.pallas_call(body, in_specs=[hbm] * comm.nt, out_specs=[hbm] * comm.nt, out_shape=comm.out_shape,
                          scratch_shapes=comm.scratch, name=name)(*ins)


def _matmul(name, a, b, extras, *, grid, a_spec, b_spec, extra_specs, out_shapes, out_specs,
            contract, epilogue, acc_shape, comm=None):
    nk = grid[2]
    ne, no = len(extras), len(out_shapes)

    def body(a_ref, b_ref, *rest):
        ex, outs, acc = rest[:ne], rest[ne:ne + no], rest[ne + no]
        kk = pl.program_id(2)

        @pl.when(kk == 0)
        def _():
            acc[...] = jnp.zeros_like(acc)

        acc[...] += lax.dot_general(a_ref[...].astype(BF16), b_ref[...].astype(BF16),
                                    (contract, ((), ())), preferred_element_type=F32)

        @pl.when(kk == nk - 1)
        def _():
            res = epilogue(acc[...], *[r[...] for r in ex])
            for r, o in zip(outs, res):
                r[...] = o.astype(r.dtype)

    return _pcall(name, body, grid=grid, in_specs=[a_spec, b_spec, *extra_specs], out_specs=out_specs,
                  out_shape=out_shapes, scratch_shapes=[pltpu.VMEM(acc_shape, F32)],
                  sem=("parallel", "parallel", "arbitrary"), args=(a, b, *extras), comm=comm)


def _ident(acc):
    return (acc,)


def _fit(tile, dim):
    return max(t for t in range(128, min(tile, dim) + 1, 128) if dim % t == 0)


def mm_nn(name, a, w, *, out_dtype, tm=1024, tn=1024, tk=2048, extras=(), epilogue=_ident, n_out=1,
          out_dtypes=None, comm=None):
    m, k = a.shape
    n = w.shape[1]
    tm, tn, tk = _fit(tm, m), _fit(tn, n), _fit(tk, k)
    out_dtypes = out_dtypes or (out_dtype,) * n_out
    mn = pl.BlockSpec((tm, tn), lambda i, j, kk: (i, j))
    return _matmul(name, a, w, extras, grid=(m // tm, n // tn, k // tk),
                   a_spec=pl.BlockSpec((tm, tk), lambda i, j, kk: (i, kk)),
                   b_spec=pl.BlockSpec((tk, tn), lambda i, j, kk: (kk, j)),
                   extra_specs=[mn] * len(extras),
                   out_shapes=[jax.ShapeDtypeStruct((m, n), dt) for dt in out_dtypes],
                   out_specs=[mn] * len(out_dtypes), contract=((1,), (0,)), epilogue=epilogue,
                   acc_shape=(tm, tn), comm=comm)


def mm_nt(name, a, w, *, out_dtype, tm=1024, tn=1024, tk=2048, extras=(), epilogue=_ident, comm=None):
    m, n = a.shape
    k = w.shape[0]
    tm, to, tc = _fit(tm, m), _fit(tn, k), _fit(tk, n)
    mo = pl.BlockSpec((tm, to), lambda i, j, kk: (i, j))
    outs, got = _matmul(name, a, w, extras, grid=(m // tm, k // to, n // tc),
                        a_spec=pl.BlockSpec((tm, tc), lambda i, j, kk: (i, kk)),
                        b_spec=pl.BlockSpec((to, tc), lambda i, j, kk: (j, kk)),
                        extra_specs=[mo] * len(extras),
                        out_shapes=[jax.ShapeDtypeStruct((m, k), out_dtype)],
                        out_specs=[mo], contract=((1,), (1,)), epilogue=epilogue, acc_shape=(tm, to),
                        comm=comm)
    return outs[0], got


def mm_tn(name, a, b, *, out_dtype=BF16, tm=1024, tn=1024, tk=2048, comm=None):
    t, k = a.shape
    n = b.shape[1]
    tm, tn, tk = _fit(tm, k), _fit(tn, n), _fit(tk, t)
    outs, got = _matmul(name, a, b, (), grid=(k // tm, n // tn, t // tk),
                        a_spec=pl.BlockSpec((tk, tm), lambda i, j, kk: (kk, i)),
                        b_spec=pl.BlockSpec((tk, tn), lambda i, j, kk: (kk, j)),
                        extra_specs=[], out_shapes=[jax.ShapeDtypeStruct((k, n), out_dtype)],
                        out_specs=[pl.BlockSpec((tm, tn), lambda i, j, kk: (i, j))],
                        contract=((0,), (0,)), epilogue=_ident, acc_shape=(tm, tn), comm=comm)
    return outs[0], got


ROWS = 256


def rmsnorm_fwd(name, x, gain):
    def body(x_ref, g_ref, h_ref):
        xf = x_ref[...]
        rstd = lax.rsqrt(jnp.mean(xf * xf, axis=-1, keepdims=True) + RMS_EPS)
        h_ref[...] = (xf * rstd * g_ref[...]).astype(BF16)

    row = pl.BlockSpec((ROWS, D), lambda i: (i, 0))
    return pl.pallas_call(body, grid=(T // ROWS,), in_specs=[row, pl.BlockSpec((1, D), lambda i: (0, 0))],
                          out_specs=row, out_shape=jax.ShapeDtypeStruct((T, D), BF16),
                          compiler_params=_cp(("parallel",)), name=name)(x, gain)


def _rms_bwd_math(dh, xf, gain):
    rstd = lax.rsqrt(jnp.mean(xf * xf, axis=-1, keepdims=True) + RMS_EPS)
    xhat = xf * rstd
    dgain = jnp.sum(dh * xhat, axis=0, keepdims=True)
    dxhat = dh * gain
    dx = rstd * (dxhat - xhat * jnp.mean(dxhat * xhat, axis=-1, keepdims=True))
    return dx, dgain


def rmsnorm_bwd(name, dh, x, gain, dres):
    def body(dh_ref, x_ref, g_ref, dres_ref, dx_ref, dxb_ref, dg_ref):
        dx, dgain = _rms_bwd_math(dh_ref[...], x_ref[...], g_ref[...])
        dx = dx + dres_ref[...]
        dx_ref[...] = dx
        dxb_ref[...] = dx.astype(BF16)

        @pl.when(pl.program_id(0) == 0)
        def _():
            dg_ref[...] = jnp.zeros_like(dg_ref)

        dg_ref[...] += dgain

    row = pl.BlockSpec((ROWS, D), lambda i: (i, 0))
    vec = pl.BlockSpec((1, D), lambda i: (0, 0))
    return pl.pallas_call(
        body, grid=(T // ROWS,), in_specs=[row, row, vec, row], out_specs=[row, row, vec],
        out_shape=[jax.ShapeDtypeStruct((T, D), F32), jax.ShapeDtypeStruct((T, D), BF16),
                   jax.ShapeDtypeStruct((1, D), F32)],
        compiler_params=_cp(("arbitrary",)), name=name)(dh, x, gain, dres)


def loss_head(name, x, gain, target):
    def body(x_ref, g_ref, t_ref, dx_ref, dxb_ref, dg_ref, loss_ref):
        xf, gain = x_ref[...], g_ref[...]
        rstd = lax.rsqrt(jnp.mean(xf * xf, axis=-1, keepdims=True) + RMS_EPS)
        err = xf * rstd * gain - t_ref[...]
        part = 0.5 * jnp.sum(jnp.mean(err * err, axis=-1, keepdims=True), axis=0, keepdims=True)
        dx, dgain = _rms_bwd_math(err * (1.0 / D), xf, gain)
        dx_ref[...] = dx
        dxb_ref[...] = dx.astype(BF16)

        @pl.when(pl.program_id(0) == 0)
        def _():
            dg_ref[...] = jnp.zeros_like(dg_ref)
            loss_ref[...] = jnp.zeros_like(loss_ref)

        dg_ref[...] += dgain
        loss_ref[...] += jnp.broadcast_to(part, loss_ref.shape)

    row = pl.BlockSpec((ROWS, D), lambda i: (i, 0))
    vec = pl.BlockSpec((1, D), lambda i: (0, 0))
    return pl.pallas_call(
        body, grid=(T // ROWS,), in_specs=[row, vec, row],
        out_specs=[row, row, vec, pl.BlockSpec((1, 128), lambda i: (0, 0))],
        out_shape=[jax.ShapeDtypeStruct((T, D), F32), jax.ShapeDtypeStruct((T, D), BF16),
                   jax.ShapeDtypeStruct((1, D), F32), jax.ShapeDtypeStruct((1, 128), F32)],
        compiler_params=_cp(("arbitrary",)), name=name)(x, gain, target)


def _a_common(n, g, s_ref, q_ref, kc_ref, kp_ref):
    row = lax.broadcasted_iota(jnp.int32, (GROUP_A * BLK, BLK), 0)
    col = lax.broadcasted_iota(jnp.int32, (GROUP_A * BLK, BLK), 1)
    qi = row & (BLK - 1)
    dist_c = (qi - col).astype(F32)
    valid_c = qi >= col
    valid_p = jnp.logical_and(col > qi, n > 0)
    hidx = lax.broadcasted_iota(jnp.int32, (GROUP_A * BLK, 1), 0) >> 7
    slope = jnp.exp((-math.log(2.0) / 4.0) * (hidx + (GROUP_A * g + 1)).astype(F32))
    sink = jnp.zeros((GROUP_A * BLK, 1), F32)
    for i in range(GROUP_A):
        sink = jnp.where(hidx == i, s_ref[GROUP_A * g + i], sink)
    qg = jnp.concatenate([q_ref[:, (GROUP_A * g + i) * HD:(GROUP_A * g + i + 1) * HD]
                          for i in range(GROUP_A)], axis=0)
    kc = kc_ref[:, g * HD:(g + 1) * HD]
    kp = kp_ref[:, g * HD:(g + 1) * HD]
    nt = (((1,), (1,)), ((), ()))
    sc = lax.dot_general(qg, kc, nt, preferred_element_type=F32) * SCALE - slope * dist_c
    sp = lax.dot_general(qg, kp, nt, preferred_element_type=F32) * SCALE - slope * (dist_c + float(BLK))
    sc = jnp.where(valid_c, sc, NEG)
    sp = jnp.where(valid_p, sp, NEG)
    m = jnp.maximum(jnp.maximum(jnp.max(sc, axis=-1, keepdims=True), jnp.max(sp, axis=-1, keepdims=True)), sink)
    pc = jnp.exp(sc - m)
    pp = jnp.exp(sp - m)
    ps = jnp.exp(sink - m)
    den = jnp.sum(pc, axis=-1, keepdims=True) + jnp.sum(pp, axis=-1, keepdims=True) + ps
    return qg, kc, kp, pc, pp, ps, den, hidx


def attn_a_fwd(name, qkv, sinks, comm=None):
    kcol, vcol = NQ_A * HD // (NKV_A * HD), NQ_A * HD // (NKV_A * HD) + 1

    def body(s_ref, q_ref, kc_ref, kp_ref, vc_ref, vp_ref, o_ref):
        n = pl.program_id(0)
        for g in range(NKV_A):
            qg, kc, kp, pc, pp, ps, den, _ = _a_common(n, g, s_ref, q_ref, kc_ref, kp_ref)
            vc = vc_ref[:, g * HD:(g + 1) * HD]
            vp = vp_ref[:, g * HD:(g + 1) * HD]
            og = (jnp.dot(pc.astype(BF16), vc, preferred_element_type=F32)
                  + jnp.dot(pp.astype(BF16), vp, preferred_element_type=F32)) / den
            for i in range(GROUP_A):
                h = GROUP_A * g + i
                o_ref[:, h * HD:(h + 1) * HD] = og[i * BLK:(i + 1) * BLK].astype(BF16)

    kvw = NKV_A * HD
    prev = lambda n: jnp.maximum(n - 1, 0)
    outs, got = _pcall(
        name, body, grid=(T // BLK,),
        in_specs=[pl.BlockSpec(memory_space=pltpu.SMEM),
                  pl.BlockSpec((BLK, NQ_A * HD), lambda n: (n, 0)),
                  pl.BlockSpec((BLK, kvw), lambda n: (n, kcol)),
                  pl.BlockSpec((BLK, kvw), lambda n: (prev(n), kcol)),
                  pl.BlockSpec((BLK, kvw), lambda n: (n, vcol)),
                  pl.BlockSpec((BLK, kvw), lambda n: (prev(n), vcol))],
        out_specs=[pl.BlockSpec((BLK, NQ_A * HD), lambda n: (n, 0))],
        out_shape=[jax.ShapeDtypeStruct((T, NQ_A * HD), BF16)],
        sem=("parallel",), args=(sinks, qkv, qkv, qkv, qkv, qkv), comm=comm)
    return outs[0], got


def attn_a_bwd(name, qkv, sinks, o, do, comm=None):
    kcol, vcol = NQ_A * HD // (NKV_A * HD), NQ_A * HD // (NKV_A * HD) + 1
    nb = T // BLK
    kvw = NKV_A * HD

    def body(s_ref, q_ref, kc_ref, kp_ref, vc_ref, vp_ref, o_ref, do_ref,
             dq_ref, dk_ref, dv_ref, ds_ref, ck_ref, cv_ref):
        n = pl.program_id(0)

        @pl.when(n == 0)
        def _():
            ds_ref[...] = jnp.zeros_like(ds_ref)
            ck_ref[...] = jnp.zeros_like(ck_ref)
            cv_ref[...] = jnp.zeros_like(cv_ref)

        @pl.when(n == nb)
        def _():
            dk_ref[...] = ck_ref[...].astype(BF16)
            dv_ref[...] = cv_ref[...].astype(BF16)

        @pl.when(n < nb)
        def _():
            lane = lax.broadcasted_iota(jnp.int32, (1, 128), 1)
            dsink_row = jnp.zeros((1, 128), F32)
            tn = (((0,), (0,)), ((), ()))
            nt = (((1,), (1,)), ((), ()))
            for g in range(NKV_A):
                qg, kc, kp, pc, pp, ps, den, hidx = _a_common(n, g, s_ref, q_ref, kc_ref, kp_ref)
                vc = vc_ref[:, g * HD:(g + 1) * HD]
                vp = vp_ref[:, g * HD:(g + 1) * HD]
                cols = [slice((GROUP_A * g + i) * HD, (GROUP_A * g + i + 1) * HD) for i in range(GROUP_A)]
                og = jnp.concatenate([o_ref[:, c] for c in cols], axis=0)
                dog = jnp.concatenate([do_ref[:, c] for c in cols], axis=0)
                delta = jnp.sum(og.astype(F32) * dog.astype(F32), axis=-1, keepdims=True)
                inv = 1.0 / den
                p_c, p_p = pc * inv, pp * inv
                dsc = p_c * (lax.dot_general(dog, vc, nt, preferred_element_type=F32) - delta)
                dsp = p_p * (lax.dot_general(dog, vp, nt, preferred_element_type=F32) - delta)
                dsk = -(ps * inv) * delta
                for i in range(GROUP_A):
                    tot = jnp.sum(jnp.where(hidx == i, dsk, 0.0), axis=0, keepdims=True)
                    dsink_row = dsink_row + jnp.where(lane == GROUP_A * g + i, tot, 0.0)
                dscb, dspb = (dsc * SCALE).astype(BF16), (dsp * SCALE).astype(BF16)
                dqg = (jnp.dot(dscb, kc, preferred_element_type=F32)
                       + jnp.dot(dspb, kp, preferred_element_type=F32))
                for i in range(GROUP_A):
                    dq_ref[:, cols[i]] = dqg[i * BLK:(i + 1) * BLK].astype(BF16)
                gs = slice(g * HD, (g + 1) * HD)
                dk_ref[:, gs] = (ck_ref[:, gs] + lax.dot_general(dspb, qg, tn, preferred_element_type=F32)).astype(BF16)
                dv_ref[:, gs] = (cv_ref[:, gs] + lax.dot_general(p_p.astype(BF16), dog, tn, preferred_element_type=F32)).astype(BF16)
                ck_ref[:, gs] = lax.dot_general(dscb, qg, tn, preferred_element_type=F32)
                cv_ref[:, gs] = lax.dot_general(p_c.astype(BF16), dog, tn, preferred_element_type=F32)
            ds_ref[...] += jnp.broadcast_to(dsink_row, ds_ref.shape)

    cur = lambda n: jnp.minimum(n, nb - 1)
    prev = lambda n: jnp.maximum(jnp.minimum(n, nb - 1) - 1, 0)
    outp = lambda n: jnp.maximum(n - 1, 0)
    qspec = pl.BlockSpec((BLK, NQ_A * HD), lambda n: (cur(n), 0))
    return _pcall(
        name, body, grid=(nb + 1,),
        in_specs=[pl.BlockSpec(memory_space=pltpu.SMEM), qspec,
                  pl.BlockSpec((BLK, kvw), lambda n: (cur(n), kcol)),
                  pl.BlockSpec((BLK, kvw), lambda n: (prev(n), kcol)),
                  pl.BlockSpec((BLK, kvw), lambda n: (cur(n), vcol)),
                  pl.BlockSpec((BLK, kvw), lambda n: (prev(n), vcol)),
                  qspec, qspec],
        out_specs=[qspec, pl.BlockSpec((BLK, kvw), lambda n: (outp(n), 0)),
                   pl.BlockSpec((BLK, kvw), lambda n: (outp(n), 0)),
                   pl.BlockSpec((8, 128), lambda n: (0, 0))],
        out_shape=[jax.ShapeDtypeStruct((T, NQ_A * HD), BF16), jax.ShapeDtypeStruct((T, kvw), BF16),
                   jax.ShapeDtypeStruct((T, kvw), BF16), jax.ShapeDtypeStruct((8, 128), F32)],
        scratch_shapes=[pltpu.VMEM((BLK, kvw), F32), pltpu.VMEM((BLK, kvw), F32)],
        sem=("arbitrary",), args=(sinks, qkv, qkv, qkv, qkv, qkv, o, do), comm=comm)


def _b_logs(z):
    l1p = jnp.log(1.0 + jnp.exp(-jnp.abs(z)))
    return jnp.minimum(z, 0.0) - l1p, -jnp.maximum(z, 0.0) - l1p


def _split_dot(x, m):
    hi = x.astype(BF16)
    lo = (x - hi.astype(F32)).astype(BF16)
    return jnp.dot(hi, m, preferred_element_type=F32) + jnp.dot(lo, m, preferred_element_type=F32)


def _b_tile(j, n, qh, k_ref, hs, c_t, mgt, tri):
    kj = k_ref[_b_rows(jnp.maximum(j, 0)), hs]
    z = lax.dot_general(qh, kj, (((1,), (1,)), ((), ())), preferred_element_type=F32) * SCALE
    lb, lm = _b_logs(z)
    valid = jnp.logical_or(jnp.logical_and(tri, j == n), jnp.logical_and(j >= 0, j < n))
    lm = jnp.where(valid, lm, 0.0)
    a = jnp.where(valid, jnp.exp(lb + c_t + _split_dot(lm, mgt)), 0.0)
    return z, a, c_t + jnp.sum(lm, axis=-1, keepdims=True)


B_CUT = 105.0
B_HEADS = (slice(0, HD), slice(HD, 2 * HD))
B_UNROLL = 3


def _b_masks():
    r = lax.broadcasted_iota(jnp.int32, (BLK, BLK), 0)
    c = lax.broadcasted_iota(jnp.int32, (BLK, BLK), 1)
    return (r > c).astype(BF16), (r < c).astype(BF16), c < r


def _b_rows(j):
    return pl.ds(pl.multiple_of(j * BLK, BLK), BLK)


def _b_live(c0, c1):
    return jnp.maximum(jnp.max(c0), jnp.max(c1)) > -B_CUT


def attn_b_fwd(name, qkv, comm=None):
    npair = NH_B // 2

    def body(q_ref, k_ref, v_ref, o_ref):
        n = pl.program_id(1)
        mgt, _, tri = _b_masks()
        qs = [q_ref[:, hs] for hs in B_HEADS]

        def group(j0, st):
            st = list(st)
            for u in range(B_UNROLL):
                j = j0 - u
                for h, hs in enumerate(B_HEADS):
                    c_t, acc = st[h]
                    _, a, c_t = _b_tile(j, n, qs[h], k_ref, hs, c_t, mgt, tri)
                    vj = v_ref[_b_rows(jnp.maximum(j, 0)), hs]
                    st[h] = (c_t, acc + jnp.dot(a.astype(BF16), vj, preferred_element_type=F32))
            return tuple(st)

        zero = (jnp.zeros((BLK, 1), F32), jnp.zeros((BLK, HD), F32))
        _, st = lax.while_loop(lambda cr: jnp.logical_and(cr[0] >= 0, _b_live(cr[1][0][0], cr[1][1][0])),
                               lambda cr: (cr[0] - B_UNROLL, group(cr[0], cr[1])), (n, (zero, zero)))
        for h, hs in enumerate(B_HEADS):
            o_ref[:, hs] = st[h][1].astype(BF16)

    outs, got = _pcall(
        name, body, grid=(npair, T // BLK),
        in_specs=[pl.BlockSpec((BLK, 2 * HD), lambda p, n: (n, p)),
                  pl.BlockSpec((T, 2 * HD), lambda p, n: (0, npair + p)),
                  pl.BlockSpec((T, 2 * HD), lambda p, n: (0, 2 * npair + p))],
        out_specs=[pl.BlockSpec((BLK, 2 * HD), lambda p, n: (n, p))],
        out_shape=[jax.ShapeDtypeStruct((T, NH_B * HD), BF16)],
        sem=("parallel", "arbitrary"), args=(qkv, qkv, qkv), comm=comm)
    return outs[0], got


def attn_b_bwd(name, qkv, do, comm=None):
    npair = NH_B // 2
    nb = T // BLK

    def body(q_ref, do_ref, k_ref, v_ref, dq_ref, dk_ref, dv_ref, a_s, z_s):
        n = pl.program_id(1)

        @pl.when(n == 0)
        def _():
            dk_ref[...] = jnp.zeros_like(dk_ref)
            dv_ref[...] = jnp.zeros_like(dv_ref)

        mgt, mlt, tri = _b_masks()
        nt = (((1,), (1,)), ((), ()))
        tn = (((0,), (0,)), ((), ()))
        qs = [q_ref[:, hs] for hs in B_HEADS]
        dos = [do_ref[:, hs] for hs in B_HEADS]

        slot = lambda j: j + (B_UNROLL - 1)

        def sweep1(j0, cs):
            cs = list(cs)
            for u in range(B_UNROLL):
                j = j0 - u
                for h, hs in enumerate(B_HEADS):
                    z, a, cs[h] = _b_tile(j, n, qs[h], k_ref, hs, cs[h], mgt, tri)
                    a_s[h, slot(j)] = a
                    z_s[h, slot(j)] = z
            return tuple(cs)

        j_end, groups, _ = lax.while_loop(
            lambda cr: jnp.logical_and(cr[0] >= 0, _b_live(cr[2][0], cr[2][1])),
            lambda cr: (cr[0] - B_UNROLL, cr[1] + 1, sweep1(cr[0], cr[2])),
            (n, 0, (jnp.zeros((BLK, 1), F32),) * 2))

        def sweep2(gi, st):
            st = list(st)
            for u in range(B_UNROLL):
                j = j_end + 1 + gi * B_UNROLL + u
                rows = _b_rows(jnp.maximum(j, 0))
                valid = jnp.logical_or(jnp.logical_and(tri, j == n), jnp.logical_and(j >= 0, j < n))
                for h, hs in enumerate(B_HEADS):
                    cg, dq = st[h]
                    kj, vj = k_ref[rows, hs], v_ref[rows, hs]
                    a, z = a_s[h, slot(j)], z_s[h, slot(j)]
                    g = a * lax.dot_general(dos[h], vj, nt, preferred_element_type=F32)
                    big_g = cg + _split_dot(g, mlt)
                    e = jnp.exp(-jnp.abs(z))
                    inv = 1.0 / (1.0 + e)
                    sig = jnp.where(z >= 0, inv, e * inv)
                    dz = jnp.where(valid, g * (1.0 - sig) - big_g * sig, 0.0)
                    dzb = (dz * SCALE).astype(BF16)
                    dk_ref[rows, hs] += lax.dot_general(dzb, qs[h], tn, preferred_element_type=F32)
                    dv_ref[rows, hs] += lax.dot_general(a.astype(BF16), dos[h], tn, preferred_element_type=F32)
                    st[h] = (cg + jnp.sum(g, axis=-1, keepdims=True),
                             dq + jnp.dot(dzb, kj, preferred_element_type=F32))
            return tuple(st)

        zero = (jnp.zeros((BLK, 1), F32), jnp.zeros((BLK, HD), F32))
        st = lax.fori_loop(0, groups, sweep2, (zero, zero))
        for h, hs in enumerate(B_HEADS):
            dq_ref[:, hs] = st[h][1].astype(BF16)

    tile = pl.BlockSpec((BLK, 2 * HD), lambda p, n: (n, p))
    full = lambda off: pl.BlockSpec((T, 2 * HD), lambda p, n: (0, off + p))
    return _pcall(
        name, body, grid=(npair, nb),
        in_specs=[tile, tile, full(npair), full(2 * npair)],
        out_specs=[tile, full(0), full(0)],
        out_shape=[jax.ShapeDtypeStruct((T, NH_B * HD), BF16), jax.ShapeDtypeStruct((T, NH_B * HD), F32),
                   jax.ShapeDtypeStruct((T, NH_B * HD), F32)],
        scratch_shapes=[pltpu.VMEM((2, nb + B_UNROLL - 1, BLK, BLK), F32)] * 2,
        sem=("parallel", "arbitrary"), args=(qkv, do, qkv, qkv), comm=comm)


def _adamw_math(w, g, m, v):
    m = B1 * m + (1.0 - B1) * g
    v = B2 * v + (1.0 - B2) * (g * g)
    m_hat = m / (1.0 - B1 ** STEP)
    v_hat = v / (1.0 - B2 ** STEP)
    delta = -LR * (m_hat / (jnp.sqrt(v_hat) + EPS) + WD * w)
    return delta, m, v


def adamw(name, parts, w, m, v, rows):
    r, c = w.shape

    def body(p_ref, w_ref, m_ref, v_ref, g_ref, d_ref, nm_ref, nv_ref):
        g = p_ref[0].astype(F32)
        for s in range(1, NDEV):
            g = g + p_ref[s].astype(F32)
        delta, nm, nv = _adamw_math(w_ref[...], g, m_ref[...], v_ref[...])
        g_ref[...] = g
        d_ref[...] = delta
        nm_ref[...] = nm
        nv_ref[...] = nv

    blk = pl.BlockSpec((rows, c), lambda i: (i, 0))
    return pl.pallas_call(
        body, grid=(r // rows,),
        in_specs=[pl.BlockSpec((NDEV, rows, c), lambda i: (0, i, 0)), blk, blk, blk],
        out_specs=[blk] * 4, out_shape=[jax.ShapeDtypeStruct((r, c), F32)] * 4,
        compiler_params=_cp(("parallel",)), name=name)(parts, w, m, v)


def _unshard_cols(g):
    return jnp.transpose(g, (1, 0, 2)).reshape(g.shape[1], -1)


def _shard_cols(w):
    k, n = w.shape
    return jnp.transpose(w.reshape(k, NDEV, n // NDEV), (1, 0, 2))


FWD_CARRY = {
    ("qkv", 0): [(0, 1)], ("attn", 0): [(0, 2)], ("mlp_in", 0): [(0, 3)], ("mlp_out", 0): [(1, 0)],
    ("qkv", 1): [(1, 1)], ("attn", 1): [(1, 2), (1, 3), (2, 0), (2, 1), (2, 2), (2, 3)],
    ("mlp_in", 1): [(3, 0)], ("mlp_out", 1): [(3, 1)],
    ("attn", 2): [(3, 2)], ("mlp_in", 2): [(3, 3)],
}
BWD_CARRY = {
    ("attn_bwd", 3): [(3, 3), (3, 2), (3, 1)],
    ("d_act", 2): [(3, 0)], ("d_h2", 2): [(2, 3)], ("attn_bwd", 2): [(2, 2)], ("d_h1", 2): [(2, 1)],
    ("d_act", 1): [(2, 0)], ("attn_bwd", 1): [(1, 3), (1, 2), (1, 1)],
    ("d_act", 0): [(1, 0)], ("d_h2", 0): [(0, 3)], ("attn_bwd", 0): [(0, 2)], ("d_h1", 0): [(0, 1)],
}


def kernel(x, a_w_qkv, a_w_o, a_sinks, b_w_qkv, b_w_o, norm_mix, norm_mlp, mlp_w_in, mlp_w_out, final_norm, loss_target, m_a_w_qkv, m_a_w_o, m_a_sinks, m_b_w_qkv, m_b_w_o, m_norm_mix, m_norm_mlp, m_mlp_w_in, m_mlp_w_out, m_final_norm, v_a_w_qkv, v_a_w_o, v_a_sinks, v_b_w_qkv, v_b_w_o, v_norm_mix, v_norm_mlp, v_mlp_w_in, v_mlp_w_out, v_final_norm):
    depth = 4
    xs = x[0]

    shards, full, sendbuf, recv = {}, {}, {}, {}
    for i in range(depth):
        j = i // 2
        wq, wo = (a_w_qkv[j], a_w_o[j]) if i % 2 == 0 else (b_w_qkv[j], b_w_o[j])
        for t, w in enumerate((wq, wo, mlp_w_in[i], mlp_w_out[i])):
            shards[(i, t)] = w.astype(BF16)

    def land_weights(keys, slabs):
        for (i, t), g in zip(keys, slabs):
            full[(i, t)] = _unshard_cols(g) if t in (0, 2) else g.reshape(-1, D)

    def land_grads(keys, slabs):
        recv.update(zip(keys, slabs))

    def carried(table, payload, gather, land, kind, i, fn):
        keys = table.get((kind, i), [])
        out, got = fn(Comm([payload[k] for k in keys], gather) if keys else None)
        land(keys, got)
        return out

    fwd = functools.partial(carried, FWD_CARRY, shards, True, land_weights)
    bwd = functools.partial(carried, BWD_CARRY, sendbuf, False, land_grads)

    relu2 = lambda acc: (acc, jnp.square(jnp.maximum(acc, 0.0)))
    add = lambda acc, res: (acc + res,)
    drelu2 = lambda acc, uu: (acc * (2.0 * jnp.maximum(uu.astype(F32), 0.0)),)

    land_weights([(0, 0)], exchange("gather_first", [shards[(0, 0)]], True))
    saved = []
    for i in range(depth):
        x1 = xs
        h1 = rmsnorm_fwd(f"norm_mix{i}", x1, norm_mix[i][None])
        (qkv,) = fwd("qkv", i, lambda c: mm_nn(f"qkv{i}", h1, full[(i, 0)], out_dtype=BF16,
                                                tn=512 if i % 2 == 0 else 768, comm=c))
        if i % 2 == 0:
            o = fwd("attn", i, lambda c: attn_a_fwd(f"attn_a{i}", qkv, a_sinks[i // 2], comm=c))
        else:
            o = fwd("attn", i, lambda c: attn_b_fwd(f"attn_b{i}", qkv, comm=c))
        (x2,) = mm_nn(f"proj{i}", o, full[(i, 1)], out_dtype=F32, extras=(x1,), epilogue=add)[0]
        h2 = rmsnorm_fwd(f"norm_mlp{i}", x2, norm_mlp[i][None])
        u, act = fwd("mlp_in", i, lambda c: mm_nn(f"mlp_in{i}", h2, full[(i, 2)], out_dtype=BF16,
                                                   epilogue=relu2, n_out=2, comm=c))
        (xs,) = fwd("mlp_out", i, lambda c: mm_nn(f"mlp_out{i}", act, full[(i, 3)], out_dtype=F32,
                                                   extras=(x2,), epilogue=add, comm=c))
        saved.append((x1, h1, qkv, o, x2, h2, u, act))

    dx, dxb, d_final, loss_part = loss_head("loss_head", xs, final_norm[None], loss_target[0])
    loss = lax.psum(loss_part[0, 0], ("x", "y", "c"))

    d_mix, d_mlp, d_sinks = [None] * depth, [None] * depth, [None] * 2
    for i in reversed(range(depth)):
        x1, h1, qkv, o, x2, h2, u, act = saved[i]
        sendbuf[(i, 3)] = mm_tn(f"g_mlp_out{i}", act, dxb)[0].reshape(NDEV, DFF // NDEV, D)
        du = bwd("d_act", i, lambda c: mm_nt(f"d_act{i}", dxb, full[(i, 3)], out_dtype=BF16, extras=(u,),
                                             epilogue=drelu2, comm=c))
        sendbuf[(i, 2)] = _shard_cols(mm_tn(f"g_mlp_in{i}", h2, du)[0])
        dh2 = bwd("d_h2", i, lambda c: mm_nt(f"d_h2{i}", du, full[(i, 2)], out_dtype=F32, comm=c))
        dx, dxb, d_mlp[i] = rmsnorm_bwd(f"norm_mlp_bwd{i}", dh2, x2, norm_mlp[i][None], dx)
        sendbuf[(i, 1)] = mm_tn(f"g_proj{i}", o, dxb)[0].reshape(NDEV, D // NDEV, D)
        do = mm_nt(f"d_o{i}", dxb, full[(i, 1)], out_dtype=BF16)[0]
        if i % 2 == 0:
            dq, dk, dv, ds = bwd("attn_bwd", i, lambda c: attn_a_bwd(f"attn_a_bwd{i}", qkv, a_sinks[i // 2],
                                                                       o, do, comm=c))
            d_sinks[i // 2] = ds[0, :NQ_A]
            dqkv = jnp.concatenate([dq, dk, dv], axis=1)
        else:
            dq, dk, dv = bwd("attn_bwd", i, lambda c: attn_b_bwd(f"attn_b_bwd{i}", qkv, do, comm=c))
            dqkv = jnp.concatenate([dq, dk.astype(BF16), dv.astype(BF16)], axis=1)
        sendbuf[(i, 0)] = _shard_cols(mm_tn(f"g_qkv{i}", h1, dqkv, tn=512 if i % 2 == 0 else 768)[0])
        dh1 = bwd("d_h1", i, lambda c: mm_nt(f"d_h1{i}", dqkv, full[(i, 0)], out_dtype=F32, comm=c))
        dx, dxb, d_mix[i] = rmsnorm_bwd(f"norm_mix_bwd{i}", dh1, x1, norm_mix[i][None], dx)

    small = jnp.zeros((16, D), F32)
    small = small.at[0:4].set(jnp.concatenate(d_mix, axis=0)).at[4:8].set(jnp.concatenate(d_mlp, axis=0))
    small = small.at[8].set(d_final[0]).at[9, :2 * NQ_A].set(jnp.concatenate(d_sinks))
    recv[(0, 0)], small_parts = exchange("exchange_last", [sendbuf[(0, 0)], small], [False, True])
    assert len(recv) == 4 * depth and len(full) == 4 * depth
    pack = lambda mix, mlp, fin, snk: (jnp.zeros((16, D), F32).at[0:4].set(mix).at[4:8].set(mlp)
                                       .at[8].set(fin).at[9, :2 * NQ_A].set(snk.reshape(-1)))
    sm = adamw("adamw_small", small_parts, pack(norm_mix, norm_mlp, final_norm, a_sinks),
               pack(m_norm_mix, m_norm_mlp, m_final_norm, m_a_sinks),
               pack(v_norm_mix, v_norm_mlp, v_final_norm, v_a_sinks), rows=16)
    unpack = lambda a: (a[0:4], a[4:8], a[8], a[9, :2 * NQ_A].reshape(2, NQ_A))

    def layer_update(i, t, w, m, v, rows):
        return adamw(f"adamw{i}_{t}", recv[(i, t)], w, m, v, rows)

    res = {}
    for i in range(depth):
        j = i // 2
        if i % 2 == 0:
            res[("qkv", i)] = layer_update(i, 0, a_w_qkv[j], m_a_w_qkv[j], v_a_w_qkv[j], 256)
            res[("o", i)] = layer_update(i, 1, a_w_o[j], m_a_w_o[j], v_a_w_o[j], 128)
        else:
            res[("qkv", i)] = layer_update(i, 0, b_w_qkv[j], m_b_w_qkv[j], v_b_w_qkv[j], 256)
            res[("o", i)] = layer_update(i, 1, b_w_o[j], m_b_w_o[j], v_b_w_o[j], 128)
        res[("in", i)] = layer_update(i, 2, mlp_w_in[i], m_mlp_w_in[i], v_mlp_w_in[i], 256)
        res[("out", i)] = layer_update(i, 3, mlp_w_out[i], m_mlp_w_out[i], v_mlp_w_out[i], 128)

    outs = []
    for q in range(4):
        mix, mlp, fin, snk = unpack(sm[q])
        outs.append([
            jnp.stack([res[("qkv", 0)][q], res[("qkv", 2)][q]]),
            jnp.stack([res[("o", 0)][q], res[("o", 2)][q]]),
            snk,
            jnp.stack([res[("qkv", 1)][q], res[("qkv", 3)][q]]),
            jnp.stack([res[("o", 1)][q], res[("o", 3)][q]]),
            mix, mlp,
            jnp.stack([res[("in", i)][q] for i in range(depth)]),
            jnp.stack([res[("out", i)][q] for i in range(depth)]),
            fin,
        ])
    return (loss, dx[None], *outs[0], *outs[1], *outs[2], *outs[3])
```

```python
import functools
import math

import jax
import jax.numpy as jnp
from jax import lax
from jax.experimental import pallas as pl
from jax.experimental.pallas import tpu as pltpu

F32, BF16 = jnp.float32, jnp.bfloat16
T, D, DFF = 4096, 2048, 8192
HD, BLK = 64, 128
NQ_A, NKV_A, GROUP_A = 32, 4, 8
NH_B = 32
NDEV = 8
RMS_EPS = 1e-5
SCALE = 0.125
NEG = -1e30
VMEM_LIMIT = 56 * 1024 * 1024
LR, B1, B2, EPS, WD, STEP = 0.001, 0.9, 0.999, 1e-08, 0.01, 10
MESH = pl.DeviceIdType.MESH


def _cp(sem, vmem=VMEM_LIMIT):
    return pltpu.CompilerParams(dimension_semantics=sem, vmem_limit_bytes=vmem)


class Comm:
    def __init__(self, ins, gather):
        self.ins, self.nt = list(ins), len(ins)
        self.gather = list(gather) if isinstance(gather, (list, tuple)) else [gather] * self.nt
        self.out_shape = [jax.ShapeDtypeStruct((NDEV,) + (a.shape if g else a.shape[1:]), a.dtype)
                          for a, g in zip(self.ins, self.gather)]
        self.scratch = [pltpu.SemaphoreType.DMA((7 * self.nt,)), pltpu.SemaphoreType.DMA((7 * self.nt,)),
                        pltpu.SemaphoreType.DMA((self.nt,))]

    def _copies(self, src, dst, sems, *kinds):
        send_sems, recv_sems, local_sems = sems
        x, y, c = lax.axis_index("x"), lax.axis_index("y"), lax.axis_index("c")
        me, sib = (x, y, c), (x, y, 1 - c)
        chips = [(1 - x, y), (x, 1 - y), (1 - x, 1 - y)]
        p = {kind: [] for kind in kinds}
        for t in range(self.nt):
            def slot(d, t=t):
                return dst[t].at[4 * d[0] + 2 * d[1] + d[2]]

            def add(kind, k, a, b, to, t=t):
                if kind in p:
                    p[kind].append(pltpu.make_async_remote_copy(
                        src_ref=a, dst_ref=b, send_sem=send_sems.at[t * 7 + k],
                        recv_sem=recv_sems.at[t * 7 + k], device_id=to, device_id_type=MESH))

            own = src[t] if self.gather[t] else src[t].at[4 * x + 2 * y + c]
            if "local" in p:
                p["local"].append(pltpu.make_async_copy(own, slot(me), local_sems.at[t]))
            if self.gather[t]:
                add("first", 0, own, slot(me), sib)
                add("first_in", 0, own, slot(sib), sib)
                for j, chip in enumerate(chips):
                    there = (*chip, c)
                    add("first", 1 + j, own, slot(me), there)
                    add("chip_in", 1 + j, own, slot(there), there)
                    add("relay", 4 + j, slot(there), slot(there), sib)
                    add("relay_in", 4 + j, own, slot((*chip, 1 - c)), sib)
            else:
                for k in range(NDEV - 1):
                    bits = k + 1
                    peer = (1 - x if bits & 4 else x, 1 - y if bits & 2 else y, 1 - c if bits & 1 else c)
                    out = src[t].at[4 * peer[0] + 2 * peer[1] + peer[2]]
                    add("first", k, out, slot(me), peer)
                    add("first_in", k, out, slot(peer), peer)
        return [p[kind] for kind in kinds]

    def start(self, src, dst, sems):
        local, first = self._copies(src, dst, sems, "local", "first")
        for cp in local + first:
            cp.start()

    def relay(self, src, dst, sems):
        chip_in, relay = self._copies(src, dst, sems, "chip_in", "relay")
        for arrived, onward in zip(chip_in, relay):
            arrived.wait_recv()
            onward.start()

    def finish(self, src, dst, sems):
        first_in, relay_in, first, relay, local = self._copies(
            src, dst, sems, "first_in", "relay_in", "first", "relay", "local")
        for cp in first_in + relay_in:
            cp.wait_recv()
        for cp in first + relay:
            cp.wait_send()
        for cp in local:
            cp.wait()


def _pcall(name, body, *, grid, in_specs, out_specs, out_shape, scratch_shapes=(), sem, args, comm=None):
    in_specs, out_specs, out_shape = list(in_specs), list(out_specs), list(out_shape)
    scratch_shapes = list(scratch_shapes)
    if comm is None:
        outs = pl.pallas_call(body, grid=grid, in_specs=in_specs, out_specs=out_specs, out_shape=out_shape,
                              scratch_shapes=scratch_shapes, compiler_params=_cp(sem), name=name)(*args)
        return list(outs), []
    ni, no, ns, nt = len(in_specs), len(out_specs), len(scratch_shapes), comm.nt
    steps = math.prod(grid)
    hbm = pl.BlockSpec(memory_space=pltpu.HBM)

    def carrier(*refs):
        ins, cin = refs[:ni], refs[ni:ni + nt]
        outs, cout = refs[ni + nt:ni + nt + no], refs[ni + nt + no:ni + 2 * nt + no]
        scr, sems = refs[ni + 2 * nt + no:ni + 2 * nt + no + ns], refs[ni + 2 * nt + no + ns:]
        step = functools.reduce(lambda acc, d: acc * grid[d] + pl.program_id(d), range(len(grid)), 0)

        @pl.when(step == 0)
        def _():
            comm.start(cin, cout, sems)

        body(*ins, *outs, *scr)

        @pl.when(step == (7 * steps) // 8)
        def _():
            comm.relay(cin, cout, sems)

        @pl.when(step == steps - 1)
        def _():
            comm.finish(cin, cout, sems)

    outs = pl.pallas_call(
        carrier, grid=grid, in_specs=in_specs + [hbm] * nt, out_specs=out_specs + [hbm] * nt,
        out_shape=out_shape + comm.out_shape, scratch_shapes=scratch_shapes + comm.scratch,
        compiler_params=_cp(("arbitrary",) * len(grid)), name=name)(*args, *comm.ins)
    return list(outs[:no]), list(outs[no:])


def exchange(name, ins, gather):
    comm = Comm(ins, gather)
    hbm = pl.BlockSpec(memory_space=pltpu.HBM)

    def body(*refs):
        src, dst, sems = refs[:comm.nt], refs[comm.nt:2 * comm.nt], refs[2 * comm.nt:]
        comm.start(src, dst, sems)
        comm.relay(src, dst, sems)
        comm.finish(src, dst, sems)

    return pl.pallas_call(body, in_specs=[hbm] * comm.nt, out_specs=[hbm] * comm.nt, out_shape=comm.out_shape,
                          scratch_shapes=comm.scratch, name=name)(*ins)


def _matmul(name, a, b, extras, *, grid, a_spec, b_spec, extra_specs, out_shapes, out_specs,
            contract, epilogue, acc_shape, comm=None):
    nk = grid[2]
    ne, no = len(extras), len(out_shapes)

    def body(a_ref, b_ref, *rest):
        ex, outs, acc = rest[:ne], rest[ne:ne + no], rest[ne + no]
        kk = pl.program_id(2)

        @pl.when(kk == 0)
        def _():
            acc[...] = jnp.zeros_like(acc)

        acc[...] += lax.dot_general(a_ref[...].astype(BF16), b_ref[...].astype(BF16),
                                    (contract, ((), ())), preferred_element_type=F32)

        @pl.when(kk == nk - 1)
        def _():
            res = epilogue(acc[...], *[r[...] for r in ex])
            for r, o in zip(outs, res):
                r[...] = o.astype(r.dtype)

    return _pcall(name, body, grid=grid, in_specs=[a_spec, b_spec, *extra_specs], out_specs=out_specs,
                  out_shape=out_shapes, scratch_shapes=[pltpu.VMEM(acc_shape, F32)],
                  sem=("parallel", "parallel", "arbitrary"), args=(a, b, *extras), comm=comm)


def _ident(acc):
    return (acc,)


def _fit(tile, dim):
    return max(t for t in range(128, min(tile, dim) + 1, 128) if dim % t == 0)


def mm_nn(name, a, w, *, out_dtype, tm=1024, tn=1024, tk=2048, extras=(), epilogue=_ident, n_out=1,
          out_dtypes=None, comm=None):
    m, k = a.shape
    tm, tk = _fit(tm, m), _fit(tk, k)
    if w.ndim == 3:
        tn = w.shape[2]
        n = NDEV * tn
        b_spec = pl.BlockSpec((None, tk, tn), lambda i, j, kk: (j, kk, 0))
    else:
        n = w.shape[1]
        tn = _fit(tn, n)
        b_spec = pl.BlockSpec((tk, tn), lambda i, j, kk: (kk, j))
    out_dtypes = out_dtypes or (out_dtype,) * n_out
    mn = pl.BlockSpec((tm, tn), lambda i, j, kk: (i, j))
    return _matmul(name, a, w, extras, grid=(m // tm, n // tn, k // tk),
                   a_spec=pl.BlockSpec((tm, tk), lambda i, j, kk: (i, kk)), b_spec=b_spec,
                   extra_specs=[mn] * len(extras),
                   out_shapes=[jax.ShapeDtypeStruct((m, n), dt) for dt in out_dtypes],
                   out_specs=[mn] * len(out_dtypes), contract=((1,), (0,)), epilogue=epilogue,
                   acc_shape=(tm, tn), comm=comm)


def mm_nt(name, a, w, *, out_dtype, tm=1024, tn=1024, tk=2048, extras=(), epilogue=_ident, comm=None):
    m, n = a.shape
    k = w.shape[-2]
    tm, to = _fit(tm, m), _fit(tn, k)
    if w.ndim == 3:
        tc = w.shape[2]
        b_spec = pl.BlockSpec((None, to, tc), lambda i, j, kk: (kk, j, 0))
    else:
        tc = _fit(tk, n)
        b_spec = pl.BlockSpec((to, tc), lambda i, j, kk: (j, kk))
    mo = pl.BlockSpec((tm, to), lambda i, j, kk: (i, j))
    outs, got = _matmul(name, a, w, extras, grid=(m // tm, k // to, n // tc),
                        a_spec=pl.BlockSpec((tm, tc), lambda i, j, kk: (i, kk)), b_spec=b_spec,
                        extra_specs=[mo] * len(extras),
                        out_shapes=[jax.ShapeDtypeStruct((m, k), out_dtype)],
                        out_specs=[mo], contract=((1,), (1,)), epilogue=epilogue, acc_shape=(tm, to),
                        comm=comm)
    return outs[0], got


def mm_tn(name, a, b, *, out_dtype=BF16, tm=1024, tn=1024, tk=2048, col_shards=False, comm=None):
    t, k = a.shape
    n = b.shape[1]
    tm, tk = _fit(tm, k), _fit(tk, t)
    if col_shards:
        tn = n // NDEV
        shape, spec = (NDEV, k, tn), pl.BlockSpec((None, tm, tn), lambda i, j, kk: (j, i, 0))
    else:
        tn = _fit(tn, n)
        shape, spec = (k, n), pl.BlockSpec((tm, tn), lambda i, j, kk: (i, j))
    outs, got = _matmul(name, a, b, (), grid=(k // tm, n // tn, t // tk),
                        a_spec=pl.BlockSpec((tk, tm), lambda i, j, kk: (kk, i)),
                        b_spec=pl.BlockSpec((tk, tn), lambda i, j, kk: (kk, j)),
                        extra_specs=[], out_shapes=[jax.ShapeDtypeStruct(shape, out_dtype)],
                        out_specs=[spec],
                        contract=((0,), (0,)), epilogue=_ident, acc_shape=(tm, tn), comm=comm)
    return outs[0], got


ROWS = 256


def rmsnorm_fwd(name, x, gain):
    def body(x_ref, g_ref, h_ref):
        xf = x_ref[...]
        rstd = lax.rsqrt(jnp.mean(xf * xf, axis=-1, keepdims=True) + RMS_EPS)
        h_ref[...] = (xf * rstd * g_ref[...]).astype(BF16)

    row = pl.BlockSpec((ROWS, D), lambda i: (i, 0))
    return pl.pallas_call(body, grid=(T // ROWS,), in_specs=[row, pl.BlockSpec((1, D), lambda i: (0, 0))],
                          out_specs=row, out_shape=jax.ShapeDtypeStruct((T, D), BF16),
                          compiler_params=_cp(("parallel",)), name=name)(x, gain)


def _rms_bwd_math(dh, xf, gain):
    rstd = lax.rsqrt(jnp.mean(xf * xf, axis=-1, keepdims=True) + RMS_EPS)
    xhat = xf * rstd
    dgain = jnp.sum(dh * xhat, axis=0, keepdims=True)
    dxhat = dh * gain
    dx = rstd * (dxhat - xhat * jnp.mean(dxhat * xhat, axis=-1, keepdims=True))
    return dx, dgain


def rmsnorm_bwd(name, dh, x, gain, dres):
    def body(dh_ref, x_ref, g_ref, dres_ref, dx_ref, dxb_ref, dg_ref):
        dx, dgain = _rms_bwd_math(dh_ref[...], x_ref[...], g_ref[...])
        dx = dx + dres_ref[...]
        dx_ref[...] = dx
        dxb_ref[...] = dx.astype(BF16)

        @pl.when(pl.program_id(0) == 0)
        def _():
            dg_ref[...] = jnp.zeros_like(dg_ref)

        dg_ref[...] += dgain

    row = pl.BlockSpec((ROWS, D), lambda i: (i, 0))
    vec = pl.BlockSpec((1, D), lambda i: (0, 0))
    return pl.pallas_call(
        body, grid=(T // ROWS,), in_specs=[row, row, vec, row], out_specs=[row, row, vec],
        out_shape=[jax.ShapeDtypeStruct((T, D), F32), jax.ShapeDtypeStruct((T, D), BF16),
                   jax.ShapeDtypeStruct((1, D), F32)],
        compiler_params=_cp(("arbitrary",)), name=name)(dh, x, gain, dres)


def loss_head(name, x, gain, target):
    def body(x_ref, g_ref, t_ref, dx_ref, dxb_ref, dg_ref, loss_ref):
        xf, gain = x_ref[...], g_ref[...]
        rstd = lax.rsqrt(jnp.mean(xf * xf, axis=-1, keepdims=True) + RMS_EPS)
        err = xf * rstd * gain - t_ref[...]
        part = 0.5 * jnp.sum(jnp.mean(err * err, axis=-1, keepdims=True), axis=0, keepdims=True)
        dx, dgain = _rms_bwd_math(err * (1.0 / D), xf, gain)
        dx_ref[...] = dx
        dxb_ref[...] = dx.astype(BF16)

        @pl.when(pl.program_id(0) == 0)
        def _():
            dg_ref[...] = jnp.zeros_like(dg_ref)
            loss_ref[...] = jnp.zeros_like(loss_ref)

        dg_ref[...] += dgain
        loss_ref[...] += jnp.broadcast_to(part, loss_ref.shape)

    row = pl.BlockSpec((ROWS, D), lambda i: (i, 0))
    vec = pl.BlockSpec((1, D), lambda i: (0, 0))
    return pl.pallas_call(
        body, grid=(T // ROWS,), in_specs=[row, vec, row],
        out_specs=[row, row, vec, pl.BlockSpec((1, 128), lambda i: (0, 0))],
        out_shape=[jax.ShapeDtypeStruct((T, D), F32), jax.ShapeDtypeStruct((T, D), BF16),
                   jax.ShapeDtypeStruct((1, D), F32), jax.ShapeDtypeStruct((1, 128), F32)],
        compiler_params=_cp(("arbitrary",)), name=name)(x, gain, target)


def _a_common(n, g, s_ref, q_ref, kc_ref, kp_ref):
    row = lax.broadcasted_iota(jnp.int32, (GROUP_A * BLK, BLK), 0)
    col = lax.broadcasted_iota(jnp.int32, (GROUP_A * BLK, BLK), 1)
    qi = row & (BLK - 1)
    dist_c = (qi - col).astype(F32)
    valid_c = qi >= col
    valid_p = jnp.logical_and(col > qi, n > 0)
    hidx = lax.broadcasted_iota(jnp.int32, (GROUP_A * BLK, 1), 0) >> 7
    slope = jnp.exp((-math.log(2.0) / 4.0) * (hidx + (GROUP_A * g + 1)).astype(F32))
    sink = jnp.zeros((GROUP_A * BLK, 1), F32)
    for i in range(GROUP_A):
        sink = jnp.where(hidx == i, s_ref[GROUP_A * g + i], sink)
    qg = jnp.concatenate([q_ref[:, (GROUP_A * g + i) * HD:(GROUP_A * g + i + 1) * HD]
                          for i in range(GROUP_A)], axis=0)
    kc = kc_ref[:, g * HD:(g + 1) * HD]
    kp = kp_ref[:, g * HD:(g + 1) * HD]
    nt = (((1,), (1,)), ((), ()))
    sc = lax.dot_general(qg, kc, nt, preferred_element_type=F32) * SCALE - slope * dist_c
    sp = lax.dot_general(qg, kp, nt, preferred_element_type=F32) * SCALE - slope * (dist_c + float(BLK))
    sc = jnp.where(valid_c, sc, NEG)
    sp = jnp.where(valid_p, sp, NEG)
    m = jnp.maximum(jnp.maximum(jnp.max(sc, axis=-1, keepdims=True), jnp.max(sp, axis=-1, keepdims=True)), sink)
    pc = jnp.exp(sc - m)
    pp = jnp.exp(sp - m)
    ps = jnp.exp(sink - m)
    den = jnp.sum(pc, axis=-1, keepdims=True) + jnp.sum(pp, axis=-1, keepdims=True) + ps
    return qg, kc, kp, pc, pp, ps, den, hidx


def attn_a_fwd(name, qkv, sinks, comm=None):
    kcol, vcol = NQ_A * HD // (NKV_A * HD), NQ_A * HD // (NKV_A * HD) + 1

    def body(s_ref, q_ref, kc_ref, kp_ref, vc_ref, vp_ref, o_ref):
        n = pl.program_id(0)
        for g in range(NKV_A):
            qg, kc, kp, pc, pp, ps, den, _ = _a_common(n, g, s_ref, q_ref, kc_ref, kp_ref)
            vc = vc_ref[:, g * HD:(g + 1) * HD]
            vp = vp_ref[:, g * HD:(g + 1) * HD]
            og = (jnp.dot(pc.astype(BF16), vc, preferred_element_type=F32)
                  + jnp.dot(pp.astype(BF16), vp, preferred_element_type=F32)) / den
            for i in range(GROUP_A):
                h = GROUP_A * g + i
                o_ref[:, h * HD:(h + 1) * HD] = og[i * BLK:(i + 1) * BLK].astype(BF16)

    kvw = NKV_A * HD
    prev = lambda n: jnp.maximum(n - 1, 0)
    outs, got = _pcall(
        name, body, grid=(T // BLK,),
        in_specs=[pl.BlockSpec(memory_space=pltpu.SMEM),
                  pl.BlockSpec((BLK, NQ_A * HD), lambda n: (n, 0)),
                  pl.BlockSpec((BLK, kvw), lambda n: (n, kcol)),
                  pl.BlockSpec((BLK, kvw), lambda n: (prev(n), kcol)),
                  pl.BlockSpec((BLK, kvw), lambda n: (n, vcol)),
                  pl.BlockSpec((BLK, kvw), lambda n: (prev(n), vcol))],
        out_specs=[pl.BlockSpec((BLK, NQ_A * HD), lambda n: (n, 0))],
        out_shape=[jax.ShapeDtypeStruct((T, NQ_A * HD), BF16)],
        sem=("parallel",), args=(sinks, qkv, qkv, qkv, qkv, qkv), comm=comm)
    return outs[0], got


def attn_a_bwd(name, qkv, sinks, o, do, comm=None):
    kcol, vcol = NQ_A * HD // (NKV_A * HD), NQ_A * HD // (NKV_A * HD) + 1
    nb = T // BLK
    kvw = NKV_A * HD

    def body(s_ref, q_ref, kc_ref, kp_ref, vc_ref, vp_ref, o_ref, do_ref,
             dq_ref, dk_ref, dv_ref, ds_ref, ck_ref, cv_ref):
        n = pl.program_id(0)

        @pl.when(n == 0)
        def _():
            ds_ref[...] = jnp.zeros_like(ds_ref)
            ck_ref[...] = jnp.zeros_like(ck_ref)
            cv_ref[...] = jnp.zeros_like(cv_ref)

        @pl.when(n == nb)
        def _():
            dk_ref[...] = ck_ref[...].astype(BF16)
            dv_ref[...] = cv_ref[...].astype(BF16)

        @pl.when(n < nb)
        def _():
            lane = lax.broadcasted_iota(jnp.int32, (1, 128), 1)
            dsink_row = jnp.zeros((1, 128), F32)
            tn = (((0,), (0,)), ((), ()))
            nt = (((1,), (1,)), ((), ()))
            for g in range(NKV_A):
                qg, kc, kp, pc, pp, ps, den, hidx = _a_common(n, g, s_ref, q_ref, kc_ref, kp_ref)
                vc = vc_ref[:, g * HD:(g + 1) * HD]
                vp = vp_ref[:, g * HD:(g + 1) * HD]
                cols = [slice((GROUP_A * g + i) * HD, (GROUP_A * g + i + 1) * HD) for i in range(GROUP_A)]
                og = jnp.concatenate([o_ref[:, c] for c in cols], axis=0)
                dog = jnp.concatenate([do_ref[:, c] for c in cols], axis=0)
                delta = jnp.sum(og.astype(F32) * dog.astype(F32), axis=-1, keepdims=True)
                inv = 1.0 / den
                p_c, p_p = pc * inv, pp * inv
                dsc = p_c * (lax.dot_general(dog, vc, nt, preferred_element_type=F32) - delta)
                dsp = p_p * (lax.dot_general(dog, vp, nt, preferred_element_type=F32) - delta)
                dsk = -(ps * inv) * delta
                for i in range(GROUP_A):
                    tot = jnp.sum(jnp.where(hidx == i, dsk, 0.0), axis=0, keepdims=True)
                    dsink_row = dsink_row + jnp.where(lane == GROUP_A * g + i, tot, 0.0)
                dscb, dspb = (dsc * SCALE).astype(BF16), (dsp * SCALE).astype(BF16)
                dqg = (jnp.dot(dscb, kc, preferred_element_type=F32)
                       + jnp.dot(dspb, kp, preferred_element_type=F32))
                for i in range(GROUP_A):
                    dq_ref[:, cols[i]] = dqg[i * BLK:(i + 1) * BLK].astype(BF16)
                gs = slice(g * HD, (g + 1) * HD)
                dk_ref[:, gs] = (ck_ref[:, gs] + lax.dot_general(dspb, qg, tn, preferred_element_type=F32)).astype(BF16)
                dv_ref[:, gs] = (cv_ref[:, gs] + lax.dot_general(p_p.astype(BF16), dog, tn, preferred_element_type=F32)).astype(BF16)
                ck_ref[:, gs] = lax.dot_general(dscb, qg, tn, preferred_element_type=F32)
                cv_ref[:, gs] = lax.dot_general(p_c.astype(BF16), dog, tn, preferred_element_type=F32)
            ds_ref[...] += jnp.broadcast_to(dsink_row, ds_ref.shape)

    cur = lambda n: jnp.minimum(n, nb - 1)
    prev = lambda n: jnp.maximum(jnp.minimum(n, nb - 1) - 1, 0)
    outp = lambda n: jnp.maximum(n - 1, 0)
    qspec = pl.BlockSpec((BLK, NQ_A * HD), lambda n: (cur(n), 0))
    return _pcall(
        name, body, grid=(nb + 1,),
        in_specs=[pl.BlockSpec(memory_space=pltpu.SMEM), qspec,
                  pl.BlockSpec((BLK, kvw), lambda n: (cur(n), kcol)),
                  pl.BlockSpec((BLK, kvw), lambda n: (prev(n), kcol)),
                  pl.BlockSpec((BLK, kvw), lambda n: (cur(n), vcol)),
                  pl.BlockSpec((BLK, kvw), lambda n: (prev(n), vcol)),
                  qspec, qspec],
        out_specs=[qspec, pl.BlockSpec((BLK, kvw), lambda n: (outp(n), 0)),
                   pl.BlockSpec((BLK, kvw), lambda n: (outp(n), 0)),
                   pl.BlockSpec((8, 128), lambda n: (0, 0))],
        out_shape=[jax.ShapeDtypeStruct((T, NQ_A * HD), BF16), jax.ShapeDtypeStruct((T, kvw), BF16),
                   jax.ShapeDtypeStruct((T, kvw), BF16), jax.ShapeDtypeStruct((8, 128), F32)],
        scratch_shapes=[pltpu.VMEM((BLK, kvw), F32), pltpu.VMEM((BLK, kvw), F32)],
        sem=("arbitrary",), args=(sinks, qkv, qkv, qkv, qkv, qkv, o, do), comm=comm)


def _b_logs(z):
    l1p = jnp.log(1.0 + jnp.exp(-jnp.abs(z)))
    return jnp.minimum(z, 0.0) - l1p, -jnp.maximum(z, 0.0) - l1p


def _split_dot(x, m):
    hi = x.astype(BF16)
    lo = (x - hi.astype(F32)).astype(BF16)
    both = jnp.dot(jnp.concatenate([hi, lo], axis=0), m, preferred_element_type=F32)
    return both[:BLK] + both[BLK:]


B_CUT = 105.0
B_NH = 4
B_HEADS = tuple(slice(h * HD, (h + 1) * HD) for h in range(B_NH))
B_UNROLL = 3
B_WIN = B_UNROLL * BLK
B_GROUPS = -(-(T // BLK) // B_UNROLL)


def _b_window(j0, n):
    first = jnp.maximum(j0 - (B_UNROLL - 1), 0) * BLK
    row = lax.broadcasted_iota(jnp.int32, (BLK, B_WIN), 0)
    col = lax.broadcasted_iota(jnp.int32, (BLK, B_WIN), 1)
    valid = jnp.logical_and(col - row < n * BLK - first, col < (j0 + 1) * BLK - first)
    return pl.ds(pl.multiple_of(first, BLK), B_WIN), valid


def _b_weights(qh, kw, valid, c_t, mgt):
    z = lax.dot_general(qh, kw, (((1,), (1,)), ((), ())), preferred_element_type=F32) * SCALE
    lb, lm = _b_logs(z)
    lm = jnp.where(valid, lm, 0.0)
    a = jnp.where(valid, jnp.exp(lb + c_t + _split_dot(lm, mgt)), 0.0)
    return z, a, c_t + jnp.sum(lm, axis=-1, keepdims=True)


def _b_tri(m_ref, later):
    r = lax.broadcasted_iota(jnp.int32, (B_WIN, B_WIN), 0)
    c = lax.broadcasted_iota(jnp.int32, (B_WIN, B_WIN), 1)
    m_ref[...] = (r > c if later else r < c).astype(BF16)


def _b_live(cs):
    return functools.reduce(jnp.maximum, [jnp.max(c) for c in cs]) > -B_CUT


def attn_b_fwd(name, qkv, comm=None):
    npair = NH_B // B_NH

    def body(q_ref, k_ref, v_ref, o_ref, mgt_ref):
        n = pl.program_id(1)

        @pl.when(jnp.logical_and(pl.program_id(0) == 0, n == 0))
        def _():
            _b_tri(mgt_ref, later=True)

        qs = [q_ref[:, hs] for hs in B_HEADS]

        def group(j0, st):
            rows, valid = _b_window(j0, n)
            mgt = mgt_ref[...]
            out = []
            for h, hs in enumerate(B_HEADS):
                c_t, acc = st[h]
                _, a, c_t = _b_weights(qs[h], k_ref[rows, hs], valid, c_t, mgt)
                out.append((c_t, acc + jnp.dot(a.astype(BF16), v_ref[rows, hs], preferred_element_type=F32)))
            return tuple(out)

        zero = (jnp.zeros((BLK, 1), F32), jnp.zeros((BLK, HD), F32))
        _, st = lax.while_loop(lambda cr: jnp.logical_and(cr[0] >= 0, _b_live([c for c, _ in cr[1]])),
                               lambda cr: (cr[0] - B_UNROLL, group(cr[0], cr[1])), (n, (zero,) * B_NH))
        for h, hs in enumerate(B_HEADS):
            o_ref[:, hs] = st[h][1].astype(BF16)

    outs, got = _pcall(
        name, body, grid=(npair, T // BLK),
        in_specs=[pl.BlockSpec((BLK, B_NH * HD), lambda p, n: (n, p)),
                  pl.BlockSpec((T, B_NH * HD), lambda p, n: (0, npair + p)),
                  pl.BlockSpec((T, B_NH * HD), lambda p, n: (0, 2 * npair + p))],
        out_specs=[pl.BlockSpec((BLK, B_NH * HD), lambda p, n: (n, p))],
        out_shape=[jax.ShapeDtypeStruct((T, NH_B * HD), BF16)],
        scratch_shapes=[pltpu.VMEM((B_WIN, B_WIN), BF16)],
        sem=("arbitrary", "arbitrary"), args=(qkv, qkv, qkv), comm=comm)
    return outs[0], got


def attn_b_bwd(name, qkv, do, comm=None):
    npair = NH_B // B_NH
    nb = T // BLK

    def body(q_ref, do_ref, k_ref, v_ref, dq_ref, dk_ref, dv_ref, a_s, z_s, mgt_ref, mlt_ref):
        n = pl.program_id(1)

        @pl.when(jnp.logical_and(pl.program_id(0) == 0, n == 0))
        def _():
            _b_tri(mgt_ref, later=True)
            _b_tri(mlt_ref, later=False)

        @pl.when(n == 0)
        def _():
            dk_ref[...] = jnp.zeros_like(dk_ref)
            dv_ref[...] = jnp.zeros_like(dv_ref)

        nt = (((1,), (1,)), ((), ()))
        tn = (((0,), (0,)), ((), ()))
        qs = [q_ref[:, hs] for hs in B_HEADS]
        dos = [do_ref[:, hs] for hs in B_HEADS]

        def sweep1(gi, j0, cs):
            rows, valid = _b_window(j0, n)
            mgt = mgt_ref[...]
            out = []
            for h, hs in enumerate(B_HEADS):
                z, a, c_t = _b_weights(qs[h], k_ref[rows, hs], valid, cs[h], mgt)
                a_s[h, gi] = a
                z_s[h, gi] = z
                out.append(c_t)
            return tuple(out)

        _, groups, _ = lax.while_loop(
            lambda cr: jnp.logical_and(cr[0] >= 0, _b_live(cr[2])),
            lambda cr: (cr[0] - B_UNROLL, cr[1] + 1, sweep1(cr[1], cr[0], cr[2])),
            (n, 0, (jnp.zeros((BLK, 1), F32),) * B_NH))

        def sweep2(i, st):
            gi = groups - 1 - i
            rows, valid = _b_window(n - gi * B_UNROLL, n)
            mlt = mlt_ref[...]
            out = []
            for h, hs in enumerate(B_HEADS):
                cg, dq = st[h]
                kw, vw = k_ref[rows, hs], v_ref[rows, hs]
                a, z = a_s[h, gi], z_s[h, gi]
                g = a * lax.dot_general(dos[h], vw, nt, preferred_element_type=F32)
                big_g = cg + _split_dot(g, mlt)
                e = jnp.exp(-jnp.abs(z))
                inv = 1.0 / (1.0 + e)
                sig = jnp.where(z >= 0, inv, e * inv)
                dz = jnp.where(valid, g * (1.0 - sig) - big_g * sig, 0.0)
                dzb = (dz * SCALE).astype(BF16)
                dk_ref[rows, hs] += lax.dot_general(dzb, qs[h], tn, preferred_element_type=F32)
                dv_ref[rows, hs] += lax.dot_general(a.astype(BF16), dos[h], tn, preferred_element_type=F32)
                out.append((cg + jnp.sum(g, axis=-1, keepdims=True),
                            dq + jnp.dot(dzb, kw, preferred_element_type=F32)))
            return tuple(out)

        zero = (jnp.zeros((BLK, 1), F32), jnp.zeros((BLK, HD), F32))
        st = lax.fori_loop(0, groups, sweep2, (zero,) * B_NH)
        for h, hs in enumerate(B_HEADS):
            dq_ref[:, hs] = st[h][1].astype(BF16)

    tile = pl.BlockSpec((BLK, B_NH * HD), lambda p, n: (n, p))
    full = lambda off: pl.BlockSpec((T, B_NH * HD), lambda p, n: (0, off + p))
    return _pcall(
        name, body, grid=(npair, nb),
        in_specs=[tile, tile, full(npair), full(2 * npair)],
        out_specs=[tile, full(0), full(0)],
        out_shape=[jax.ShapeDtypeStruct((T, NH_B * HD), BF16), jax.ShapeDtypeStruct((T, NH_B * HD), F32),
                   jax.ShapeDtypeStruct((T, NH_B * HD), F32)],
        scratch_shapes=[pltpu.VMEM((B_NH, B_GROUPS, BLK, B_WIN), F32)] * 2 + [pltpu.VMEM((B_WIN, B_WIN), BF16)] * 2,
        sem=("arbitrary", "arbitrary"), args=(qkv, do, qkv, qkv), comm=comm)


def _adamw_math(w, g, m, v):
    m = B1 * m + (1.0 - B1) * g
    v = B2 * v + (1.0 - B2) * (g * g)
    m_hat = m / (1.0 - B1 ** STEP)
    v_hat = v / (1.0 - B2 ** STEP)
    delta = -LR * (m_hat / (jnp.sqrt(v_hat) + EPS) + WD * w)
    return delta, m, v


def adamw(name, parts, w, m, v, rows):
    r, c = w.shape

    def body(p_ref, w_ref, m_ref, v_ref, g_ref, d_ref, nm_ref, nv_ref):
        g = p_ref[0].astype(F32)
        for s in range(1, NDEV):
            g = g + p_ref[s].astype(F32)
        delta, nm, nv = _adamw_math(w_ref[...], g, m_ref[...], v_ref[...])
        g_ref[...] = g
        d_ref[...] = delta
        nm_ref[...] = nm
        nv_ref[...] = nv

    blk = pl.BlockSpec((rows, c), lambda i: (i, 0))
    return pl.pallas_call(
        body, grid=(r // rows,),
        in_specs=[pl.BlockSpec((NDEV, rows, c), lambda i: (0, i, 0)), blk, blk, blk],
        out_specs=[blk] * 4, out_shape=[jax.ShapeDtypeStruct((r, c), F32)] * 4,
        compiler_params=_cp(("parallel",)), name=name)(parts, w, m, v)


def _unshard_cols(g):
    return jnp.transpose(g, (1, 0, 2)).reshape(g.shape[1], -1)


def _shard_cols(w):
    k, n = w.shape
    return jnp.transpose(w.reshape(k, NDEV, n // NDEV), (1, 0, 2))


FWD_CARRY = {
    ("qkv", 0): [(0, 1)], ("attn", 0): [(0, 2)], ("mlp_in", 0): [(0, 3)], ("mlp_out", 0): [(1, 0)],
    ("qkv", 1): [(1, 1)], ("attn", 1): [(1, 2), (1, 3), (2, 0), (2, 1), (2, 2), (2, 3)],
    ("mlp_in", 1): [(3, 0)], ("mlp_out", 1): [(3, 1)],
    ("attn", 2): [(3, 2)], ("mlp_in", 2): [(3, 3)],
}
BWD_CARRY = {
    ("attn_bwd", 3): [(3, 3), (3, 2), (3, 1)],
    ("d_act", 2): [(3, 0)], ("d_h2", 2): [(2, 3)], ("attn_bwd", 2): [(2, 2)], ("d_h1", 2): [(2, 1)],
    ("d_act", 1): [(2, 0)], ("attn_bwd", 1): [(1, 3), (1, 2), (1, 1)],
    ("d_act", 0): [(1, 0)], ("d_h2", 0): [(0, 3)], ("attn_bwd", 0): [(0, 2)], ("d_h1", 0): [(0, 1)],
}


def kernel(x, a_w_qkv, a_w_o, a_sinks, b_w_qkv, b_w_o, norm_mix, norm_mlp, mlp_w_in, mlp_w_out, final_norm, loss_target, m_a_w_qkv, m_a_w_o, m_a_sinks, m_b_w_qkv, m_b_w_o, m_norm_mix, m_norm_mlp, m_mlp_w_in, m_mlp_w_out, m_final_norm, v_a_w_qkv, v_a_w_o, v_a_sinks, v_b_w_qkv, v_b_w_o, v_norm_mix, v_norm_mlp, v_mlp_w_in, v_mlp_w_out, v_final_norm):
    depth = 4
    xs = x[0]

    shards, full, sendbuf, recv = {}, {}, {}, {}
    for i in range(depth):
        j = i // 2
        wq, wo = (a_w_qkv[j], a_w_o[j]) if i % 2 == 0 else (b_w_qkv[j], b_w_o[j])
        for t, w in enumerate((wq, wo, mlp_w_in[i], mlp_w_out[i])):
            shards[(i, t)] = w.astype(BF16)

    def land_weights(keys, slabs):
        for (i, t), g in zip(keys, slabs):
            if t in (1, 3):
                full[(i, t)] = g.reshape(-1, D)
            else:
                full[(i, t)] = g if g.shape[2] % 128 == 0 else _unshard_cols(g)

    def land_grads(keys, slabs):
        recv.update(zip(keys, slabs))

    def carried(table, payload, gather, land, kind, i, fn):
        keys = table.get((kind, i), [])
        out, got = fn(Comm([payload[k] for k in keys], gather) if keys else None)
        land(keys, got)
        return out

    fwd = functools.partial(carried, FWD_CARRY, shards, True, land_weights)
    bwd = functools.partial(carried, BWD_CARRY, sendbuf, False, land_grads)

    relu2 = lambda acc: (acc, jnp.square(jnp.maximum(acc, 0.0)))
    add = lambda acc, res: (acc + res,)
    drelu2 = lambda acc, uu: (acc * (2.0 * jnp.maximum(uu.astype(F32), 0.0)),)

    land_weights([(0, 0)], exchange("gather_first", [shards[(0, 0)]], True))
    saved = []
    for i in range(depth):
        x1 = xs
        h1 = rmsnorm_fwd(f"norm_mix{i}", x1, norm_mix[i][None])
        (qkv,) = fwd("qkv", i, lambda c: mm_nn(f"qkv{i}", h1, full[(i, 0)], out_dtype=BF16, tn=512, comm=c))
        if i % 2 == 0:
            o = fwd("attn", i, lambda c: attn_a_fwd(f"attn_a{i}", qkv, a_sinks[i // 2], comm=c))
        else:
            o = fwd("attn", i, lambda c: attn_b_fwd(f"attn_b{i}", qkv, comm=c))
        (x2,) = mm_nn(f"proj{i}", o, full[(i, 1)], out_dtype=F32, extras=(x1,), epilogue=add)[0]
        h2 = rmsnorm_fwd(f"norm_mlp{i}", x2, norm_mlp[i][None])
        u, act = fwd("mlp_in", i, lambda c: mm_nn(f"mlp_in{i}", h2, full[(i, 2)], out_dtype=BF16,
                                                   epilogue=relu2, n_out=2, comm=c))
        (xs,) = fwd("mlp_out", i, lambda c: mm_nn(f"mlp_out{i}", act, full[(i, 3)], out_dtype=F32,
                                                   extras=(x2,), epilogue=add, comm=c))
        saved.append((x1, h1, qkv, o, x2, h2, u, act))

    dx, dxb, d_final, loss_part = loss_head("loss_head", xs, final_norm[None], loss_target[0])
    loss = lax.psum(loss_part[0, 0], ("x", "y", "c"))

    d_mix, d_mlp, d_sinks = [None] * depth, [None] * depth, [None] * 2
    for i in reversed(range(depth)):
        x1, h1, qkv, o, x2, h2, u, act = saved[i]
        sendbuf[(i, 3)] = mm_tn(f"g_mlp_out{i}", act, dxb)[0].reshape(NDEV, DFF // NDEV, D)
        du = bwd("d_act", i, lambda c: mm_nt(f"d_act{i}", dxb, full[(i, 3)], out_dtype=BF16, extras=(u,),
                                             epilogue=drelu2, comm=c))
        sendbuf[(i, 2)] = mm_tn(f"g_mlp_in{i}", h2, du, col_shards=True)[0]
        dh2 = bwd("d_h2", i, lambda c: mm_nt(f"d_h2{i}", du, full[(i, 2)], out_dtype=F32, comm=c))
        dx, dxb, d_mlp[i] = rmsnorm_bwd(f"norm_mlp_bwd{i}", dh2, x2, norm_mlp[i][None], dx)
        sendbuf[(i, 1)] = mm_tn(f"g_proj{i}", o, dxb)[0].reshape(NDEV, D // NDEV, D)
        do = mm_nt(f"d_o{i}", dxb, full[(i, 1)], out_dtype=BF16)[0]
        if i % 2 == 0:
            dq, dk, dv, ds = bwd("attn_bwd", i, lambda c: attn_a_bwd(f"attn_a_bwd{i}", qkv, a_sinks[i // 2],
                                                                       o, do, comm=c))
            d_sinks[i // 2] = ds[0, :NQ_A]
            dqkv = jnp.concatenate([dq, dk, dv], axis=1)
        else:
            dq, dk, dv = bwd("attn_bwd", i, lambda c: attn_b_bwd(f"attn_b_bwd{i}", qkv, do, comm=c))
            dqkv = jnp.concatenate([dq, dk.astype(BF16), dv.astype(BF16)], axis=1)
        if i % 2 == 0:
            sendbuf[(i, 0)] = _shard_cols(mm_tn(f"g_qkv{i}", h1, dqkv, tn=512)[0])
        else:
            sendbuf[(i, 0)] = mm_tn(f"g_qkv{i}", h1, dqkv, col_shards=True)[0]
        dh1 = bwd("d_h1", i, lambda c: mm_nt(f"d_h1{i}", dqkv, full[(i, 0)], out_dtype=F32, comm=c))
        dx, dxb, d_mix[i] = rmsnorm_bwd(f"norm_mix_bwd{i}", dh1, x1, norm_mix[i][None], dx)

    small = jnp.zeros((16, D), F32)
    small = small.at[0:4].set(jnp.concatenate(d_mix, axis=0)).at[4:8].set(jnp.concatenate(d_mlp, axis=0))
    small = small.at[8].set(d_final[0]).at[9, :2 * NQ_A].set(jnp.concatenate(d_sinks))
    recv[(0, 0)], small_parts = exchange("exchange_last", [sendbuf[(0, 0)], small], [False, True])
    assert len(recv) == 4 * depth and len(full) == 4 * depth
    pack = lambda mix, mlp, fin, snk: (jnp.zeros((16, D), F32).at[0:4].set(mix).at[4:8].set(mlp)
                                       .at[8].set(fin).at[9, :2 * NQ_A].set(snk.reshape(-1)))
    sm = adamw("adamw_small", small_parts, pack(norm_mix, norm_mlp, final_norm, a_sinks),
               pack(m_norm_mix, m_norm_mlp, m_final_norm, m_a_sinks),
               pack(v_norm_mix, v_norm_mlp, v_final_norm, v_a_sinks), rows=16)
    unpack = lambda a: (a[0:4], a[4:8], a[8], a[9, :2 * NQ_A].reshape(2, NQ_A))

    def layer_update(i, t, w, m, v, rows):
        return adamw(f"adamw{i}_{t}", recv[(i, t)], w, m, v, rows)

    res = {}
    for i in range(depth):
        j = i // 2
        if i % 2 == 0:
            res[("qkv", i)] = layer_update(i, 0, a_w_qkv[j], m_a_w_qkv[j], v_a_w_qkv[j], 256)
            res[("o", i)] = layer_update(i, 1, a_w_o[j], m_a_w_o[j], v_a_w_o[j], 128)
        else:
            res[("qkv", i)] = layer_update(i, 0, b_w_qkv[j], m_b_w_qkv[j], v_b_w_qkv[j], 256)
            res[("o", i)] = layer_update(i, 1, b_w_o[j], m_b_w_o[j], v_b_w_o[j], 128)
        res[("in", i)] = layer_update(i, 2, mlp_w_in[i], m_mlp_w_in[i], v_mlp_w_in[i], 256)
        res[("out", i)] = layer_update(i, 3, mlp_w_out[i], m_mlp_w_out[i], v_mlp_w_out[i], 128)

    outs = []
    for q in range(4):
        mix, mlp, fin, snk = unpack(sm[q])
        outs.append([
            jnp.stack([res[("qkv", 0)][q], res[("qkv", 2)][q]]),
            jnp.stack([res[("o", 0)][q], res[("o", 2)][q]]),
            snk,
            jnp.stack([res[("qkv", 1)][q], res[("qkv", 3)][q]]),
            jnp.stack([res[("o", 1)][q], res[("o", 3)][q]]),
            mix, mlp,
            jnp.stack([res[("in", i)][q] for i in range(depth)]),
            jnp.stack([res[("out", i)][q] for i in range(depth)]),
            fin,
        ])
    return (loss, dx[None], *outs[0], *outs[1], *outs[2], *outs[3])
```

```python
import functools
import math

import jax
import jax.numpy as jnp
from jax import lax
from jax.experimental import pallas as pl
from jax.experimental.pallas import tpu as pltpu

F32, BF16 = jnp.float32, jnp.bfloat16
T, D, DFF = 4096, 2048, 8192
HD, BLK = 64, 128
NQ_A, NKV_A, GROUP_A = 32, 4, 8
NH_B = 32
NDEV = 8
RMS_EPS = 1e-5
SCALE = 0.125
NEG = -1e30
VMEM_LIMIT = 56 * 1024 * 1024
LR, B1, B2, EPS, WD, STEP = 0.001, 0.9, 0.999, 1e-08, 0.01, 10
MESH = pl.DeviceIdType.MESH


def _cp(sem, vmem=VMEM_LIMIT):
    return pltpu.CompilerParams(dimension_semantics=sem, vmem_limit_bytes=vmem)


class Comm:
    def __init__(self, ins, gather):
        self.ins, self.nt = list(ins), len(ins)
        self.gather = list(gather) if isinstance(gather, (list, tuple)) else [gather] * self.nt
        self.out_shape = [jax.ShapeDtypeStruct((NDEV,) + (a.shape if g else a.shape[1:]), a.dtype)
                          for a, g in zip(self.ins, self.gather)]
        self.scratch = [pltpu.SemaphoreType.DMA((7 * self.nt,)), pltpu.SemaphoreType.DMA((7 * self.nt,)),
                        pltpu.SemaphoreType.DMA((self.nt,))]

    def _copies(self, src, dst, sems, *kinds):
        send_sems, recv_sems, local_sems = sems
        x, y, c = lax.axis_index("x"), lax.axis_index("y"), lax.axis_index("c")
        me, sib = (x, y, c), (x, y, 1 - c)
        chips = [(1 - x, y), (x, 1 - y), (1 - x, 1 - y)]
        p = {kind: [] for kind in kinds}
        for t in range(self.nt):
            def slot(d, t=t):
                return dst[t].at[4 * d[0] + 2 * d[1] + d[2]]

            def add(kind, k, a, b, to, t=t):
                if kind in p:
                    p[kind].append(pltpu.make_async_remote_copy(
                        src_ref=a, dst_ref=b, send_sem=send_sems.at[t * 7 + k],
                        recv_sem=recv_sems.at[t * 7 + k], device_id=to, device_id_type=MESH))

            own = src[t] if self.gather[t] else src[t].at[4 * x + 2 * y + c]
            if "local" in p:
                p["local"].append(pltpu.make_async_copy(own, slot(me), local_sems.at[t]))
            if self.gather[t]:
                add("first", 0, own, slot(me), sib)
                add("first_in", 0, own, slot(sib), sib)
                for j, chip in enumerate(chips):
                    there = (*chip, c)
                    add("first", 1 + j, own, slot(me), there)
                    add("chip_in", 1 + j, own, slot(there), there)
                    add("relay", 4 + j, slot(there), slot(there), sib)
                    add("relay_in", 4 + j, own, slot((*chip, 1 - c)), sib)
            else:
                for k in range(NDEV - 1):
                    bits = k + 1
                    peer = (1 - x if bits & 4 else x, 1 - y if bits & 2 else y, 1 - c if bits & 1 else c)
                    out = src[t].at[4 * peer[0] + 2 * peer[1] + peer[2]]
                    add("first", k, out, slot(me), peer)
                    add("first_in", k, out, slot(peer), peer)
        return [p[kind] for kind in kinds]

    def start(self, src, dst, sems):
        local, first = self._copies(src, dst, sems, "local", "first")
        for cp in local + first:
            cp.start()

    def relay(self, src, dst, sems):
        chip_in, relay = self._copies(src, dst, sems, "chip_in", "relay")
        for arrived, onward in zip(chip_in, relay):
            arrived.wait_recv()
            onward.start()

    def finish(self, src, dst, sems):
        first_in, relay_in, first, relay, local = self._copies(
            src, dst, sems, "first_in", "relay_in", "first", "relay", "local")
        for cp in first_in + relay_in:
            cp.wait_recv()
        for cp in first + relay:
            cp.wait_send()
        for cp in local:
            cp.wait()


def _pcall(name, body, *, grid, in_specs, out_specs, out_shape, scratch_shapes=(), sem, args, comm=None):
    in_specs, out_specs, out_shape = list(in_specs), list(out_specs), list(out_shape)
    scratch_shapes = list(scratch_shapes)
    if comm is None:
        outs = pl.pallas_call(body, grid=grid, in_specs=in_specs, out_specs=out_specs, out_shape=out_shape,
                              scratch_shapes=scratch_shapes, compiler_params=_cp(sem), name=name)(*args)
        return list(outs), []
    ni, no, ns, nt = len(in_specs), len(out_specs), len(scratch_shapes), comm.nt
    steps = math.prod(grid)
    hbm = pl.BlockSpec(memory_space=pltpu.HBM)

    def carrier(*refs):
        ins, cin = refs[:ni], refs[ni:ni + nt]
        outs, cout = refs[ni + nt:ni + nt + no], refs[ni + nt + no:ni + 2 * nt + no]
        scr, sems = refs[ni + 2 * nt + no:ni + 2 * nt + no + ns], refs[ni + 2 * nt + no + ns:]
        step = functools.reduce(lambda acc, d: acc * grid[d] + pl.program_id(d), range(len(grid)), 0)

        @pl.when(step == 0)
        def _():
            comm.start(cin, cout, sems)

        body(*ins, *outs, *scr)

        @pl.when(step == (7 * steps) // 8)
        def _():
            comm.relay(cin, cout, sems)

        @pl.when(step == steps - 1)
        def _():
            comm.finish(cin, cout, sems)

    outs = pl.pallas_call(
        carrier, grid=grid, in_specs=in_specs + [hbm] * nt, out_specs=out_specs + [hbm] * nt,
        out_shape=out_shape + comm.out_shape, scratch_shapes=scratch_shapes + comm.scratch,
        compiler_params=_cp(("arbitrary",) * len(grid)), name=name)(*args, *comm.ins)
    return list(outs[:no]), list(outs[no:])


def exchange(name, ins, gather):
    comm = Comm(ins, gather)
    hbm = pl.BlockSpec(memory_space=pltpu.HBM)

    def body(*refs):
        src, dst, sems = refs[:comm.nt], refs[comm.nt:2 * comm.nt], refs[2 * comm.nt:]
        comm.start(src, dst, sems)
        comm.relay(src, dst, sems)
        comm.finish(src, dst, sems)

    return pl.pallas_call(body, in_specs=[hbm] * comm.nt, out_specs=[hbm] * comm.nt, out_shape=comm.out_shape,
                          scratch_shapes=comm.scratch, name=name)(*ins)


def _matmul(name, a, b, extras, *, grid, a_spec, b_spec, extra_specs, out_shapes, out_specs,
            contract, epilogue, acc_shape, comm=None):
    nk = grid[2]
    ne, no = len(extras), len(out_shapes)

    def body(a_ref, b_ref, *rest):
        ex, outs, acc = rest[:ne], rest[ne:ne + no], rest[ne + no]
        kk = pl.program_id(2)

        @pl.when(kk == 0)
        def _():
            acc[...] = jnp.zeros_like(acc)

        acc[...] += lax.dot_general(a_ref[...].astype(BF16), b_ref[...].astype(BF16),
                                    (contract, ((), ())), preferred_element_type=F32)

        @pl.when(kk == nk - 1)
        def _():
            res = epilogue(acc[...], *[r[...] for r in ex])
            for r, o in zip(outs, res):
                r[...] = o.astype(r.dtype)

    return _pcall(name, body, grid=grid, in_specs=[a_spec, b_spec, *extra_specs], out_specs=out_specs,
                  out_shape=out_shapes, scratch_shapes=[pltpu.VMEM(acc_shape, F32)],
                  sem=("parallel", "parallel", "arbitrary"), args=(a, b, *extras), comm=comm)


def _ident(acc):
    return (acc,)


def _fit(tile, dim):
    return max(t for t in range(128, min(tile, dim) + 1, 128) if dim % t == 0)


def mm_nn(name, a, w, *, out_dtype, tm=1024, tn=1024, tk=2048, extras=(), epilogue=_ident, n_out=1,
          out_dtypes=None, comm=None):
    m, k = a.shape
    tm, tk = _fit(tm, m), _fit(tk, k)
    if w.ndim == 3:
        tn = w.shape[2]
        n = NDEV * tn
        b_spec = pl.BlockSpec((None, tk, tn), lambda i, j, kk: (j, kk, 0))
    else:
        n = w.shape[1]
        tn = _fit(tn, n)
        b_spec = pl.BlockSpec((tk, tn), lambda i, j, kk: (kk, j))
    out_dtypes = out_dtypes or (out_dtype,) * n_out
    mn = pl.BlockSpec((tm, tn), lambda i, j, kk: (i, j))
    return _matmul(name, a, w, extras, grid=(m // tm, n // tn, k // tk),
                   a_spec=pl.BlockSpec((tm, tk), lambda i, j, kk: (i, kk)), b_spec=b_spec,
                   extra_specs=[mn] * len(extras),
                   out_shapes=[jax.ShapeDtypeStruct((m, n), dt) for dt in out_dtypes],
                   out_specs=[mn] * len(out_dtypes), contract=((1,), (0,)), epilogue=epilogue,
                   acc_shape=(tm, tn), comm=comm)


def mm_nt(name, a, w, *, out_dtype, tm=1024, tn=1024, tk=2048, extras=(), epilogue=_ident, comm=None):
    m, n = a.shape
    k = w.shape[-2]
    tm, to = _fit(tm, m), _fit(tn, k)
    if w.ndim == 3:
        tc = w.shape[2]
        b_spec = pl.BlockSpec((None, to, tc), lambda i, j, kk: (kk, j, 0))
    else:
        tc = _fit(tk, n)
        b_spec = pl.BlockSpec((to, tc), lambda i, j, kk: (j, kk))
    mo = pl.BlockSpec((tm, to), lambda i, j, kk: (i, j))
    outs, got = _matmul(name, a, w, extras, grid=(m // tm, k // to, n // tc),
                        a_spec=pl.BlockSpec((tm, tc), lambda i, j, kk: (i, kk)), b_spec=b_spec,
                        extra_specs=[mo] * len(extras),
                        out_shapes=[jax.ShapeDtypeStruct((m, k), out_dtype)],
                        out_specs=[mo], contract=((1,), (1,)), epilogue=epilogue, acc_shape=(tm, to),
                        comm=comm)
    return outs[0], got


def mm_tn(name, a, b, *, out_dtype=BF16, tm=1024, tn=1024, tk=2048, col_shards=False, comm=None):
    t, k = a.shape
    n = b.shape[1]
    tm, tk = _fit(tm, k), _fit(tk, t)
    if col_shards:
        tn = n // NDEV
        shape, spec = (NDEV, k, tn), pl.BlockSpec((None, tm, tn), lambda i, j, kk: (j, i, 0))
    else:
        tn = _fit(tn, n)
        shape, spec = (k, n), pl.BlockSpec((tm, tn), lambda i, j, kk: (i, j))
    outs, got = _matmul(name, a, b, (), grid=(k // tm, n // tn, t // tk),
                        a_spec=pl.BlockSpec((tk, tm), lambda i, j, kk: (kk, i)),
                        b_spec=pl.BlockSpec((tk, tn), lambda i, j, kk: (kk, j)),
                        extra_specs=[], out_shapes=[jax.ShapeDtypeStruct(shape, out_dtype)],
                        out_specs=[spec],
                        contract=((0,), (0,)), epilogue=_ident, acc_shape=(tm, tn), comm=comm)
    return outs[0], got


ROWS = 256


def rmsnorm_fwd(name, x, gain):
    def body(x_ref, g_ref, h_ref):
        xf = x_ref[...]
        rstd = lax.rsqrt(jnp.mean(xf * xf, axis=-1, keepdims=True) + RMS_EPS)
        h_ref[...] = (xf * rstd * g_ref[...]).astype(BF16)

    row = pl.BlockSpec((ROWS, D), lambda i: (i, 0))
    return pl.pallas_call(body, grid=(T // ROWS,), in_specs=[row, pl.BlockSpec((1, D), lambda i: (0, 0))],
                          out_specs=row, out_shape=jax.ShapeDtypeStruct((T, D), BF16),
                          compiler_params=_cp(("parallel",)), name=name)(x, gain)


def _rms_bwd_math(dh, xf, gain):
    rstd = lax.rsqrt(jnp.mean(xf * xf, axis=-1, keepdims=True) + RMS_EPS)
    xhat = xf * rstd
    dgain = jnp.sum(dh * xhat, axis=0, keepdims=True)
    dxhat = dh * gain
    dx = rstd * (dxhat - xhat * jnp.mean(dxhat * xhat, axis=-1, keepdims=True))
    return dx, dgain


def rmsnorm_bwd(name, dh, x, gain, dres):
    def body(dh_ref, x_ref, g_ref, dres_ref, dx_ref, dxb_ref, dg_ref):
        dx, dgain = _rms_bwd_math(dh_ref[...], x_ref[...], g_ref[...])
        dx = dx + dres_ref[...]
        dx_ref[...] = dx
        dxb_ref[...] = dx.astype(BF16)

        @pl.when(pl.program_id(0) == 0)
        def _():
            dg_ref[...] = jnp.zeros_like(dg_ref)

        dg_ref[...] += dgain

    row = pl.BlockSpec((ROWS, D), lambda i: (i, 0))
    vec = pl.BlockSpec((1, D), lambda i: (0, 0))
    return pl.pallas_call(
        body, grid=(T // ROWS,), in_specs=[row, row, vec, row], out_specs=[row, row, vec],
        out_shape=[jax.ShapeDtypeStruct((T, D), F32), jax.ShapeDtypeStruct((T, D), BF16),
                   jax.ShapeDtypeStruct((1, D), F32)],
        compiler_params=_cp(("arbitrary",)), name=name)(dh, x, gain, dres)


def loss_head(name, x, gain, target):
    def body(x_ref, g_ref, t_ref, dx_ref, dxb_ref, dg_ref, loss_ref):
        xf, gain = x_ref[...], g_ref[...]
        rstd = lax.rsqrt(jnp.mean(xf * xf, axis=-1, keepdims=True) + RMS_EPS)
        err = xf * rstd * gain - t_ref[...]
        part = 0.5 * jnp.sum(jnp.mean(err * err, axis=-1, keepdims=True), axis=0, keepdims=True)
        dx, dgain = _rms_bwd_math(err * (1.0 / D), xf, gain)
        dx_ref[...] = dx
        dxb_ref[...] = dx.astype(BF16)

        @pl.when(pl.program_id(0) == 0)
        def _():
            dg_ref[...] = jnp.zeros_like(dg_ref)
            loss_ref[...] = jnp.zeros_like(loss_ref)

        dg_ref[...] += dgain
        loss_ref[...] += jnp.broadcast_to(part, loss_ref.shape)

    row = pl.BlockSpec((ROWS, D), lambda i: (i, 0))
    vec = pl.BlockSpec((1, D), lambda i: (0, 0))
    return pl.pallas_call(
        body, grid=(T // ROWS,), in_specs=[row, vec, row],
        out_specs=[row, row, vec, pl.BlockSpec((1, 128), lambda i: (0, 0))],
        out_shape=[jax.ShapeDtypeStruct((T, D), F32), jax.ShapeDtypeStruct((T, D), BF16),
                   jax.ShapeDtypeStruct((1, D), F32), jax.ShapeDtypeStruct((1, 128), F32)],
        compiler_params=_cp(("arbitrary",)), name=name)(x, gain, target)


def _a_common(n, g, s_ref, q_ref, kc_ref, kp_ref):
    row = lax.broadcasted_iota(jnp.int32, (GROUP_A * BLK, BLK), 0)
    col = lax.broadcasted_iota(jnp.int32, (GROUP_A * BLK, BLK), 1)
    qi = row & (BLK - 1)
    dist_c = (qi - col).astype(F32)
    valid_c = qi >= col
    valid_p = jnp.logical_and(col > qi, n > 0)
    hidx = lax.broadcasted_iota(jnp.int32, (GROUP_A * BLK, 1), 0) >> 7
    slope = jnp.exp((-math.log(2.0) / 4.0) * (hidx + (GROUP_A * g + 1)).astype(F32))
    sink = jnp.zeros((GROUP_A * BLK, 1), F32)
    for i in range(GROUP_A):
        sink = jnp.where(hidx == i, s_ref[GROUP_A * g + i], sink)
    qg = jnp.concatenate([q_ref[:, (GROUP_A * g + i) * HD:(GROUP_A * g + i + 1) * HD]
                          for i in range(GROUP_A)], axis=0)
    kc = kc_ref[:, g * HD:(g + 1) * HD]
    kp = kp_ref[:, g * HD:(g + 1) * HD]
    nt = (((1,), (1,)), ((), ()))
    sc = lax.dot_general(qg, kc, nt, preferred_element_type=F32) * SCALE - slope * dist_c
    sp = lax.dot_general(qg, kp, nt, preferred_element_type=F32) * SCALE - slope * (dist_c + float(BLK))
    sc = jnp.where(valid_c, sc, NEG)
    sp = jnp.where(valid_p, sp, NEG)
    m = jnp.maximum(jnp.maximum(jnp.max(sc, axis=-1, keepdims=True), jnp.max(sp, axis=-1, keepdims=True)), sink)
    pc = jnp.exp(sc - m)
    pp = jnp.exp(sp - m)
    ps = jnp.exp(sink - m)
    den = jnp.sum(pc, axis=-1, keepdims=True) + jnp.sum(pp, axis=-1, keepdims=True) + ps
    return qg, kc, kp, pc, pp, ps, den, hidx


def attn_a_fwd(name, qkv, sinks, comm=None):
    kcol, vcol = NQ_A * HD // (NKV_A * HD), NQ_A * HD // (NKV_A * HD) + 1

    def body(s_ref, q_ref, kc_ref, kp_ref, vc_ref, vp_ref, o_ref):
        n = pl.program_id(0)
        for g in range(NKV_A):
            qg, kc, kp, pc, pp, ps, den, _ = _a_common(n, g, s_ref, q_ref, kc_ref, kp_ref)
            vc = vc_ref[:, g * HD:(g + 1) * HD]
            vp = vp_ref[:, g * HD:(g + 1) * HD]
            og = (jnp.dot(pc.astype(BF16), vc, preferred_element_type=F32)
                  + jnp.dot(pp.astype(BF16), vp, preferred_element_type=F32)) / den
            for i in range(GROUP_A):
                h = GROUP_A * g + i
                o_ref[:, h * HD:(h + 1) * HD] = og[i * BLK:(i + 1) * BLK].astype(BF16)

    kvw = NKV_A * HD
    prev = lambda n: jnp.maximum(n - 1, 0)
    outs, got = _pcall(
        name, body, grid=(T // BLK,),
        in_specs=[pl.BlockSpec(memory_space=pltpu.SMEM),
                  pl.BlockSpec((BLK, NQ_A * HD), lambda n: (n, 0)),
                  pl.BlockSpec((BLK, kvw), lambda n: (n, kcol)),
                  pl.BlockSpec((BLK, kvw), lambda n: (prev(n), kcol)),
                  pl.BlockSpec((BLK, kvw), lambda n: (n, vcol)),
                  pl.BlockSpec((BLK, kvw), lambda n: (prev(n), vcol))],
        out_specs=[pl.BlockSpec((BLK, NQ_A * HD), lambda n: (n, 0))],
        out_shape=[jax.ShapeDtypeStruct((T, NQ_A * HD), BF16)],
        sem=("parallel",), args=(sinks, qkv, qkv, qkv, qkv, qkv), comm=comm)
    return outs[0], got


def attn_a_bwd(name, qkv, sinks, o, do, comm=None):
    kcol, vcol = NQ_A * HD // (NKV_A * HD), NQ_A * HD // (NKV_A * HD) + 1
    nb = T // BLK
    kvw = NKV_A * HD

    def body(s_ref, q_ref, kc_ref, kp_ref, vc_ref, vp_ref, o_ref, do_ref,
             dq_ref, dk_ref, dv_ref, ds_ref, ck_ref, cv_ref):
        n = pl.program_id(0)

        @pl.when(n == 0)
        def _():
            ds_ref[...] = jnp.zeros_like(ds_ref)
            ck_ref[...] = jnp.zeros_like(ck_ref)
            cv_ref[...] = jnp.zeros_like(cv_ref)

        @pl.when(n == nb)
        def _():
            dk_ref[...] = ck_ref[...].astype(BF16)
            dv_ref[...] = cv_ref[...].astype(BF16)

        @pl.when(n < nb)
        def _():
            lane = lax.broadcasted_iota(jnp.int32, (1, 128), 1)
            dsink_row = jnp.zeros((1, 128), F32)
            tn = (((0,), (0,)), ((), ()))
            nt = (((1,), (1,)), ((), ()))
            for g in range(NKV_A):
                qg, kc, kp, pc, pp, ps, den, hidx = _a_common(n, g, s_ref, q_ref, kc_ref, kp_ref)
                vc = vc_ref[:, g * HD:(g + 1) * HD]
                vp = vp_ref[:, g * HD:(g + 1) * HD]
                cols = [slice((GROUP_A * g + i) * HD, (GROUP_A * g + i + 1) * HD) for i in range(GROUP_A)]
                og = jnp.concatenate([o_ref[:, c] for c in cols], axis=0)
                dog = jnp.concatenate([do_ref[:, c] for c in cols], axis=0)
                delta = jnp.sum(og.astype(F32) * dog.astype(F32), axis=-1, keepdims=True)
                inv = 1.0 / den
                p_c, p_p = pc * inv, pp * inv
                dsc = p_c * (lax.dot_general(dog, vc, nt, preferred_element_type=F32) - delta)
                dsp = p_p * (lax.dot_general(dog, vp, nt, preferred_element_type=F32) - delta)
                dsk = -(ps * inv) * delta
                for i in range(GROUP_A):
                    tot = jnp.sum(jnp.where(hidx == i, dsk, 0.0), axis=0, keepdims=True)
                    dsink_row = dsink_row + jnp.where(lane == GROUP_A * g + i, tot, 0.0)
                dscb, dspb = (dsc * SCALE).astype(BF16), (dsp * SCALE).astype(BF16)
                dqg = (jnp.dot(dscb, kc, preferred_element_type=F32)
                       + jnp.dot(dspb, kp, preferred_element_type=F32))
                for i in range(GROUP_A):
                    dq_ref[:, cols[i]] = dqg[i * BLK:(i + 1) * BLK].astype(BF16)
                gs = slice(g * HD, (g + 1) * HD)
                dk_ref[:, gs] = (ck_ref[:, gs] + lax.dot_general(dspb, qg, tn, preferred_element_type=F32)).astype(BF16)
                dv_ref[:, gs] = (cv_ref[:, gs] + lax.dot_general(p_p.astype(BF16), dog, tn, preferred_element_type=F32)).astype(BF16)
                ck_ref[:, gs] = lax.dot_general(dscb, qg, tn, preferred_element_type=F32)
                cv_ref[:, gs] = lax.dot_general(p_c.astype(BF16), dog, tn, preferred_element_type=F32)
            ds_ref[...] += jnp.broadcast_to(dsink_row, ds_ref.shape)

    cur = lambda n: jnp.minimum(n, nb - 1)
    prev = lambda n: jnp.maximum(jnp.minimum(n, nb - 1) - 1, 0)
    outp = lambda n: jnp.maximum(n - 1, 0)
    qspec = pl.BlockSpec((BLK, NQ_A * HD), lambda n: (cur(n), 0))
    return _pcall(
        name, body, grid=(nb + 1,),
        in_specs=[pl.BlockSpec(memory_space=pltpu.SMEM), qspec,
                  pl.BlockSpec((BLK, kvw), lambda n: (cur(n), kcol)),
                  pl.BlockSpec((BLK, kvw), lambda n: (prev(n), kcol)),
                  pl.BlockSpec((BLK, kvw), lambda n: (cur(n), vcol)),
                  pl.BlockSpec((BLK, kvw), lambda n: (prev(n), vcol)),
                  qspec, qspec],
        out_specs=[qspec, pl.BlockSpec((BLK, kvw), lambda n: (outp(n), 0)),
                   pl.BlockSpec((BLK, kvw), lambda n: (outp(n), 0)),
                   pl.BlockSpec((8, 128), lambda n: (0, 0))],
        out_shape=[jax.ShapeDtypeStruct((T, NQ_A * HD), BF16), jax.ShapeDtypeStruct((T, kvw), BF16),
                   jax.ShapeDtypeStruct((T, kvw), BF16), jax.ShapeDtypeStruct((8, 128), F32)],
        scratch_shapes=[pltpu.VMEM((BLK, kvw), F32), pltpu.VMEM((BLK, kvw), F32)],
        sem=("arbitrary",), args=(sinks, qkv, qkv, qkv, qkv, qkv, o, do), comm=comm)


def _b_logs(z):
    l1p = jnp.log(1.0 + jnp.exp(-jnp.abs(z)))
    return jnp.minimum(z, 0.0) - l1p, -jnp.maximum(z, 0.0) - l1p


def _cumsum_excl(x, later):
    r = lax.broadcasted_iota(jnp.int32, (BLK, BLK), 0)
    c = lax.broadcasted_iota(jnp.int32, (BLK, BLK), 1)
    tri = (r > c if later else r < c).astype(BF16)
    hi = x.astype(BF16)
    lo = (x - hi.astype(F32)).astype(BF16)
    blocks = [slice(b * BLK, (b + 1) * BLK) for b in range(B_UNROLL)]
    stacked = jnp.concatenate([hi[:, b] for b in blocks] + [lo[:, b] for b in blocks], axis=0)
    inside = jnp.dot(stacked, tri, preferred_element_type=F32)
    sums = [jnp.sum(x[:, b], axis=-1, keepdims=True) for b in blocks]
    out = []
    for i in range(B_UNROLL):
        other = sums[i + 1:] if later else sums[:i]
        part = inside[i * BLK:(i + 1) * BLK] + inside[(B_UNROLL + i) * BLK:(B_UNROLL + i + 1) * BLK]
        out.append(functools.reduce(jnp.add, other, part))
    return jnp.concatenate(out, axis=1), functools.reduce(jnp.add, sums)


B_CUT = 105.0
B_NH = 4
B_HEADS = tuple(slice(h * HD, (h + 1) * HD) for h in range(B_NH))
B_UNROLL = 3
B_WIN = B_UNROLL * BLK
B_GROUPS = -(-(T // BLK) // B_UNROLL)


def _b_window(j0, n):
    first = jnp.maximum(j0 - (B_UNROLL - 1), 0) * BLK
    row = lax.broadcasted_iota(jnp.int32, (BLK, B_WIN), 0)
    col = lax.broadcasted_iota(jnp.int32, (BLK, B_WIN), 1)
    valid = jnp.logical_and(col - row < n * BLK - first, col < (j0 + 1) * BLK - first)
    return pl.ds(pl.multiple_of(first, BLK), B_WIN), valid


def _b_weights(qh, kw, valid, c_t):
    z = lax.dot_general(qh, kw, (((1,), (1,)), ((), ())), preferred_element_type=F32) * SCALE
    lb, lm = _b_logs(z)
    lm = jnp.where(valid, lm, 0.0)
    after, total = _cumsum_excl(lm, later=True)
    a = jnp.where(valid, jnp.exp(lb + c_t + after), 0.0)
    return z, a, c_t + total


def _b_live(cs):
    return functools.reduce(jnp.maximum, [jnp.max(c) for c in cs]) > -B_CUT


def attn_b_fwd(name, qkv, comm=None):
    npair = NH_B // B_NH

    def body(q_ref, k_ref, v_ref, o_ref):
        n = pl.program_id(1)
        qs = [q_ref[:, hs] for hs in B_HEADS]

        def group(j0, st):
            rows, valid = _b_window(j0, n)
            out = []
            for h, hs in enumerate(B_HEADS):
                c_t, acc = st[h]
                _, a, c_t = _b_weights(qs[h], k_ref[rows, hs], valid, c_t)
                out.append((c_t, acc + jnp.dot(a.astype(BF16), v_ref[rows, hs], preferred_element_type=F32)))
            return tuple(out)

        zero = (jnp.zeros((BLK, 1), F32), jnp.zeros((BLK, HD), F32))
        _, st = lax.while_loop(lambda cr: jnp.logical_and(cr[0] >= 0, _b_live([c for c, _ in cr[1]])),
                               lambda cr: (cr[0] - B_UNROLL, group(cr[0], cr[1])), (n, (zero,) * B_NH))
        for h, hs in enumerate(B_HEADS):
            o_ref[:, hs] = st[h][1].astype(BF16)

    outs, got = _pcall(
        name, body, grid=(npair, T // BLK),
        in_specs=[pl.BlockSpec((BLK, B_NH * HD), lambda p, n: (n, p)),
                  pl.BlockSpec((T, B_NH * HD), lambda p, n: (0, npair + p)),
                  pl.BlockSpec((T, B_NH * HD), lambda p, n: (0, 2 * npair + p))],
        out_specs=[pl.BlockSpec((BLK, B_NH * HD), lambda p, n: (n, p))],
        out_shape=[jax.ShapeDtypeStruct((T, NH_B * HD), BF16)],
        sem=("parallel", "arbitrary"), args=(qkv, qkv, qkv), comm=comm)
    return outs[0], got


def attn_b_bwd(name, qkv, do, comm=None):
    npair = NH_B // B_NH
    nb = T // BLK

    def body(q_ref, do_ref, k_ref, v_ref, dq_ref, dk_ref, dv_ref, a_s, z_s):
        n = pl.program_id(1)

        @pl.when(n == 0)
        def _():
            dk_ref[...] = jnp.zeros_like(dk_ref)
            dv_ref[...] = jnp.zeros_like(dv_ref)

        nt = (((1,), (1,)), ((), ()))
        tn = (((0,), (0,)), ((), ()))
        qs = [q_ref[:, hs] for hs in B_HEADS]
        dos = [do_ref[:, hs] for hs in B_HEADS]

        def sweep1(gi, j0, cs):
            rows, valid = _b_window(j0, n)
            out = []
            for h, hs in enumerate(B_HEADS):
                z, a, c_t = _b_weights(qs[h], k_ref[rows, hs], valid, cs[h])
                a_s[h, gi] = a
                z_s[h, gi] = z
                out.append(c_t)
            return tuple(out)

        _, groups, _ = lax.while_loop(
            lambda cr: jnp.logical_and(cr[0] >= 0, _b_live(cr[2])),
            lambda cr: (cr[0] - B_UNROLL, cr[1] + 1, sweep1(cr[1], cr[0], cr[2])),
            (n, 0, (jnp.zeros((BLK, 1), F32),) * B_NH))

        def sweep2(i, st):
            gi = groups - 1 - i
            rows, valid = _b_window(n - gi * B_UNROLL, n)
            out = []
            for h, hs in enumerate(B_HEADS):
                cg, dq = st[h]
                kw, vw = k_ref[rows, hs], v_ref[rows, hs]
                a, z = a_s[h, gi], z_s[h, gi]
                g = a * lax.dot_general(dos[h], vw, nt, preferred_element_type=F32)
                before, total = _cumsum_excl(g, later=False)
                big_g = cg + before
                e = jnp.exp(-jnp.abs(z))
                inv = 1.0 / (1.0 + e)
                sig = jnp.where(z >= 0, inv, e * inv)
                dz = jnp.where(valid, g * (1.0 - sig) - big_g * sig, 0.0)
                dzb = (dz * SCALE).astype(BF16)
                dk_ref[rows, hs] += lax.dot_general(dzb, qs[h], tn, preferred_element_type=F32)
                dv_ref[rows, hs] += lax.dot_general(a.astype(BF16), dos[h], tn, preferred_element_type=F32)
                out.append((cg + total, dq + jnp.dot(dzb, kw, preferred_element_type=F32)))
            return tuple(out)

        zero = (jnp.zeros((BLK, 1), F32), jnp.zeros((BLK, HD), F32))
        st = lax.fori_loop(0, groups, sweep2, (zero,) * B_NH)
        for h, hs in enumerate(B_HEADS):
            dq_ref[:, hs] = st[h][1].astype(BF16)

    tile = pl.BlockSpec((BLK, B_NH * HD), lambda p, n: (n, p))
    full = lambda off: pl.BlockSpec((T, B_NH * HD), lambda p, n: (0, off + p))
    return _pcall(
        name, body, grid=(npair, nb),
        in_specs=[tile, tile, full(npair), full(2 * npair)],
        out_specs=[tile, full(0), full(0)],
        out_shape=[jax.ShapeDtypeStruct((T, NH_B * HD), BF16), jax.ShapeDtypeStruct((T, NH_B * HD), F32),
                   jax.ShapeDtypeStruct((T, NH_B * HD), F32)],
        scratch_shapes=[pltpu.VMEM((B_NH, B_GROUPS, BLK, B_WIN), F32)] * 2,
        sem=("parallel", "arbitrary"), args=(qkv, do, qkv, qkv), comm=comm)


def _adamw_math(w, g, m, v):
    m = B1 * m + (1.0 - B1) * g
    v = B2 * v + (1.0 - B2) * (g * g)
    m_hat = m / (1.0 - B1 ** STEP)
    v_hat = v / (1.0 - B2 ** STEP)
    delta = -LR * (m_hat / (jnp.sqrt(v_hat) + EPS) + WD * w)
    return delta, m, v


def adamw(name, parts, w, m, v, rows, layer=0, prev=None):
    nl, r, c = w.shape

    def body(p_ref, w_ref, m_ref, v_ref, *rest):
        g_ref, d_ref, nm_ref, nv_ref = rest[-4:]
        g = p_ref[0].astype(F32)
        for s in range(1, NDEV):
            g = g + p_ref[s].astype(F32)
        delta, nm, nv = _adamw_math(w_ref[...], g, m_ref[...], v_ref[...])
        g_ref[...] = g
        d_ref[...] = delta
        nm_ref[...] = nm
        nv_ref[...] = nv

    blk = pl.BlockSpec((None, rows, c), lambda i: (layer, i, 0))
    carried = [] if prev is None else list(prev)
    return pl.pallas_call(
        body, grid=(r // rows,),
        in_specs=[pl.BlockSpec((NDEV, rows, c), lambda i: (0, i, 0)), blk, blk, blk]
        + [pl.BlockSpec(memory_space=pl.ANY)] * len(carried),
        out_specs=[blk] * 4, out_shape=[jax.ShapeDtypeStruct((nl, r, c), F32)] * 4,
        input_output_aliases={4 + q: q for q in range(len(carried))},
        compiler_params=_cp(("parallel",)), name=name)(parts, w, m, v, *carried)


def _unshard_cols(g):
    return jnp.transpose(g, (1, 0, 2)).reshape(g.shape[1], -1)


def _shard_cols(w):
    k, n = w.shape
    return jnp.transpose(w.reshape(k, NDEV, n // NDEV), (1, 0, 2))


FWD_CARRY = {
    ("qkv", 0): [(0, 1)], ("attn", 0): [(0, 2)], ("mlp_in", 0): [(0, 3)], ("mlp_out", 0): [(1, 0)],
    ("qkv", 1): [(1, 1)], ("attn", 1): [(1, 2), (1, 3), (2, 0), (2, 1), (2, 2), (2, 3)],
    ("mlp_in", 1): [(3, 0)], ("mlp_out", 1): [(3, 1)],
    ("attn", 2): [(3, 2)], ("mlp_in", 2): [(3, 3)],
}
BWD_CARRY = {
    ("attn_bwd", 3): [(3, 3), (3, 2), (3, 1)],
    ("d_act", 2): [(3, 0)], ("attn_bwd", 2): [(2, 3)], ("d_h1", 2): [(2, 1)],
    ("d_act", 1): [(2, 0)], ("attn_bwd", 1): [(2, 2), (1, 3), (1, 2), (1, 1)],
    ("d_act", 0): [(1, 0)], ("d_h2", 0): [(0, 3)], ("attn_bwd", 0): [(0, 2)], ("d_h1", 0): [(0, 1)],
}


def kernel(x, a_w_qkv, a_w_o, a_sinks, b_w_qkv, b_w_o, norm_mix, norm_mlp, mlp_w_in, mlp_w_out, final_norm, loss_target, m_a_w_qkv, m_a_w_o, m_a_sinks, m_b_w_qkv, m_b_w_o, m_norm_mix, m_norm_mlp, m_mlp_w_in, m_mlp_w_out, m_final_norm, v_a_w_qkv, v_a_w_o, v_a_sinks, v_b_w_qkv, v_b_w_o, v_norm_mix, v_norm_mlp, v_mlp_w_in, v_mlp_w_out, v_final_norm):
    depth = 4
    xs = x[0]

    shards, full, sendbuf, recv = {}, {}, {}, {}
    for i in range(depth):
        j = i // 2
        wq, wo = (a_w_qkv[j], a_w_o[j]) if i % 2 == 0 else (b_w_qkv[j], b_w_o[j])
        for t, w in enumerate((wq, wo, mlp_w_in[i], mlp_w_out[i])):
            shards[(i, t)] = w.astype(BF16)

    def land_weights(keys, slabs):
        for (i, t), g in zip(keys, slabs):
            if t in (1, 3):
                full[(i, t)] = g.reshape(-1, D)
            else:
                full[(i, t)] = g if g.shape[2] % 128 == 0 else _unshard_cols(g)

    def land_grads(keys, slabs):
        recv.update(zip(keys, slabs))

    def carried(table, payload, gather, land, kind, i, fn):
        keys = table.get((kind, i), [])
        out, got = fn(Comm([payload[k] for k in keys], gather) if keys else None)
        land(keys, got)
        return out

    fwd = functools.partial(carried, FWD_CARRY, shards, True, land_weights)
    bwd = functools.partial(carried, BWD_CARRY, sendbuf, False, land_grads)

    relu2 = lambda acc: (acc, jnp.square(jnp.maximum(acc, 0.0)))
    add = lambda acc, res: (acc + res,)
    drelu2 = lambda acc, uu: (acc * (2.0 * jnp.maximum(uu.astype(F32), 0.0)),)

    land_weights([(0, 0)], exchange("gather_first", [shards[(0, 0)]], True))
    saved = []
    for i in range(depth):
        x1 = xs
        h1 = rmsnorm_fwd(f"norm_mix{i}", x1, norm_mix[i][None])
        (qkv,) = fwd("qkv", i, lambda c: mm_nn(f"qkv{i}", h1, full[(i, 0)], out_dtype=BF16, tn=512, comm=c))
        if i % 2 == 0:
            o = fwd("attn", i, lambda c: attn_a_fwd(f"attn_a{i}", qkv, a_sinks[i // 2], comm=c))
        else:
            o = fwd("attn", i, lambda c: attn_b_fwd(f"attn_b{i}", qkv, comm=c))
        (x2,) = mm_nn(f"proj{i}", o, full[(i, 1)], out_dtype=F32, extras=(x1,), epilogue=add)[0]
        h2 = rmsnorm_fwd(f"norm_mlp{i}", x2, norm_mlp[i][None])
        u, act = fwd("mlp_in", i, lambda c: mm_nn(f"mlp_in{i}", h2, full[(i, 2)], out_dtype=BF16,
                                                   epilogue=relu2, n_out=2, comm=c))
        (xs,) = fwd("mlp_out", i, lambda c: mm_nn(f"mlp_out{i}", act, full[(i, 3)], out_dtype=F32,
                                                   extras=(x2,), epilogue=add, comm=c))
        saved.append((x1, h1, qkv, o, x2, h2, u, act))

    dx, dxb, d_final, loss_part = loss_head("loss_head", xs, final_norm[None], loss_target[0])
    loss = lax.psum(loss_part[0, 0], ("x", "y", "c"))

    d_mix, d_mlp, d_sinks = [None] * depth, [None] * depth, [None] * 2
    for i in reversed(range(depth)):
        x1, h1, qkv, o, x2, h2, u, act = saved[i]
        sendbuf[(i, 3)] = mm_tn(f"g_mlp_out{i}", act, dxb)[0].reshape(NDEV, DFF // NDEV, D)
        du = bwd("d_act", i, lambda c: mm_nt(f"d_act{i}", dxb, full[(i, 3)], out_dtype=BF16, extras=(u,),
                                             epilogue=drelu2, comm=c))
        sendbuf[(i, 2)] = mm_tn(f"g_mlp_in{i}", h2, du, col_shards=True)[0]
        dh2 = bwd("d_h2", i, lambda c: mm_nt(f"d_h2{i}", du, full[(i, 2)], out_dtype=F32, comm=c))
        dx, dxb, d_mlp[i] = rmsnorm_bwd(f"norm_mlp_bwd{i}", dh2, x2, norm_mlp[i][None], dx)
        sendbuf[(i, 1)] = mm_tn(f"g_proj{i}", o, dxb)[0].reshape(NDEV, D // NDEV, D)
        do = mm_nt(f"d_o{i}", dxb, full[(i, 1)], out_dtype=BF16)[0]
        if i % 2 == 0:
            dq, dk, dv, ds = bwd("attn_bwd", i, lambda c: attn_a_bwd(f"attn_a_bwd{i}", qkv, a_sinks[i // 2],
                                                                       o, do, comm=c))
            d_sinks[i // 2] = ds[0, :NQ_A]
            dqkv = jnp.concatenate([dq, dk, dv], axis=1)
        else:
            dq, dk, dv = bwd("attn_bwd", i, lambda c: attn_b_bwd(f"attn_b_bwd{i}", qkv, do, comm=c))
            dqkv = jnp.concatenate([dq, dk.astype(BF16), dv.astype(BF16)], axis=1)
        if i % 2 == 0:
            sendbuf[(i, 0)] = _shard_cols(mm_tn(f"g_qkv{i}", h1, dqkv, tn=512)[0])
        else:
            sendbuf[(i, 0)] = mm_tn(f"g_qkv{i}", h1, dqkv, col_shards=True)[0]
        dh1 = bwd("d_h1", i, lambda c: mm_nt(f"d_h1{i}", dqkv, full[(i, 0)], out_dtype=F32, comm=c))
        dx, dxb, d_mix[i] = rmsnorm_bwd(f"norm_mix_bwd{i}", dh1, x1, norm_mix[i][None], dx)

    small = jnp.zeros((16, D), F32)
    small = small.at[0:4].set(jnp.concatenate(d_mix, axis=0)).at[4:8].set(jnp.concatenate(d_mlp, axis=0))
    small = small.at[8].set(d_final[0]).at[9, :2 * NQ_A].set(jnp.concatenate(d_sinks))
    recv[(0, 0)], small_parts = exchange("exchange_last", [sendbuf[(0, 0)], small], [False, True])
    assert len(recv) == 4 * depth and len(full) == 4 * depth
    pack = lambda mix, mlp, fin, snk: (jnp.zeros((16, D), F32).at[0:4].set(mix).at[4:8].set(mlp)
                                       .at[8].set(fin).at[9, :2 * NQ_A].set(snk.reshape(-1)))
    sm = adamw("adamw_small", small_parts, pack(norm_mix, norm_mlp, final_norm, a_sinks)[None],
               pack(m_norm_mix, m_norm_mlp, m_final_norm, m_a_sinks)[None],
               pack(v_norm_mix, v_norm_mlp, v_final_norm, v_a_sinks)[None], rows=16)
    unpack = lambda a: (a[0, 0:4], a[0, 4:8], a[0, 8], a[0, 9, :2 * NQ_A].reshape(2, NQ_A))

    def update(kind, layers, t, w, m, v, rows):
        res = None
        for l, i in enumerate(layers):
            res = adamw(f"adamw_{kind}{i}", recv[(i, t)], w, m, v, rows, layer=l, prev=res)
        return res

    upd = [update("a_qkv", (0, 2), 0, a_w_qkv, m_a_w_qkv, v_a_w_qkv, 256),
           update("a_o", (0, 2), 1, a_w_o, m_a_w_o, v_a_w_o, 128),
           None,
           update("b_qkv", (1, 3), 0, b_w_qkv, m_b_w_qkv, v_b_w_qkv, 256),
           update("b_o", (1, 3), 1, b_w_o, m_b_w_o, v_b_w_o, 128),
           None, None,
           update("mlp_in", range(depth), 2, mlp_w_in, m_mlp_w_in, v_mlp_w_in, 256),
           update("mlp_out", range(depth), 3, mlp_w_out, m_mlp_w_out, v_mlp_w_out, 128),
           None]
    outs = []
    for q in range(4):
        mix, mlp, fin, snk = unpack(sm[q])
        small_out = {2: snk, 5: mix, 6: mlp, 9: fin}
        outs.append([small_out[k] if u is None else u[q] for k, u in enumerate(upd)])
    return (loss, dx[None], *outs[0], *outs[1], *outs[2], *outs[3])
```

```python
import functools
import math

import jax
import jax.numpy as jnp
from jax import lax
from jax.experimental import pallas as pl
from jax.experimental.pallas import tpu as pltpu

F32, BF16 = jnp.float32, jnp.bfloat16
T, D, DFF = 4096, 2048, 8192
HD, BLK = 64, 128
NQ_A, NKV_A, GROUP_A = 32, 4, 8
NH_B = 32
NDEV = 8
RMS_EPS = 1e-5
SCALE = 0.125
NEG = -1e30
VMEM_LIMIT = 56 * 1024 * 1024
LR, B1, B2, EPS, WD, STEP = 0.001, 0.9, 0.999, 1e-08, 0.01, 10
MESH = pl.DeviceIdType.MESH


def _cp(sem, vmem=VMEM_LIMIT):
    return pltpu.CompilerParams(dimension_semantics=sem, vmem_limit_bytes=vmem)


class Comm:
    def __init__(self, ins, gather):
        self.ins, self.nt = list(ins), len(ins)
        self.gather = list(gather) if isinstance(gather, (list, tuple)) else [gather] * self.nt
        self.out_shape = [jax.ShapeDtypeStruct((NDEV,) + (a.shape if g else a.shape[1:]), a.dtype)
                          for a, g in zip(self.ins, self.gather)]
        self.scratch = [pltpu.SemaphoreType.DMA((7 * self.nt,)), pltpu.SemaphoreType.DMA((7 * self.nt,)),
                        pltpu.SemaphoreType.DMA((self.nt,))]

    def _copies(self, src, dst, sems, *kinds):
        send_sems, recv_sems, local_sems = sems
        x, y, c = lax.axis_index("x"), lax.axis_index("y"), lax.axis_index("c")
        me, sib = (x, y, c), (x, y, 1 - c)
        chips = [(1 - x, y), (x, 1 - y), (1 - x, 1 - y)]
        p = {kind: [] for kind in kinds}
        for t in range(self.nt):
            def slot(d, t=t):
                return dst[t].at[4 * d[0] + 2 * d[1] + d[2]]

            def add(kind, k, a, b, to, t=t):
                if kind in p:
                    p[kind].append(pltpu.make_async_remote_copy(
                        src_ref=a, dst_ref=b, send_sem=send_sems.at[t * 7 + k],
                        recv_sem=recv_sems.at[t * 7 + k], device_id=to, device_id_type=MESH))

            own = src[t] if self.gather[t] else src[t].at[4 * x + 2 * y + c]
            if "local" in p:
                p["local"].append(pltpu.make_async_copy(own, slot(me), local_sems.at[t]))
            if self.gather[t]:
                add("first", 0, own, slot(me), sib)
                add("first_in", 0, own, slot(sib), sib)
                for j, chip in enumerate(chips):
                    there = (*chip, c)
                    add("first", 1 + j, own, slot(me), there)
                    add("chip_in", 1 + j, own, slot(there), there)
                    add("relay", 4 + j, slot(there), slot(there), sib)
                    add("relay_in", 4 + j, own, slot((*chip, 1 - c)), sib)
            else:
                for k in range(NDEV - 1):
                    bits = k + 1
                    peer = (1 - x if bits & 4 else x, 1 - y if bits & 2 else y, 1 - c if bits & 1 else c)
                    out = src[t].at[4 * peer[0] + 2 * peer[1] + peer[2]]
                    add("first", k, out, slot(me), peer)
                    add("first_in", k, out, slot(peer), peer)
        return [p[kind] for kind in kinds]

    def start(self, src, dst, sems):
        local, first = self._copies(src, dst, sems, "local", "first")
        for cp in local + first:
            cp.start()

    def relay(self, src, dst, sems):
        chip_in, relay = self._copies(src, dst, sems, "chip_in", "relay")
        for arrived, onward in zip(chip_in, relay):
            arrived.wait_recv()
            onward.start()

    def finish(self, src, dst, sems):
        first_in, relay_in, first, relay, local = self._copies(
            src, dst, sems, "first_in", "relay_in", "first", "relay", "local")
        for cp in first_in + relay_in:
            cp.wait_recv()
        for cp in first + relay:
            cp.wait_send()
        for cp in local:
            cp.wait()


def _pcall(name, body, *, grid, in_specs, out_specs, out_shape, scratch_shapes=(), sem, args, comm=None):
    in_specs, out_specs, out_shape = list(in_specs), list(out_specs), list(out_shape)
    scratch_shapes = list(scratch_shapes)
    if comm is None:
        outs = pl.pallas_call(body, grid=grid, in_specs=in_specs, out_specs=out_specs, out_shape=out_shape,
                              scratch_shapes=scratch_shapes, compiler_params=_cp(sem), name=name)(*args)
        return list(outs), []
    ni, no, ns, nt = len(in_specs), len(out_specs), len(scratch_shapes), comm.nt
    steps = math.prod(grid)
    hbm = pl.BlockSpec(memory_space=pltpu.HBM)

    def carrier(*refs):
        ins, cin = refs[:ni], refs[ni:ni + nt]
        outs, cout = refs[ni + nt:ni + nt + no], refs[ni + nt + no:ni + 2 * nt + no]
        scr, sems = refs[ni + 2 * nt + no:ni + 2 * nt + no + ns], refs[ni + 2 * nt + no + ns:]
        step = functools.reduce(lambda acc, d: acc * grid[d] + pl.program_id(d), range(len(grid)), 0)

        @pl.when(step == 0)
        def _():
            comm.start(cin, cout, sems)

        body(*ins, *outs, *scr)

        @pl.when(step == (7 * steps) // 8)
        def _():
            comm.relay(cin, cout, sems)

        @pl.when(step == steps - 1)
        def _():
            comm.finish(cin, cout, sems)

    outs = pl.pallas_call(
        carrier, grid=grid, in_specs=in_specs + [hbm] * nt, out_specs=out_specs + [hbm] * nt,
        out_shape=out_shape + comm.out_shape, scratch_shapes=scratch_shapes + comm.scratch,
        compiler_params=_cp(("arbitrary",) * len(grid)), name=name)(*args, *comm.ins)
    return list(outs[:no]), list(outs[no:])


def exchange(name, ins, gather):
    comm = Comm(ins, gather)
    hbm = pl.BlockSpec(memory_space=pltpu.HBM)

    def body(*refs):
        src, dst, sems = refs[:comm.nt], refs[comm.nt:2 * comm.nt], refs[2 * comm.nt:]
        comm.start(src, dst, sems)
        comm.relay(src, dst, sems)
        comm.finish(src, dst, sems)

    return pl.pallas_call(body, in_specs=[hbm] * comm.nt, out_specs=[hbm] * comm.nt, out_shape=comm.out_shape,
                          scratch_shapes=comm.scratch, name=name)(*ins)


def _matmul(name, a, b, extras, *, grid, a_spec, b_spec, extra_specs, out_shapes, out_specs,
            contract, epilogue, acc_shape, comm=None):
    nk = grid[2]
    ne, no = len(extras), len(out_shapes)

    def body(a_ref, b_ref, *rest):
        ex, outs, acc = rest[:ne], rest[ne:ne + no], rest[ne + no]
        kk = pl.program_id(2)

        @pl.when(kk == 0)
        def _():
            acc[...] = jnp.zeros_like(acc)

        acc[...] += lax.dot_general(a_ref[...].astype(BF16), b_ref[...].astype(BF16),
                                    (contract, ((), ())), preferred_element_type=F32)

        @pl.when(kk == nk - 1)
        def _():
            res = epilogue(acc[...], *[r[...] for r in ex])
            for r, o in zip(outs, res):
                r[...] = o.astype(r.dtype)

    return _pcall(name, body, grid=grid, in_specs=[a_spec, b_spec, *extra_specs], out_specs=out_specs,
                  out_shape=out_shapes, scratch_shapes=[pltpu.VMEM(acc_shape, F32)],
                  sem=("parallel", "parallel", "arbitrary"), args=(a, b, *extras), comm=comm)


def _ident(acc):
    return (acc,)


def _fit(tile, dim):
    return max(t for t in range(128, min(tile, dim) + 1, 128) if dim % t == 0)


def mm_nn(name, a, w, *, out_dtype, tm=1024, tn=1024, tk=2048, extras=(), epilogue=_ident, n_out=1,
          out_dtypes=None, comm=None):
    m, k = a.shape
    tm, tk = _fit(tm, m), _fit(tk, k)
    if w.ndim == 3:
        tn = w.shape[2]
        n = NDEV * tn
        b_spec = pl.BlockSpec((None, tk, tn), lambda i, j, kk: (j, kk, 0))
    else:
        n = w.shape[1]
        tn = _fit(tn, n)
        b_spec = pl.BlockSpec((tk, tn), lambda i, j, kk: (kk, j))
    out_dtypes = out_dtypes or (out_dtype,) * n_out
    mn = pl.BlockSpec((tm, tn), lambda i, j, kk: (i, j))
    return _matmul(name, a, w, extras, grid=(m // tm, n // tn, k // tk),
                   a_spec=pl.BlockSpec((tm, tk), lambda i, j, kk: (i, kk)), b_spec=b_spec,
                   extra_specs=[mn] * len(extras),
                   out_shapes=[jax.ShapeDtypeStruct((m, n), dt) for dt in out_dtypes],
                   out_specs=[mn] * len(out_dtypes), contract=((1,), (0,)), epilogue=epilogue,
                   acc_shape=(tm, tn), comm=comm)


def mm_nt(name, a, w, *, out_dtype, tm=1024, tn=1024, tk=2048, extras=(), epilogue=_ident, comm=None):
    m, n = a.shape
    k = w.shape[-2]
    tm, to = _fit(tm, m), _fit(tn, k)
    if w.ndim == 3:
        tc = w.shape[2]
        b_spec = pl.BlockSpec((None, to, tc), lambda i, j, kk: (kk, j, 0))
    else:
        tc = _fit(tk, n)
        b_spec = pl.BlockSpec((to, tc), lambda i, j, kk: (j, kk))
    mo = pl.BlockSpec((tm, to), lambda i, j, kk: (i, j))
    outs, got = _matmul(name, a, w, extras, grid=(m // tm, k // to, n // tc),
                        a_spec=pl.BlockSpec((tm, tc), lambda i, j, kk: (i, kk)), b_spec=b_spec,
                        extra_specs=[mo] * len(extras),
                        out_shapes=[jax.ShapeDtypeStruct((m, k), out_dtype)],
                        out_specs=[mo], contract=((1,), (1,)), epilogue=epilogue, acc_shape=(tm, to),
                        comm=comm)
    return outs[0], got


def mm_tn(name, a, b, *, out_dtype=BF16, tm=1024, tn=1024, tk=2048, col_shards=False, comm=None):
    t, k = a.shape
    n = b.shape[1]
    tm, tk = _fit(tm, k), _fit(tk, t)
    if col_shards:
        tn = n // NDEV
        shape, spec = (NDEV, k, tn), pl.BlockSpec((None, tm, tn), lambda i, j, kk: (j, i, 0))
    else:
        tn = _fit(tn, n)
        shape, spec = (k, n), pl.BlockSpec((tm, tn), lambda i, j, kk: (i, j))
    outs, got = _matmul(name, a, b, (), grid=(k // tm, n // tn, t // tk),
                        a_spec=pl.BlockSpec((tk, tm), lambda i, j, kk: (kk, i)),
                        b_spec=pl.BlockSpec((tk, tn), lambda i, j, kk: (kk, j)),
                        extra_specs=[], out_shapes=[jax.ShapeDtypeStruct(shape, out_dtype)],
                        out_specs=[spec],
                        contract=((0,), (0,)), epilogue=_ident, acc_shape=(tm, tn), comm=comm)
    return outs[0], got


ROWS = 256


def rmsnorm_fwd(name, x, gain):
    def body(x_ref, g_ref, h_ref):
        xf = x_ref[...]
        rstd = lax.rsqrt(jnp.mean(xf * xf, axis=-1, keepdims=True) + RMS_EPS)
        h_ref[...] = (xf * rstd * g_ref[...]).astype(BF16)

    row = pl.BlockSpec((ROWS, D), lambda i: (i, 0))
    return pl.pallas_call(body, grid=(T // ROWS,), in_specs=[row, pl.BlockSpec((1, D), lambda i: (0, 0))],
                          out_specs=row, out_shape=jax.ShapeDtypeStruct((T, D), BF16),
                          compiler_params=_cp(("parallel",)), name=name)(x, gain)


def _rms_bwd_math(dh, xf, gain):
    rstd = lax.rsqrt(jnp.mean(xf * xf, axis=-1, keepdims=True) + RMS_EPS)
    xhat = xf * rstd
    dgain = jnp.sum(dh * xhat, axis=0, keepdims=True)
    dxhat = dh * gain
    dx = rstd * (dxhat - xhat * jnp.mean(dxhat * xhat, axis=-1, keepdims=True))
    return dx, dgain


def rmsnorm_bwd(name, dh, x, gain, dres):
    def body(dh_ref, x_ref, g_ref, dres_ref, dx_ref, dxb_ref, dg_ref):
        dx, dgain = _rms_bwd_math(dh_ref[...], x_ref[...], g_ref[...])
        dx = dx + dres_ref[...]
        dx_ref[...] = dx
        dxb_ref[...] = dx.astype(BF16)

        @pl.when(pl.program_id(0) == 0)
        def _():
            dg_ref[...] = jnp.zeros_like(dg_ref)

        dg_ref[...] += dgain

    row = pl.BlockSpec((ROWS, D), lambda i: (i, 0))
    vec = pl.BlockSpec((1, D), lambda i: (0, 0))
    return pl.pallas_call(
        body, grid=(T // ROWS,), in_specs=[row, row, vec, row], out_specs=[row, row, vec],
        out_shape=[jax.ShapeDtypeStruct((T, D), F32), jax.ShapeDtypeStruct((T, D), BF16),
                   jax.ShapeDtypeStruct((1, D), F32)],
        compiler_params=_cp(("arbitrary",)), name=name)(dh, x, gain, dres)


def loss_head(name, x, gain, target):
    def body(x_ref, g_ref, t_ref, dx_ref, dxb_ref, dg_ref, loss_ref):
        xf, gain = x_ref[...], g_ref[...]
        rstd = lax.rsqrt(jnp.mean(xf * xf, axis=-1, keepdims=True) + RMS_EPS)
        err = xf * rstd * gain - t_ref[...]
        part = 0.5 * jnp.sum(jnp.mean(err * err, axis=-1, keepdims=True), axis=0, keepdims=True)
        dx, dgain = _rms_bwd_math(err * (1.0 / D), xf, gain)
        dx_ref[...] = dx
        dxb_ref[...] = dx.astype(BF16)

        @pl.when(pl.program_id(0) == 0)
        def _():
            dg_ref[...] = jnp.zeros_like(dg_ref)
            loss_ref[...] = jnp.zeros_like(loss_ref)

        dg_ref[...] += dgain
        loss_ref[...] += jnp.broadcast_to(part, loss_ref.shape)

    row = pl.BlockSpec((ROWS, D), lambda i: (i, 0))
    vec = pl.BlockSpec((1, D), lambda i: (0, 0))
    return pl.pallas_call(
        body, grid=(T // ROWS,), in_specs=[row, vec, row],
        out_specs=[row, row, vec, pl.BlockSpec((1, 128), lambda i: (0, 0))],
        out_shape=[jax.ShapeDtypeStruct((T, D), F32), jax.ShapeDtypeStruct((T, D), BF16),
                   jax.ShapeDtypeStruct((1, D), F32), jax.ShapeDtypeStruct((1, 128), F32)],
        compiler_params=_cp(("arbitrary",)), name=name)(x, gain, target)


def _a_common(n, g, s_ref, q_ref, kc_ref, kp_ref):
    row = lax.broadcasted_iota(jnp.int32, (GROUP_A * BLK, BLK), 0)
    col = lax.broadcasted_iota(jnp.int32, (GROUP_A * BLK, BLK), 1)
    qi = row & (BLK - 1)
    dist_c = (qi - col).astype(F32)
    valid_c = qi >= col
    valid_p = jnp.logical_and(col > qi, n > 0)
    hidx = lax.broadcasted_iota(jnp.int32, (GROUP_A * BLK, 1), 0) >> 7
    slope = jnp.exp((-math.log(2.0) / 4.0) * (hidx + (GROUP_A * g + 1)).astype(F32))
    sink = jnp.zeros((GROUP_A * BLK, 1), F32)
    for i in range(GROUP_A):
        sink = jnp.where(hidx == i, s_ref[GROUP_A * g + i], sink)
    qg = jnp.concatenate([q_ref[:, (GROUP_A * g + i) * HD:(GROUP_A * g + i + 1) * HD]
                          for i in range(GROUP_A)], axis=0)
    kc = kc_ref[:, g * HD:(g + 1) * HD]
    kp = kp_ref[:, g * HD:(g + 1) * HD]
    nt = (((1,), (1,)), ((), ()))
    sc = lax.dot_general(qg, kc, nt, preferred_element_type=F32) * SCALE - slope * dist_c
    sp = lax.dot_general(qg, kp, nt, preferred_element_type=F32) * SCALE - slope * (dist_c + float(BLK))
    sc = jnp.where(valid_c, sc, NEG)
    sp = jnp.where(valid_p, sp, NEG)
    m = jnp.maximum(jnp.maximum(jnp.max(sc, axis=-1, keepdims=True), jnp.max(sp, axis=-1, keepdims=True)), sink)
    pc = jnp.exp(sc - m)
    pp = jnp.exp(sp - m)
    ps = jnp.exp(sink - m)
    den = jnp.sum(pc, axis=-1, keepdims=True) + jnp.sum(pp, axis=-1, keepdims=True) + ps
    return qg, kc, kp, pc, pp, ps, den, hidx


def attn_a_fwd(name, qkv, sinks, comm=None):
    kcol, vcol = NQ_A * HD // (NKV_A * HD), NQ_A * HD // (NKV_A * HD) + 1

    def body(s_ref, q_ref, kc_ref, kp_ref, vc_ref, vp_ref, o_ref):
        n = pl.program_id(0)
        for g in range(NKV_A):
            qg, kc, kp, pc, pp, ps, den, _ = _a_common(n, g, s_ref, q_ref, kc_ref, kp_ref)
            vc = vc_ref[:, g * HD:(g + 1) * HD]
            vp = vp_ref[:, g * HD:(g + 1) * HD]
            og = (jnp.dot(pc.astype(BF16), vc, preferred_element_type=F32)
                  + jnp.dot(pp.astype(BF16), vp, preferred_element_type=F32)) / den
            for i in range(GROUP_A):
                h = GROUP_A * g + i
                o_ref[:, h * HD:(h + 1) * HD] = og[i * BLK:(i + 1) * BLK].astype(BF16)

    kvw = NKV_A * HD
    prev = lambda n: jnp.maximum(n - 1, 0)
    outs, got = _pcall(
        name, body, grid=(T // BLK,),
        in_specs=[pl.BlockSpec(memory_space=pltpu.SMEM),
                  pl.BlockSpec((BLK, NQ_A * HD), lambda n: (n, 0)),
                  pl.BlockSpec((BLK, kvw), lambda n: (n, kcol)),
                  pl.BlockSpec((BLK, kvw), lambda n: (prev(n), kcol)),
                  pl.BlockSpec((BLK, kvw), lambda n: (n, vcol)),
                  pl.BlockSpec((BLK, kvw), lambda n: (prev(n), vcol))],
        out_specs=[pl.BlockSpec((BLK, NQ_A * HD), lambda n: (n, 0))],
        out_shape=[jax.ShapeDtypeStruct((T, NQ_A * HD), BF16)],
        sem=("parallel",), args=(sinks, qkv, qkv, qkv, qkv, qkv), comm=comm)
    return outs[0], got


def attn_a_bwd(name, qkv, sinks, o, do, comm=None):
    kcol, vcol = NQ_A * HD // (NKV_A * HD), NQ_A * HD // (NKV_A * HD) + 1
    nb = T // BLK
    kvw = NKV_A * HD

    def body(s_ref, q_ref, kc_ref, kp_ref, vc_ref, vp_ref, o_ref, do_ref,
             dq_ref, dk_ref, dv_ref, ds_ref, ck_ref, cv_ref):
        n = pl.program_id(0)

        @pl.when(n == 0)
        def _():
            ds_ref[...] = jnp.zeros_like(ds_ref)
            ck_ref[...] = jnp.zeros_like(ck_ref)
            cv_ref[...] = jnp.zeros_like(cv_ref)

        @pl.when(n == nb)
        def _():
            dk_ref[...] = ck_ref[...].astype(BF16)
            dv_ref[...] = cv_ref[...].astype(BF16)

        @pl.when(n < nb)
        def _():
            lane = lax.broadcasted_iota(jnp.int32, (1, 128), 1)
            dsink_row = jnp.zeros((1, 128), F32)
            tn = (((0,), (0,)), ((), ()))
            nt = (((1,), (1,)), ((), ()))
            for g in range(NKV_A):
                qg, kc, kp, pc, pp, ps, den, hidx = _a_common(n, g, s_ref, q_ref, kc_ref, kp_ref)
                vc = vc_ref[:, g * HD:(g + 1) * HD]
                vp = vp_ref[:, g * HD:(g + 1) * HD]
                cols = [slice((GROUP_A * g + i) * HD, (GROUP_A * g + i + 1) * HD) for i in range(GROUP_A)]
                og = jnp.concatenate([o_ref[:, c] for c in cols], axis=0)
                dog = jnp.concatenate([do_ref[:, c] for c in cols], axis=0)
                delta = jnp.sum(og.astype(F32) * dog.astype(F32), axis=-1, keepdims=True)
                inv = 1.0 / den
                p_c, p_p = pc * inv, pp * inv
                dsc = p_c * (lax.dot_general(dog, vc, nt, preferred_element_type=F32) - delta)
                dsp = p_p * (lax.dot_general(dog, vp, nt, preferred_element_type=F32) - delta)
                dsk = -(ps * inv) * delta
                for i in range(GROUP_A):
                    tot = jnp.sum(jnp.where(hidx == i, dsk, 0.0), axis=0, keepdims=True)
                    dsink_row = dsink_row + jnp.where(lane == GROUP_A * g + i, tot, 0.0)
                dscb, dspb = (dsc * SCALE).astype(BF16), (dsp * SCALE).astype(BF16)
                dqg = (jnp.dot(dscb, kc, preferred_element_type=F32)
                       + jnp.dot(dspb, kp, preferred_element_type=F32))
                for i in range(GROUP_A):
                    dq_ref[:, cols[i]] = dqg[i * BLK:(i + 1) * BLK].astype(BF16)
                gs = slice(g * HD, (g + 1) * HD)
                dk_ref[:, gs] = (ck_ref[:, gs] + lax.dot_general(dspb, qg, tn, preferred_element_type=F32)).astype(BF16)
                dv_ref[:, gs] = (cv_ref[:, gs] + lax.dot_general(p_p.astype(BF16), dog, tn, preferred_element_type=F32)).astype(BF16)
                ck_ref[:, gs] = lax.dot_general(dscb, qg, tn, preferred_element_type=F32)
                cv_ref[:, gs] = lax.dot_general(p_c.astype(BF16), dog, tn, preferred_element_type=F32)
            ds_ref[...] += jnp.broadcast_to(dsink_row, ds_ref.shape)

    cur = lambda n: jnp.minimum(n, nb - 1)
    prev = lambda n: jnp.maximum(jnp.minimum(n, nb - 1) - 1, 0)
    outp = lambda n: jnp.maximum(n - 1, 0)
    qspec = pl.BlockSpec((BLK, NQ_A * HD), lambda n: (cur(n), 0))
    return _pcall(
        name, body, grid=(nb + 1,),
        in_specs=[pl.BlockSpec(memory_space=pltpu.SMEM), qspec,
                  pl.BlockSpec((BLK, kvw), lambda n: (cur(n), kcol)),
                  pl.BlockSpec((BLK, kvw), lambda n: (prev(n), kcol)),
                  pl.BlockSpec((BLK, kvw), lambda n: (cur(n), vcol)),
                  pl.BlockSpec((BLK, kvw), lambda n: (prev(n), vcol)),
                  qspec, qspec],
        out_specs=[qspec, pl.BlockSpec((BLK, kvw), lambda n: (outp(n), 0)),
                   pl.BlockSpec((BLK, kvw), lambda n: (outp(n), 0)),
                   pl.BlockSpec((8, 128), lambda n: (0, 0))],
        out_shape=[jax.ShapeDtypeStruct((T, NQ_A * HD), BF16), jax.ShapeDtypeStruct((T, kvw), BF16),
                   jax.ShapeDtypeStruct((T, kvw), BF16), jax.ShapeDtypeStruct((8, 128), F32)],
        scratch_shapes=[pltpu.VMEM((BLK, kvw), F32), pltpu.VMEM((BLK, kvw), F32)],
        sem=("arbitrary",), args=(sinks, qkv, qkv, qkv, qkv, qkv, o, do), comm=comm)


def _b_logs(z):
    lb = jnp.minimum(z, 0.0) - jnp.log(1.0 + jnp.exp(-jnp.abs(z)))
    return lb, lb - z


def _cumsum_excl(xs, later):
    r = lax.broadcasted_iota(jnp.int32, (BLK, BLK), 0)
    c = lax.broadcasted_iota(jnp.int32, (BLK, BLK), 1)
    tri = (r > c if later else r < c).astype(BF16)
    blocks = [slice(b * BLK, (b + 1) * BLK) for b in range(B_UNROLL)]
    pieces = []
    for x in xs:
        hi = x.astype(BF16)
        lo = (x - hi.astype(F32)).astype(BF16)
        pieces += [hi[:, b] for b in blocks] + [lo[:, b] for b in blocks]
    inside = jnp.dot(jnp.concatenate(pieces, axis=0), tri, preferred_element_type=F32)
    piece = lambda i: inside[i * BLK:(i + 1) * BLK]
    outs, totals = [], []
    for h, x in enumerate(xs):
        base = 2 * B_UNROLL * h
        sums = [jnp.sum(x[:, b], axis=-1, keepdims=True) for b in blocks]
        cols = [functools.reduce(jnp.add, sums[i + 1:] if later else sums[:i],
                                 piece(base + i) + piece(base + B_UNROLL + i)) for i in range(B_UNROLL)]
        outs.append(jnp.concatenate(cols, axis=1))
        totals.append(functools.reduce(jnp.add, sums))
    return outs, totals


B_CUT = 105.0
B_NH = 4
B_HEADS = tuple(slice(h * HD, (h + 1) * HD) for h in range(B_NH))
B_UNROLL = 3
B_WIN = B_UNROLL * BLK
B_GROUPS = -(-(T // BLK) // B_UNROLL)


def _b_window(j0, n):
    first = jnp.maximum(j0 - (B_UNROLL - 1), 0) * BLK
    row = lax.broadcasted_iota(jnp.int32, (BLK, B_WIN), 0)
    col = lax.broadcasted_iota(jnp.int32, (BLK, B_WIN), 1)
    valid = jnp.logical_and(col - row < n * BLK - first, col < (j0 + 1) * BLK - first)
    return pl.ds(pl.multiple_of(first, BLK), B_WIN), valid


def _b_weights(qs, kws, valid, cs):
    nt = (((1,), (1,)), ((), ()))
    zs = [lax.dot_general(q, kw, nt, preferred_element_type=F32) * SCALE for q, kw in zip(qs, kws)]
    logs = [_b_logs(z) for z in zs]
    lms = [jnp.where(valid, lm, 0.0) for _, lm in logs]
    afters, totals = _cumsum_excl(lms, later=True)
    weights = [jnp.where(valid, jnp.exp(lb + c + after), 0.0) for (lb, _), c, after in zip(logs, cs, afters)]
    return zs, weights, [c + t for c, t in zip(cs, totals)]


def _b_live(cs):
    return functools.reduce(jnp.maximum, [jnp.max(c) for c in cs]) > -B_CUT


def attn_b_fwd(name, qkv, comm=None):
    npair = NH_B // B_NH

    def body(q_ref, k_ref, v_ref, o_ref):
        n = pl.program_id(1)
        qs = [q_ref[:, hs] for hs in B_HEADS]

        def group(j0, st):
            rows, valid = _b_window(j0, n)
            _, weights, cs = _b_weights(qs, [k_ref[rows, hs] for hs in B_HEADS], valid, [c for c, _ in st])
            accs = [acc + jnp.dot(a.astype(BF16), v_ref[rows, hs], preferred_element_type=F32)
                    for (_, acc), a, hs in zip(st, weights, B_HEADS)]
            return tuple(zip(cs, accs))

        zero = (jnp.zeros((BLK, 1), F32), jnp.zeros((BLK, HD), F32))
        _, st = lax.while_loop(lambda cr: jnp.logical_and(cr[0] >= 0, _b_live([c for c, _ in cr[1]])),
                               lambda cr: (cr[0] - B_UNROLL, group(cr[0], cr[1])), (n, (zero,) * B_NH))
        for h, hs in enumerate(B_HEADS):
            o_ref[:, hs] = st[h][1].astype(BF16)

    outs, got = _pcall(
        name, body, grid=(npair, T // BLK),
        in_specs=[pl.BlockSpec((BLK, B_NH * HD), lambda p, n: (n, p)),
                  pl.BlockSpec((T, B_NH * HD), lambda p, n: (0, npair + p)),
                  pl.BlockSpec((T, B_NH * HD), lambda p, n: (0, 2 * npair + p))],
        out_specs=[pl.BlockSpec((BLK, B_NH * HD), lambda p, n: (n, p))],
        out_shape=[jax.ShapeDtypeStruct((T, NH_B * HD), BF16)],
        sem=("parallel", "arbitrary"), args=(qkv, qkv, qkv), comm=comm)
    return outs[0], got


def attn_b_bwd(name, qkv, do, comm=None):
    npair = NH_B // B_NH
    nb = T // BLK

    def body(q_ref, do_ref, k_ref, v_ref, dq_ref, dk_ref, dv_ref, a_s, z_s):
        n = pl.program_id(1)

        @pl.when(n == 0)
        def _():
            dk_ref[...] = jnp.zeros_like(dk_ref)
            dv_ref[...] = jnp.zeros_like(dv_ref)

        nt = (((1,), (1,)), ((), ()))
        tn = (((0,), (0,)), ((), ()))
        qs = [q_ref[:, hs] for hs in B_HEADS]
        dos = [do_ref[:, hs] for hs in B_HEADS]

        def sweep1(gi, j0, cs):
            rows, valid = _b_window(j0, n)
            zs, weights, cs = _b_weights(qs, [k_ref[rows, hs] for hs in B_HEADS], valid, cs)
            for h in range(B_NH):
                a_s[h, gi] = weights[h]
                z_s[h, gi] = zs[h]
            return tuple(cs)

        _, groups, _ = lax.while_loop(
            lambda cr: jnp.logical_and(cr[0] >= 0, _b_live(cr[2])),
            lambda cr: (cr[0] - B_UNROLL, cr[1] + 1, sweep1(cr[1], cr[0], cr[2])),
            (n, 0, (jnp.zeros((BLK, 1), F32),) * B_NH))

        def sweep2(i, st):
            gi = groups - 1 - i
            rows, valid = _b_window(n - gi * B_UNROLL, n)
            kws, vws = [k_ref[rows, hs] for hs in B_HEADS], [v_ref[rows, hs] for hs in B_HEADS]
            weights, zs = [a_s[h, gi] for h in range(B_NH)], [z_s[h, gi] for h in range(B_NH)]
            gs = [a * lax.dot_general(do, vw, nt, preferred_element_type=F32)
                  for a, do, vw in zip(weights, dos, vws)]
            befores, totals = _cumsum_excl(gs, later=False)
            dzbs = []
            for (cg, _), g, before, z in zip(st, gs, befores, zs):
                e = jnp.exp(-jnp.abs(z))
                inv = 1.0 / (1.0 + e)
                sig = jnp.where(z >= 0, inv, e * inv)
                dz = jnp.where(valid, g * (1.0 - sig) - (cg + before) * sig, 0.0)
                dzbs.append((dz * SCALE).astype(BF16))
            for h, hs in enumerate(B_HEADS):
                dk_ref[rows, hs] += lax.dot_general(dzbs[h], qs[h], tn, preferred_element_type=F32)
                dv_ref[rows, hs] += lax.dot_general(weights[h].astype(BF16), dos[h], tn, preferred_element_type=F32)
            return tuple((cg + total, dq + jnp.dot(dzb, kw, preferred_element_type=F32))
                         for (cg, dq), total, dzb, kw in zip(st, totals, dzbs, kws))

        zero = (jnp.zeros((BLK, 1), F32), jnp.zeros((BLK, HD), F32))
        st = lax.fori_loop(0, groups, sweep2, (zero,) * B_NH)
        for h, hs in enumerate(B_HEADS):
            dq_ref[:, hs] = st[h][1].astype(BF16)

    tile = pl.BlockSpec((BLK, B_NH * HD), lambda p, n: (n, p))
    full = lambda off: pl.BlockSpec((T, B_NH * HD), lambda p, n: (0, off + p))
    return _pcall(
        name, body, grid=(npair, nb),
        in_specs=[tile, tile, full(npair), full(2 * npair)],
        out_specs=[tile, full(0), full(0)],
        out_shape=[jax.ShapeDtypeStruct((T, NH_B * HD), BF16), jax.ShapeDtypeStruct((T, NH_B * HD), F32),
                   jax.ShapeDtypeStruct((T, NH_B * HD), F32)],
        scratch_shapes=[pltpu.VMEM((B_NH, B_GROUPS, BLK, B_WIN), F32)] * 2,
        sem=("parallel", "arbitrary"), args=(qkv, do, qkv, qkv), comm=comm)


def _adamw_math(w, g, m, v):
    m = B1 * m + (1.0 - B1) * g
    v = B2 * v + (1.0 - B2) * (g * g)
    m_hat = m / (1.0 - B1 ** STEP)
    v_hat = v / (1.0 - B2 ** STEP)
    delta = -LR * (m_hat / (jnp.sqrt(v_hat) + EPS) + WD * w)
    return delta, m, v


def adamw(name, parts, w, m, v, rows, layer=0, prev=None):
    nl, r, c = w.shape

    def body(p_ref, w_ref, m_ref, v_ref, *rest):
        g_ref, d_ref, nm_ref, nv_ref = rest[-4:]
        g = p_ref[0].astype(F32)
        for s in range(1, NDEV):
            g = g + p_ref[s].astype(F32)
        delta, nm, nv = _adamw_math(w_ref[...], g, m_ref[...], v_ref[...])
        g_ref[...] = g
        d_ref[...] = delta
        nm_ref[...] = nm
        nv_ref[...] = nv

    blk = pl.BlockSpec((None, rows, c), lambda i: (layer, i, 0))
    carried = [] if prev is None else list(prev)
    return pl.pallas_call(
        body, grid=(r // rows,),
        in_specs=[pl.BlockSpec((NDEV, rows, c), lambda i: (0, i, 0)), blk, blk, blk]
        + [pl.BlockSpec(memory_space=pl.ANY)] * len(carried),
        out_specs=[blk] * 4, out_shape=[jax.ShapeDtypeStruct((nl, r, c), F32)] * 4,
        input_output_aliases={4 + q: q for q in range(len(carried))},
        compiler_params=_cp(("parallel",)), name=name)(parts, w, m, v, *carried)


def _unshard_cols(g):
    return jnp.transpose(g, (1, 0, 2)).reshape(g.shape[1], -1)


def _shard_cols(w):
    k, n = w.shape
    return jnp.transpose(w.reshape(k, NDEV, n // NDEV), (1, 0, 2))


FWD_CARRY = {
    ("qkv", 0): [(0, 1)], ("attn", 0): [(0, 2)], ("mlp_in", 0): [(0, 3)], ("mlp_out", 0): [(1, 0)],
    ("qkv", 1): [(1, 1)], ("attn", 1): [(1, 2), (1, 3), (2, 0), (2, 1), (2, 2), (2, 3)],
    ("mlp_in", 1): [(3, 0)], ("mlp_out", 1): [(3, 1)],
    ("attn", 2): [(3, 2)], ("mlp_in", 2): [(3, 3)],
}
BWD_CARRY = {
    ("attn_bwd", 3): [(3, 3), (3, 2), (3, 1)],
    ("d_act", 2): [(3, 0)], ("attn_bwd", 2): [(2, 3)], ("d_h1", 2): [(2, 1)],
    ("d_act", 1): [(2, 0)], ("attn_bwd", 1): [(2, 2), (1, 3), (1, 2), (1, 1)],
    ("d_act", 0): [(1, 0)], ("d_h2", 0): [(0, 3)], ("attn_bwd", 0): [(0, 2)], ("d_h1", 0): [(0, 1)],
}


def kernel(x, a_w_qkv, a_w_o, a_sinks, b_w_qkv, b_w_o, norm_mix, norm_mlp, mlp_w_in, mlp_w_out, final_norm, loss_target, m_a_w_qkv, m_a_w_o, m_a_sinks, m_b_w_qkv, m_b_w_o, m_norm_mix, m_norm_mlp, m_mlp_w_in, m_mlp_w_out, m_final_norm, v_a_w_qkv, v_a_w_o, v_a_sinks, v_b_w_qkv, v_b_w_o, v_norm_mix, v_norm_mlp, v_mlp_w_in, v_mlp_w_out, v_final_norm):
    depth = 4
    xs = x[0]

    shards, full, sendbuf, recv = {}, {}, {}, {}
    for i in range(depth):
        j = i // 2
        wq, wo = (a_w_qkv[j], a_w_o[j]) if i % 2 == 0 else (b_w_qkv[j], b_w_o[j])
        for t, w in enumerate((wq, wo, mlp_w_in[i], mlp_w_out[i])):
            shards[(i, t)] = w.astype(BF16)

    def land_weights(keys, slabs):
        for (i, t), g in zip(keys, slabs):
            if t in (1, 3):
                full[(i, t)] = g.reshape(-1, D)
            else:
                full[(i, t)] = g if g.shape[2] % 128 == 0 else _unshard_cols(g)

    def land_grads(keys, slabs):
        recv.update(zip(keys, slabs))

    def carried(table, payload, gather, land, kind, i, fn):
        keys = table.get((kind, i), [])
        out, got = fn(Comm([payload[k] for k in keys], gather) if keys else None)
        land(keys, got)
        return out

    fwd = functools.partial(carried, FWD_CARRY, shards, True, land_weights)
    bwd = functools.partial(carried, BWD_CARRY, sendbuf, False, land_grads)

    relu2 = lambda acc: (acc, jnp.square(jnp.maximum(acc, 0.0)))
    add = lambda acc, res: (acc + res,)
    drelu2 = lambda acc, uu: (acc * (2.0 * jnp.maximum(uu.astype(F32), 0.0)),)

    land_weights([(0, 0)], exchange("gather_first", [shards[(0, 0)]], True))
    saved = []
    for i in range(depth):
        x1 = xs
        h1 = rmsnorm_fwd(f"norm_mix{i}", x1, norm_mix[i][None])
        (qkv,) = fwd("qkv", i, lambda c: mm_nn(f"qkv{i}", h1, full[(i, 0)], out_dtype=BF16, tn=512, comm=c))
        if i % 2 == 0:
            o = fwd("attn", i, lambda c: attn_a_fwd(f"attn_a{i}", qkv, a_sinks[i // 2], comm=c))
        else:
            o = fwd("attn", i, lambda c: attn_b_fwd(f"attn_b{i}", qkv, comm=c))
        (x2,) = mm_nn(f"proj{i}", o, full[(i, 1)], out_dtype=F32, extras=(x1,), epilogue=add)[0]
        h2 = rmsnorm_fwd(f"norm_mlp{i}", x2, norm_mlp[i][None])
        u, act = fwd("mlp_in", i, lambda c: mm_nn(f"mlp_in{i}", h2, full[(i, 2)], out_dtype=BF16,
                                                   epilogue=relu2, n_out=2, comm=c))
        (xs,) = fwd("mlp_out", i, lambda c: mm_nn(f"mlp_out{i}", act, full[(i, 3)], out_dtype=F32,
                                                   extras=(x2,), epilogue=add, comm=c))
        saved.append((x1, h1, qkv, o, x2, h2, u, act))

    dx, dxb, d_final, loss_part = loss_head("loss_head", xs, final_norm[None], loss_target[0])
    loss = lax.psum(loss_part[0, 0], ("x", "y", "c"))

    d_mix, d_mlp, d_sinks = [None] * depth, [None] * depth, [None] * 2
    for i in reversed(range(depth)):
        x1, h1, qkv, o, x2, h2, u, act = saved[i]
        sendbuf[(i, 3)] = mm_tn(f"g_mlp_out{i}", act, dxb)[0].reshape(NDEV, DFF // NDEV, D)
        du = bwd("d_act", i, lambda c: mm_nt(f"d_act{i}", dxb, full[(i, 3)], out_dtype=BF16, extras=(u,),
                                             epilogue=drelu2, comm=c))
        sendbuf[(i, 2)] = mm_tn(f"g_mlp_in{i}", h2, du, col_shards=True)[0]
        dh2 = bwd("d_h2", i, lambda c: mm_nt(f"d_h2{i}", du, full[(i, 2)], out_dtype=F32, comm=c))
        dx, dxb, d_mlp[i] = rmsnorm_bwd(f"norm_mlp_bwd{i}", dh2, x2, norm_mlp[i][None], dx)
        sendbuf[(i, 1)] = mm_tn(f"g_proj{i}", o, dxb)[0].reshape(NDEV, D // NDEV, D)
        do = mm_nt(f"d_o{i}", dxb, full[(i, 1)], out_dtype=BF16)[0]
        if i % 2 == 0:
            dq, dk, dv, ds = bwd("attn_bwd", i, lambda c: attn_a_bwd(f"attn_a_bwd{i}", qkv, a_sinks[i // 2],
                                                                       o, do, comm=c))
            d_sinks[i // 2] = ds[0, :NQ_A]
            dqkv = jnp.concatenate([dq, dk, dv], axis=1)
        else:
            dq, dk, dv = bwd("attn_bwd", i, lambda c: attn_b_bwd(f"attn_b_bwd{i}", qkv, do, comm=c))
            dqkv = jnp.concatenate([dq, dk.astype(BF16), dv.astype(BF16)], axis=1)
        if i % 2 == 0:
            sendbuf[(i, 0)] = _shard_cols(mm_tn(f"g_qkv{i}", h1, dqkv, tn=512)[0])
        else:
            sendbuf[(i, 0)] = mm_tn(f"g_qkv{i}", h1, dqkv, col_shards=True)[0]
        dh1 = bwd("d_h1", i, lambda c: mm_nt(f"d_h1{i}", dqkv, full[(i, 0)], out_dtype=F32, comm=c))
        dx, dxb, d_mix[i] = rmsnorm_bwd(f"norm_mix_bwd{i}", dh1, x1, norm_mix[i][None], dx)

    small = jnp.zeros((16, D), F32)
    small = small.at[0:4].set(jnp.concatenate(d_mix, axis=0)).at[4:8].set(jnp.concatenate(d_mlp, axis=0))
    small = small.at[8].set(d_final[0]).at[9, :2 * NQ_A].set(jnp.concatenate(d_sinks))
    recv[(0, 0)], small_parts = exchange("exchange_last", [sendbuf[(0, 0)], small], [False, True])
    assert len(recv) == 4 * depth and len(full) == 4 * depth
    pack = lambda mix, mlp, fin, snk: (jnp.zeros((16, D), F32).at[0:4].set(mix).at[4:8].set(mlp)
                                       .at[8].set(fin).at[9, :2 * NQ_A].set(snk.reshape(-1)))
    sm = adamw("adamw_small", small_parts, pack(norm_mix, norm_mlp, final_norm, a_sinks)[None],
               pack(m_norm_mix, m_norm_mlp, m_final_norm, m_a_sinks)[None],
               pack(v_norm_mix, v_norm_mlp, v_final_norm, v_a_sinks)[None], rows=16)
    unpack = lambda a: (a[0, 0:4], a[0, 4:8], a[0, 8], a[0, 9, :2 * NQ_A].reshape(2, NQ_A))

    def update(kind, layers, t, w, m, v, rows):
        res = None
        for l, i in enumerate(layers):
            res = adamw(f"adamw_{kind}{i}", recv[(i, t)], w, m, v, rows, layer=l, prev=res)
        return res

    upd = [update("a_qkv", (0, 2), 0, a_w_qkv, m_a_w_qkv, v_a_w_qkv, 256),
           update("a_o", (0, 2), 1, a_w_o, m_a_w_o, v_a_w_o, 128),
           None,
           update("b_qkv", (1, 3), 0, b_w_qkv, m_b_w_qkv, v_b_w_qkv, 256),
           update("b_o", (1, 3), 1, b_w_o, m_b_w_o, v_b_w_o, 128),
           None, None,
           update("mlp_in", range(depth), 2, mlp_w_in, m_mlp_w_in, v_mlp_w_in, 256),
           update("mlp_out", range(depth), 3, mlp_w_out, m_mlp_w_out, v_mlp_w_out, 128),
           None]
    outs = []
    for q in range(4):
        mix, mlp, fin, snk = unpack(sm[q])
        small_out = {2: snk, 5: mix, 6: mlp, 9: fin}
        outs.append([small_out[k] if u is None else u[q] for k, u in enumerate(upd)])
    return (loss, dx[None], *outs[0], *outs[1], *outs[2], *outs[3])
```

```python
import functools
import math

import jax
import jax.numpy as jnp
from jax import lax
from jax.experimental import pallas as pl
from jax.experimental.pallas import tpu as pltpu

F32, BF16 = jnp.float32, jnp.bfloat16
T, D, DFF = 4096, 2048, 8192
HD, BLK = 64, 128
NQ_A, NKV_A, GROUP_A = 32, 4, 8
NH_B = 32
NDEV = 8
RMS_EPS = 1e-5
SCALE = 0.125
NEG = -1e30
VMEM_LIMIT = 56 * 1024 * 1024
LR, B1, B2, EPS, WD, STEP = 0.001, 0.9, 0.999, 1e-08, 0.01, 10
MESH = pl.DeviceIdType.MESH


def _cp(sem, vmem=VMEM_LIMIT):
    return pltpu.CompilerParams(dimension_semantics=sem, vmem_limit_bytes=vmem)


class Comm:
    def __init__(self, ins, gather, rows=None, into=None):
        self.ins, self.nt = list(ins), len(ins)
        self.gather = list(gather) if isinstance(gather, (list, tuple)) else [gather] * self.nt
        self.rows = rows or [None] * self.nt
        self.into = into or [None] * self.nt
        self.out_shape = [jax.ShapeDtypeStruct((NDEV,) + (a.shape if g else a.shape[1:]), a.dtype)
                          for a, g in zip(self.ins, self.gather)]
        self.scratch = [pltpu.SemaphoreType.DMA((7 * self.nt,)), pltpu.SemaphoreType.DMA((7 * self.nt,)),
                        pltpu.SemaphoreType.DMA((self.nt,))]

    def _copies(self, src, dst, sems, *kinds):
        send_sems, recv_sems, local_sems = sems
        x, y, c = lax.axis_index("x"), lax.axis_index("y"), lax.axis_index("c")
        me, sib = (x, y, c), (x, y, 1 - c)
        chips = [(1 - x, y), (x, 1 - y), (1 - x, 1 - y)]
        p = {kind: [] for kind in kinds}
        for t in range(self.nt):
            part = (slice(None),) if self.rows[t] is None else (pl.ds(*self.rows[t]),)

            def slot(d, t=t, part=part):
                return dst[t].at[(4 * d[0] + 2 * d[1] + d[2],) + part]

            def add(kind, k, a, b, to, t=t):
                if kind in p:
                    p[kind].append(pltpu.make_async_remote_copy(
                        src_ref=a, dst_ref=b, send_sem=send_sems.at[t * 7 + k],
                        recv_sem=recv_sems.at[t * 7 + k], device_id=to, device_id_type=MESH))

            own = src[t] if self.gather[t] else src[t].at[(4 * x + 2 * y + c,) + part]
            if "local" in p:
                p["local"].append(pltpu.make_async_copy(own, slot(me), local_sems.at[t]))
            if self.gather[t]:
                add("first", 0, own, slot(me), sib)
                add("first_in", 0, own, slot(sib), sib)
                for j, chip in enumerate(chips):
                    there = (*chip, c)
                    add("first", 1 + j, own, slot(me), there)
                    add("chip_in", 1 + j, own, slot(there), there)
                    add("relay", 4 + j, slot(there), slot(there), sib)
                    add("relay_in", 4 + j, own, slot((*chip, 1 - c)), sib)
            else:
                for k in range(NDEV - 1):
                    bits = k + 1
                    peer = (1 - x if bits & 4 else x, 1 - y if bits & 2 else y, 1 - c if bits & 1 else c)
                    out = src[t].at[(4 * peer[0] + 2 * peer[1] + peer[2],) + part]
                    add("first", k, out, slot(me), peer)
                    add("first_in", k, out, slot(peer), peer)
        return [p[kind] for kind in kinds]

    def start(self, src, dst, sems):
        local, first = self._copies(src, dst, sems, "local", "first")
        for cp in local + first:
            cp.start()

    def relay(self, src, dst, sems):
        chip_in, relay = self._copies(src, dst, sems, "chip_in", "relay")
        for arrived, onward in zip(chip_in, relay):
            arrived.wait_recv()
            onward.start()

    def finish(self, src, dst, sems):
        first_in, relay_in, first, relay, local = self._copies(
            src, dst, sems, "first_in", "relay_in", "first", "relay", "local")
        for cp in first_in + relay_in:
            cp.wait_recv()
        for cp in first + relay:
            cp.wait_send()
        for cp in local:
            cp.wait()


def _pcall(name, body, *, grid, in_specs, out_specs, out_shape, scratch_shapes=(), sem, args, comm=None):
    in_specs, out_specs, out_shape = list(in_specs), list(out_specs), list(out_shape)
    scratch_shapes = list(scratch_shapes)
    if comm is None:
        outs = pl.pallas_call(body, grid=grid, in_specs=in_specs, out_specs=out_specs, out_shape=out_shape,
                              scratch_shapes=scratch_shapes, compiler_params=_cp(sem), name=name)(*args)
        return list(outs), []
    ni, no, ns, nt = len(in_specs), len(out_specs), len(scratch_shapes), comm.nt
    steps = math.prod(grid)
    hbm = pl.BlockSpec(memory_space=pltpu.HBM)
    landing = [t for t in range(nt) if comm.into[t] is not None]

    def carrier(*refs):
        ins, cin = refs[:ni], refs[ni:ni + nt]
        refs = refs[:ni + nt] + refs[ni + nt + len(landing):]
        outs, cout = refs[ni + nt:ni + nt + no], refs[ni + nt + no:ni + 2 * nt + no]
        scr, sems = refs[ni + 2 * nt + no:ni + 2 * nt + no + ns], refs[ni + 2 * nt + no + ns:]
        step = functools.reduce(lambda acc, d: acc * grid[d] + pl.program_id(d), range(len(grid)), 0)

        @pl.when(step == 0)
        def _():
            comm.start(cin, cout, sems)

        body(*ins, *outs, *scr)

        @pl.when(step == (7 * steps) // 8)
        def _():
            comm.relay(cin, cout, sems)

        @pl.when(step == steps - 1)
        def _():
            comm.finish(cin, cout, sems)

    outs = pl.pallas_call(
        carrier, grid=grid, in_specs=in_specs + [hbm] * (nt + len(landing)), out_specs=out_specs + [hbm] * nt,
        out_shape=out_shape + comm.out_shape, scratch_shapes=scratch_shapes + comm.scratch,
        input_output_aliases={ni + nt + a: no + t for a, t in enumerate(landing)},
        compiler_params=_cp(("arbitrary",) * len(grid)), name=name)(*args, *comm.ins, *[comm.into[t] for t in landing])
    return list(outs[:no]), list(outs[no:])


def exchange(name, ins, gather):
    comm = Comm(ins, gather)
    hbm = pl.BlockSpec(memory_space=pltpu.HBM)

    def body(*refs):
        src, dst, sems = refs[:comm.nt], refs[comm.nt:2 * comm.nt], refs[2 * comm.nt:]
        comm.start(src, dst, sems)
        comm.relay(src, dst, sems)
        comm.finish(src, dst, sems)

    return pl.pallas_call(body, in_specs=[hbm] * comm.nt, out_specs=[hbm] * comm.nt, out_shape=comm.out_shape,
                          scratch_shapes=comm.scratch, name=name)(*ins)


def _matmul(name, a, b, extras, *, grid, a_spec, b_spec, extra_specs, out_shapes, out_specs,
            contract, epilogue, acc_shape, comm=None):
    nk = grid[2]
    ne, no = len(extras), len(out_shapes)

    def body(a_ref, b_ref, *rest):
        ex, outs, acc = rest[:ne], rest[ne:ne + no], rest[ne + no]
        kk = pl.program_id(2)

        @pl.when(kk == 0)
        def _():
            acc[...] = jnp.zeros_like(acc)

        acc[...] += lax.dot_general(a_ref[...].astype(BF16), b_ref[...].astype(BF16),
                                    (contract, ((), ())), preferred_element_type=F32)

        @pl.when(kk == nk - 1)
        def _():
            res = epilogue(acc[...], *[r[...] for r in ex])
            for r, o in zip(outs, res):
                r[...] = o.astype(r.dtype)

    return _pcall(name, body, grid=grid, in_specs=[a_spec, b_spec, *extra_specs], out_specs=out_specs,
                  out_shape=out_shapes, scratch_shapes=[pltpu.VMEM(acc_shape, F32)],
                  sem=("parallel", "parallel", "arbitrary"), args=(a, b, *extras), comm=comm)


def _ident(acc):
    return (acc,)


def _fit(tile, dim):
    return max(t for t in range(128, min(tile, dim) + 1, 128) if dim % t == 0)


def mm_nn(name, a, w, *, out_dtype, tm=1024, tn=1024, tk=2048, extras=(), epilogue=_ident, n_out=1,
          out_dtypes=None, comm=None):
    m, k = a.shape
    tm, tk = _fit(tm, m), _fit(tk, k)
    if w.ndim == 3:
        tn = w.shape[2]
        n = NDEV * tn
        b_spec = pl.BlockSpec((None, tk, tn), lambda i, j, kk: (j, kk, 0))
    else:
        n = w.shape[1]
        tn = _fit(tn, n)
        b_spec = pl.BlockSpec((tk, tn), lambda i, j, kk: (kk, j))
    out_dtypes = out_dtypes or (out_dtype,) * n_out
    mn = pl.BlockSpec((tm, tn), lambda i, j, kk: (i, j))
    return _matmul(name, a, w, extras, grid=(m // tm, n // tn, k // tk),
                   a_spec=pl.BlockSpec((tm, tk), lambda i, j, kk: (i, kk)), b_spec=b_spec,
                   extra_specs=[mn] * len(extras),
                   out_shapes=[jax.ShapeDtypeStruct((m, n), dt) for dt in out_dtypes],
                   out_specs=[mn] * len(out_dtypes), contract=((1,), (0,)), epilogue=epilogue,
                   acc_shape=(tm, tn), comm=comm)


def mm_nt(name, a, w, *, out_dtype, tm=1024, tn=1024, tk=2048, extras=(), epilogue=_ident, comm=None):
    m, n = a.shape
    k = w.shape[-2]
    tm, to = _fit(tm, m), _fit(tn, k)
    if w.ndim == 3:
        tc = w.shape[2]
        b_spec = pl.BlockSpec((None, to, tc), lambda i, j, kk: (kk, j, 0))
    else:
        tc = _fit(tk, n)
        b_spec = pl.BlockSpec((to, tc), lambda i, j, kk: (j, kk))
    mo = pl.BlockSpec((tm, to), lambda i, j, kk: (i, j))
    outs, got = _matmul(name, a, w, extras, grid=(m // tm, k // to, n // tc),
                        a_spec=pl.BlockSpec((tm, tc), lambda i, j, kk: (i, kk)), b_spec=b_spec,
                        extra_specs=[mo] * len(extras),
                        out_shapes=[jax.ShapeDtypeStruct((m, k), out_dtype)],
                        out_specs=[mo], contract=((1,), (1,)), epilogue=epilogue, acc_shape=(tm, to),
                        comm=comm)
    return outs[0], got


def mm_tn(name, a, b, *, out_dtype=BF16, tm=1024, tn=1024, tk=2048, col_shards=False, comm=None):
    t, k = a.shape
    n = b.shape[1]
    tm, tk = _fit(tm, k), _fit(tk, t)
    if col_shards:
        tn = n // NDEV
        shape, spec = (NDEV, k, tn), pl.BlockSpec((None, tm, tn), lambda i, j, kk: (j, i, 0))
    else:
        tn = _fit(tn, n)
        shape, spec = (k, n), pl.BlockSpec((tm, tn), lambda i, j, kk: (i, j))
    outs, got = _matmul(name, a, b, (), grid=(k // tm, n // tn, t // tk),
                        a_spec=pl.BlockSpec((tk, tm), lambda i, j, kk: (kk, i)),
                        b_spec=pl.BlockSpec((tk, tn), lambda i, j, kk: (kk, j)),
                        extra_specs=[], out_shapes=[jax.ShapeDtypeStruct(shape, out_dtype)],
                        out_specs=[spec],
                        contract=((0,), (0,)), epilogue=_ident, acc_shape=(tm, tn), comm=comm)
    return outs[0], got


ROWS = 256


def rmsnorm_fwd(name, x, gain):
    def body(x_ref, g_ref, h_ref):
        xf = x_ref[...]
        rstd = lax.rsqrt(jnp.mean(xf * xf, axis=-1, keepdims=True) + RMS_EPS)
        h_ref[...] = (xf * rstd * g_ref[...]).astype(BF16)

    row = pl.BlockSpec((ROWS, D), lambda i: (i, 0))
    return pl.pallas_call(body, grid=(T // ROWS,), in_specs=[row, pl.BlockSpec((1, D), lambda i: (0, 0))],
                          out_specs=row, out_shape=jax.ShapeDtypeStruct((T, D), BF16),
                          compiler_params=_cp(("parallel",)), name=name)(x, gain)


def _rms_bwd_math(dh, xf, gain):
    rstd = lax.rsqrt(jnp.mean(xf * xf, axis=-1, keepdims=True) + RMS_EPS)
    xhat = xf * rstd
    dgain = jnp.sum(dh * xhat, axis=0, keepdims=True)
    dxhat = dh * gain
    dx = rstd * (dxhat - xhat * jnp.mean(dxhat * xhat, axis=-1, keepdims=True))
    return dx, dgain


def rmsnorm_bwd(name, dh, x, gain, dres):
    def body(dh_ref, x_ref, g_ref, dres_ref, dx_ref, dxb_ref, dg_ref):
        dx, dgain = _rms_bwd_math(dh_ref[...], x_ref[...], g_ref[...])
        dx = dx + dres_ref[...]
        dx_ref[...] = dx
        dxb_ref[...] = dx.astype(BF16)

        @pl.when(pl.program_id(0) == 0)
        def _():
            dg_ref[...] = jnp.zeros_like(dg_ref)

        dg_ref[...] += dgain

    row = pl.BlockSpec((ROWS, D), lambda i: (i, 0))
    vec = pl.BlockSpec((1, D), lambda i: (0, 0))
    return pl.pallas_call(
        body, grid=(T // ROWS,), in_specs=[row, row, vec, row], out_specs=[row, row, vec],
        out_shape=[jax.ShapeDtypeStruct((T, D), F32), jax.ShapeDtypeStruct((T, D), BF16),
                   jax.ShapeDtypeStruct((1, D), F32)],
        compiler_params=_cp(("arbitrary",)), name=name)(dh, x, gain, dres)


def loss_head(name, x, gain, target):
    def body(x_ref, g_ref, t_ref, dx_ref, dxb_ref, dg_ref, loss_ref):
        xf, gain = x_ref[...], g_ref[...]
        rstd = lax.rsqrt(jnp.mean(xf * xf, axis=-1, keepdims=True) + RMS_EPS)
        err = xf * rstd * gain - t_ref[...]
        part = 0.5 * jnp.sum(jnp.mean(err * err, axis=-1, keepdims=True), axis=0, keepdims=True)
        dx, dgain = _rms_bwd_math(err * (1.0 / D), xf, gain)
        dx_ref[...] = dx
        dxb_ref[...] = dx.astype(BF16)

        @pl.when(pl.program_id(0) == 0)
        def _():
            dg_ref[...] = jnp.zeros_like(dg_ref)
            loss_ref[...] = jnp.zeros_like(loss_ref)

        dg_ref[...] += dgain
        loss_ref[...] += jnp.broadcast_to(part, loss_ref.shape)

    row = pl.BlockSpec((ROWS, D), lambda i: (i, 0))
    vec = pl.BlockSpec((1, D), lambda i: (0, 0))
    return pl.pallas_call(
        body, grid=(T // ROWS,), in_specs=[row, vec, row],
        out_specs=[row, row, vec, pl.BlockSpec((1, 128), lambda i: (0, 0))],
        out_shape=[jax.ShapeDtypeStruct((T, D), F32), jax.ShapeDtypeStruct((T, D), BF16),
                   jax.ShapeDtypeStruct((1, D), F32), jax.ShapeDtypeStruct((1, 128), F32)],
        compiler_params=_cp(("arbitrary",)), name=name)(x, gain, target)


def _a_common(n, g, s_ref, q_ref, kc_ref, kp_ref):
    row = lax.broadcasted_iota(jnp.int32, (GROUP_A * BLK, BLK), 0)
    col = lax.broadcasted_iota(jnp.int32, (GROUP_A * BLK, BLK), 1)
    qi = row & (BLK - 1)
    dist_c = (qi - col).astype(F32)
    valid_c = qi >= col
    valid_p = jnp.logical_and(col > qi, n > 0)
    hidx = lax.broadcasted_iota(jnp.int32, (GROUP_A * BLK, 1), 0) >> 7
    slope = jnp.exp((-math.log(2.0) / 4.0) * (hidx + (GROUP_A * g + 1)).astype(F32))
    sink = jnp.zeros((GROUP_A * BLK, 1), F32)
    for i in range(GROUP_A):
        sink = jnp.where(hidx == i, s_ref[GROUP_A * g + i], sink)
    qg = jnp.concatenate([q_ref[:, (GROUP_A * g + i) * HD:(GROUP_A * g + i + 1) * HD]
                          for i in range(GROUP_A)], axis=0)
    kc = kc_ref[:, g * HD:(g + 1) * HD]
    kp = kp_ref[:, g * HD:(g + 1) * HD]
    nt = (((1,), (1,)), ((), ()))
    sc = lax.dot_general(qg, kc, nt, preferred_element_type=F32) * SCALE - slope * dist_c
    sp = lax.dot_general(qg, kp, nt, preferred_element_type=F32) * SCALE - slope * (dist_c + float(BLK))
    sc = jnp.where(valid_c, sc, NEG)
    sp = jnp.where(valid_p, sp, NEG)
    m = jnp.maximum(jnp.maximum(jnp.max(sc, axis=-1, keepdims=True), jnp.max(sp, axis=-1, keepdims=True)), sink)
    pc = jnp.exp(sc - m)
    pp = jnp.exp(sp - m)
    ps = jnp.exp(sink - m)
    den = jnp.sum(pc, axis=-1, keepdims=True) + jnp.sum(pp, axis=-1, keepdims=True) + ps
    return qg, kc, kp, pc, pp, ps, den, hidx


def attn_a_fwd(name, qkv, sinks, comm=None):
    kcol, vcol = NQ_A * HD // (NKV_A * HD), NQ_A * HD // (NKV_A * HD) + 1

    def body(s_ref, q_ref, kc_ref, kp_ref, vc_ref, vp_ref, o_ref):
        n = pl.program_id(0)
        for g in range(NKV_A):
            qg, kc, kp, pc, pp, ps, den, _ = _a_common(n, g, s_ref, q_ref, kc_ref, kp_ref)
            vc = vc_ref[:, g * HD:(g + 1) * HD]
            vp = vp_ref[:, g * HD:(g + 1) * HD]
            og = (jnp.dot(pc.astype(BF16), vc, preferred_element_type=F32)
                  + jnp.dot(pp.astype(BF16), vp, preferred_element_type=F32)) / den
            for i in range(GROUP_A):
                h = GROUP_A * g + i
                o_ref[:, h * HD:(h + 1) * HD] = og[i * BLK:(i + 1) * BLK].astype(BF16)

    kvw = NKV_A * HD
    prev = lambda n: jnp.maximum(n - 1, 0)
    outs, got = _pcall(
        name, body, grid=(T // BLK,),
        in_specs=[pl.BlockSpec(memory_space=pltpu.SMEM),
                  pl.BlockSpec((BLK, NQ_A * HD), lambda n: (n, 0)),
                  pl.BlockSpec((BLK, kvw), lambda n: (n, kcol)),
                  pl.BlockSpec((BLK, kvw), lambda n: (prev(n), kcol)),
                  pl.BlockSpec((BLK, kvw), lambda n: (n, vcol)),
                  pl.BlockSpec((BLK, kvw), lambda n: (prev(n), vcol))],
        out_specs=[pl.BlockSpec((BLK, NQ_A * HD), lambda n: (n, 0))],
        out_shape=[jax.ShapeDtypeStruct((T, NQ_A * HD), BF16)],
        sem=("parallel",), args=(sinks, qkv, qkv, qkv, qkv, qkv), comm=comm)
    return outs[0], got


def attn_a_bwd(name, qkv, sinks, o, do, comm=None):
    kcol, vcol = NQ_A * HD // (NKV_A * HD), NQ_A * HD // (NKV_A * HD) + 1
    nb = T // BLK
    kvw = NKV_A * HD

    def body(s_ref, q_ref, kc_ref, kp_ref, vc_ref, vp_ref, o_ref, do_ref,
             dq_ref, dk_ref, dv_ref, ds_ref, ck_ref, cv_ref):
        n = pl.program_id(0)

        @pl.when(n == 0)
        def _():
            ds_ref[...] = jnp.zeros_like(ds_ref)
            ck_ref[...] = jnp.zeros_like(ck_ref)
            cv_ref[...] = jnp.zeros_like(cv_ref)

        @pl.when(n == nb)
        def _():
            dk_ref[...] = ck_ref[...].astype(BF16)
            dv_ref[...] = cv_ref[...].astype(BF16)

        @pl.when(n < nb)
        def _():
            lane = lax.broadcasted_iota(jnp.int32, (1, 128), 1)
            dsink_row = jnp.zeros((1, 128), F32)
            tn = (((0,), (0,)), ((), ()))
            nt = (((1,), (1,)), ((), ()))
            for g in range(NKV_A):
                qg, kc, kp, pc, pp, ps, den, hidx = _a_common(n, g, s_ref, q_ref, kc_ref, kp_ref)
                vc = vc_ref[:, g * HD:(g + 1) * HD]
                vp = vp_ref[:, g * HD:(g + 1) * HD]
                cols = [slice((GROUP_A * g + i) * HD, (GROUP_A * g + i + 1) * HD) for i in range(GROUP_A)]
                og = jnp.concatenate([o_ref[:, c] for c in cols], axis=0)
                dog = jnp.concatenate([do_ref[:, c] for c in cols], axis=0)
                delta = jnp.sum(og.astype(F32) * dog.astype(F32), axis=-1, keepdims=True)
                inv = 1.0 / den
                p_c, p_p = pc * inv, pp * inv
                dsc = p_c * (lax.dot_general(dog, vc, nt, preferred_element_type=F32) - delta)
                dsp = p_p * (lax.dot_general(dog, vp, nt, preferred_element_type=F32) - delta)
                dsk = -(ps * inv) * delta
                for i in range(GROUP_A):
                    tot = jnp.sum(jnp.where(hidx == i, dsk, 0.0), axis=0, keepdims=True)
                    dsink_row = dsink_row + jnp.where(lane == GROUP_A * g + i, tot, 0.0)
                dscb, dspb = (dsc * SCALE).astype(BF16), (dsp * SCALE).astype(BF16)
                dqg = (jnp.dot(dscb, kc, preferred_element_type=F32)
                       + jnp.dot(dspb, kp, preferred_element_type=F32))
                for i in range(GROUP_A):
                    dq_ref[:, cols[i]] = dqg[i * BLK:(i + 1) * BLK].astype(BF16)
                gs = slice(g * HD, (g + 1) * HD)
                dk_ref[:, gs] = (ck_ref[:, gs] + lax.dot_general(dspb, qg, tn, preferred_element_type=F32)).astype(BF16)
                dv_ref[:, gs] = (cv_ref[:, gs] + lax.dot_general(p_p.astype(BF16), dog, tn, preferred_element_type=F32)).astype(BF16)
                ck_ref[:, gs] = lax.dot_general(dscb, qg, tn, preferred_element_type=F32)
                cv_ref[:, gs] = lax.dot_general(p_c.astype(BF16), dog, tn, preferred_element_type=F32)
            ds_ref[...] += jnp.broadcast_to(dsink_row, ds_ref.shape)

    cur = lambda n: jnp.minimum(n, nb - 1)
    prev = lambda n: jnp.maximum(jnp.minimum(n, nb - 1) - 1, 0)
    outp = lambda n: jnp.maximum(n - 1, 0)
    qspec = pl.BlockSpec((BLK, NQ_A * HD), lambda n: (cur(n), 0))
    return _pcall(
        name, body, grid=(nb + 1,),
        in_specs=[pl.BlockSpec(memory_space=pltpu.SMEM), qspec,
                  pl.BlockSpec((BLK, kvw), lambda n: (cur(n), kcol)),
                  pl.BlockSpec((BLK, kvw), lambda n: (prev(n), kcol)),
                  pl.BlockSpec((BLK, kvw), lambda n: (cur(n), vcol)),
                  pl.BlockSpec((BLK, kvw), lambda n: (prev(n), vcol)),
                  qspec, qspec],
        out_specs=[qspec, pl.BlockSpec((BLK, kvw), lambda n: (outp(n), 0)),
                   pl.BlockSpec((BLK, kvw), lambda n: (outp(n), 0)),
                   pl.BlockSpec((8, 128), lambda n: (0, 0))],
        out_shape=[jax.ShapeDtypeStruct((T, NQ_A * HD), BF16), jax.ShapeDtypeStruct((T, kvw), BF16),
                   jax.ShapeDtypeStruct((T, kvw), BF16), jax.ShapeDtypeStruct((8, 128), F32)],
        scratch_shapes=[pltpu.VMEM((BLK, kvw), F32), pltpu.VMEM((BLK, kvw), F32)],
        sem=("arbitrary",), args=(sinks, qkv, qkv, qkv, qkv, qkv, o, do), comm=comm)


def _b_logs(z):
    lb = jnp.minimum(z, 0.0) - jnp.log(1.0 + jnp.exp(-jnp.abs(z)))
    return lb, lb - z


def _cumsum_excl(xs, later):
    r = lax.broadcasted_iota(jnp.int32, (BLK, BLK), 0)
    c = lax.broadcasted_iota(jnp.int32, (BLK, BLK), 1)
    tri = (r > c if later else r < c).astype(BF16)
    blocks = [slice(b * BLK, (b + 1) * BLK) for b in range(B_UNROLL)]
    pieces = []
    for x in xs:
        hi = x.astype(BF16)
        lo = (x - hi.astype(F32)).astype(BF16)
        pieces += [hi[:, b] for b in blocks] + [lo[:, b] for b in blocks]
    inside = jnp.dot(jnp.concatenate(pieces, axis=0), tri, preferred_element_type=F32)
    piece = lambda i: inside[i * BLK:(i + 1) * BLK]
    outs, totals = [], []
    for h, x in enumerate(xs):
        base = 2 * B_UNROLL * h
        sums = [jnp.sum(x[:, b], axis=-1, keepdims=True) for b in blocks]
        cols = [functools.reduce(jnp.add, sums[i + 1:] if later else sums[:i],
                                 piece(base + i) + piece(base + B_UNROLL + i)) for i in range(B_UNROLL)]
        outs.append(jnp.concatenate(cols, axis=1))
        totals.append(functools.reduce(jnp.add, sums))
    return outs, totals


B_CUT = 105.0
B_NH = 4
B_HEADS = tuple(slice(h * HD, (h + 1) * HD) for h in range(B_NH))
B_UNROLL = 3
B_WIN = B_UNROLL * BLK
B_GROUPS = -(-(T // BLK) // B_UNROLL)


def _b_window(j0, n):
    first = jnp.maximum(j0 - (B_UNROLL - 1), 0) * BLK
    row = lax.broadcasted_iota(jnp.int32, (BLK, B_WIN), 0)
    col = lax.broadcasted_iota(jnp.int32, (BLK, B_WIN), 1)
    valid = jnp.logical_and(col - row < n * BLK - first, col < (j0 + 1) * BLK - first)
    return pl.ds(pl.multiple_of(first, BLK), B_WIN), valid


def _b_weights(qs, kws, valid, cs):
    nt = (((1,), (1,)), ((), ()))
    zs = [lax.dot_general(q, kw, nt, preferred_element_type=F32) * SCALE for q, kw in zip(qs, kws)]
    logs = [_b_logs(z) for z in zs]
    lms = [jnp.where(valid, lm, 0.0) for _, lm in logs]
    afters, totals = _cumsum_excl(lms, later=True)
    weights = [jnp.where(valid, jnp.exp(lb + c + after), 0.0) for (lb, _), c, after in zip(logs, cs, afters)]
    return zs, weights, [c + t for c, t in zip(cs, totals)]


def _b_live(cs):
    return functools.reduce(jnp.maximum, [jnp.max(c) for c in cs]) > -B_CUT


def attn_b_fwd(name, qkv, comm=None):
    npair = NH_B // B_NH

    def body(q_ref, k_ref, v_ref, o_ref):
        n = pl.program_id(1)
        qs = [q_ref[:, hs] for hs in B_HEADS]

        def group(j0, st):
            rows, valid = _b_window(j0, n)
            _, weights, cs = _b_weights(qs, [k_ref[rows, hs] for hs in B_HEADS], valid, [c for c, _ in st])
            accs = [acc + jnp.dot(a.astype(BF16), v_ref[rows, hs], preferred_element_type=F32)
                    for (_, acc), a, hs in zip(st, weights, B_HEADS)]
            return tuple(zip(cs, accs))

        zero = (jnp.zeros((BLK, 1), F32), jnp.zeros((BLK, HD), F32))
        _, st = lax.while_loop(lambda cr: jnp.logical_and(cr[0] >= 0, _b_live([c for c, _ in cr[1]])),
                               lambda cr: (cr[0] - B_UNROLL, group(cr[0], cr[1])), (n, (zero,) * B_NH))
        for h, hs in enumerate(B_HEADS):
            o_ref[:, hs] = st[h][1].astype(BF16)

    outs, got = _pcall(
        name, body, grid=(npair, T // BLK),
        in_specs=[pl.BlockSpec((BLK, B_NH * HD), lambda p, n: (n, p)),
                  pl.BlockSpec((T, B_NH * HD), lambda p, n: (0, npair + p)),
                  pl.BlockSpec((T, B_NH * HD), lambda p, n: (0, 2 * npair + p))],
        out_specs=[pl.BlockSpec((BLK, B_NH * HD), lambda p, n: (n, p))],
        out_shape=[jax.ShapeDtypeStruct((T, NH_B * HD), BF16)],
        sem=("parallel", "arbitrary"), args=(qkv, qkv, qkv), comm=comm)
    return outs[0], got


def attn_b_bwd(name, qkv, do, comm=None):
    npair = NH_B // B_NH
    nb = T // BLK

    def body(q_ref, do_ref, k_ref, v_ref, dq_ref, dk_ref, dv_ref, a_s, z_s):
        n = pl.program_id(1)

        @pl.when(n == 0)
        def _():
            dk_ref[...] = jnp.zeros_like(dk_ref)
            dv_ref[...] = jnp.zeros_like(dv_ref)

        nt = (((1,), (1,)), ((), ()))
        tn = (((0,), (0,)), ((), ()))
        qs = [q_ref[:, hs] for hs in B_HEADS]
        dos = [do_ref[:, hs] for hs in B_HEADS]

        def sweep1(gi, j0, cs):
            rows, valid = _b_window(j0, n)
            zs, weights, cs = _b_weights(qs, [k_ref[rows, hs] for hs in B_HEADS], valid, cs)
            for h in range(B_NH):
                a_s[h, gi] = weights[h]
                z_s[h, gi] = zs[h]
            return tuple(cs)

        _, groups, _ = lax.while_loop(
            lambda cr: jnp.logical_and(cr[0] >= 0, _b_live(cr[2])),
            lambda cr: (cr[0] - B_UNROLL, cr[1] + 1, sweep1(cr[1], cr[0], cr[2])),
            (n, 0, (jnp.zeros((BLK, 1), F32),) * B_NH))

        def sweep2(i, st):
            gi = groups - 1 - i
            rows, valid = _b_window(n - gi * B_UNROLL, n)
            kws, vws = [k_ref[rows, hs] for hs in B_HEADS], [v_ref[rows, hs] for hs in B_HEADS]
            weights, zs = [a_s[h, gi] for h in range(B_NH)], [z_s[h, gi] for h in range(B_NH)]
            gs = [a * lax.dot_general(do, vw, nt, preferred_element_type=F32)
                  for a, do, vw in zip(weights, dos, vws)]
            befores, totals = _cumsum_excl(gs, later=False)
            dzbs = []
            for (cg, _), g, before, z in zip(st, gs, befores, zs):
                e = jnp.exp(-jnp.abs(z))
                inv = 1.0 / (1.0 + e)
                sig = jnp.where(z >= 0, inv, e * inv)
                dz = jnp.where(valid, g * (1.0 - sig) - (cg + before) * sig, 0.0)
                dzbs.append((dz * SCALE).astype(BF16))
            for h, hs in enumerate(B_HEADS):
                dk_ref[rows, hs] += lax.dot_general(dzbs[h], qs[h], tn, preferred_element_type=F32)
                dv_ref[rows, hs] += lax.dot_general(weights[h].astype(BF16), dos[h], tn, preferred_element_type=F32)
            return tuple((cg + total, dq + jnp.dot(dzb, kw, preferred_element_type=F32))
                         for (cg, dq), total, dzb, kw in zip(st, totals, dzbs, kws))

        zero = (jnp.zeros((BLK, 1), F32), jnp.zeros((BLK, HD), F32))
        st = lax.fori_loop(0, groups, sweep2, (zero,) * B_NH)
        for h, hs in enumerate(B_HEADS):
            dq_ref[:, hs] = st[h][1].astype(BF16)

    tile = pl.BlockSpec((BLK, B_NH * HD), lambda p, n: (n, p))
    full = lambda off: pl.BlockSpec((T, B_NH * HD), lambda p, n: (0, off + p))
    return _pcall(
        name, body, grid=(npair, nb),
        in_specs=[tile, tile, full(npair), full(2 * npair)],
        out_specs=[tile, full(0), full(0)],
        out_shape=[jax.ShapeDtypeStruct((T, NH_B * HD), BF16), jax.ShapeDtypeStruct((T, NH_B * HD), F32),
                   jax.ShapeDtypeStruct((T, NH_B * HD), F32)],
        scratch_shapes=[pltpu.VMEM((B_NH, B_GROUPS, BLK, B_WIN), F32)] * 2,
        sem=("parallel", "arbitrary"), args=(qkv, do, qkv, qkv), comm=comm)


def _adamw_math(w, g, m, v):
    m = B1 * m + (1.0 - B1) * g
    v = B2 * v + (1.0 - B2) * (g * g)
    m_hat = m / (1.0 - B1 ** STEP)
    v_hat = v / (1.0 - B2 ** STEP)
    delta = -LR * (m_hat / (jnp.sqrt(v_hat) + EPS) + WD * w)
    return delta, m, v


def adamw(name, parts, w, m, v, rows, layer=0, prev=None):
    nl, r, c = w.shape

    def body(p_ref, w_ref, m_ref, v_ref, *rest):
        g_ref, d_ref, nm_ref, nv_ref = rest[-4:]
        g = p_ref[0].astype(F32)
        for s in range(1, NDEV):
            g = g + p_ref[s].astype(F32)
        delta, nm, nv = _adamw_math(w_ref[...], g, m_ref[...], v_ref[...])
        g_ref[...] = g
        d_ref[...] = delta
        nm_ref[...] = nm
        nv_ref[...] = nv

    blk = pl.BlockSpec((None, rows, c), lambda i: (layer, i, 0))
    carried = [] if prev is None else list(prev)
    return pl.pallas_call(
        body, grid=(r // rows,),
        in_specs=[pl.BlockSpec((NDEV, rows, c), lambda i: (0, i, 0)), blk, blk, blk]
        + [pl.BlockSpec(memory_space=pl.ANY)] * len(carried),
        out_specs=[blk] * 4, out_shape=[jax.ShapeDtypeStruct((nl, r, c), F32)] * 4,
        input_output_aliases={4 + q: q for q in range(len(carried))},
        compiler_params=_cp(("parallel",)), name=name)(parts, w, m, v, *carried)


def _unshard_cols(g):
    return jnp.transpose(g, (1, 0, 2)).reshape(g.shape[1], -1)


def _shard_cols(w):
    k, n = w.shape
    return jnp.transpose(w.reshape(k, NDEV, n // NDEV), (1, 0, 2))


FWD_CARRY = {
    ("qkv", 0): [(0, 1)], ("attn", 0): [(0, 2)], ("mlp_in", 0): [(0, 3)], ("mlp_out", 0): [(1, 0)],
    ("qkv", 1): [(1, 1)], ("attn", 1): [(1, 2), (1, 3), (2, 0), (2, 1)],
    ("mlp_in", 1): [(2, 2)], ("mlp_out", 1): [(2, 3)],
    ("qkv", 2): [(3, 1)], ("attn", 2): [(3, 0)], ("mlp_in", 2): [(3, 2)], ("mlp_out", 2): [(3, 3)],
}
BWD_CARRY = {
    ("attn_bwd", 3): [(3, 3), (3, 2), (3, 1)],
    ("g_mlp_out", 2): [(3, 0, 0)], ("d_act", 2): [(3, 0, 1)], ("g_mlp_in", 2): [(2, 3, 0)],
    ("d_h2", 2): [(2, 3, 1)], ("attn_bwd", 2): [(2, 2)], ("d_h1", 2): [(2, 1)],
    ("g_mlp_out", 1): [(2, 0)], ("attn_bwd", 1): [(1, 3), (1, 2), (1, 1)],
    ("g_mlp_out", 0): [(1, 0, 0)], ("d_act", 0): [(1, 0, 1)], ("g_mlp_in", 0): [(0, 3, 0)],
    ("d_h2", 0): [(0, 3, 1)], ("attn_bwd", 0): [(0, 2)], ("d_h1", 0): [(0, 1)],
}


def kernel(x, a_w_qkv, a_w_o, a_sinks, b_w_qkv, b_w_o, norm_mix, norm_mlp, mlp_w_in, mlp_w_out, final_norm, loss_target, m_a_w_qkv, m_a_w_o, m_a_sinks, m_b_w_qkv, m_b_w_o, m_norm_mix, m_norm_mlp, m_mlp_w_in, m_mlp_w_out, m_final_norm, v_a_w_qkv, v_a_w_o, v_a_sinks, v_b_w_qkv, v_b_w_o, v_norm_mix, v_norm_mlp, v_mlp_w_in, v_mlp_w_out, v_final_norm):
    depth = 4
    xs = x[0]

    shards, full, sendbuf, recv = {}, {}, {}, {}
    for i in range(depth):
        j = i // 2
        wq, wo = (a_w_qkv[j], a_w_o[j]) if i % 2 == 0 else (b_w_qkv[j], b_w_o[j])
        for t, w in enumerate((wq, wo, mlp_w_in[i], mlp_w_out[i])):
            shards[(i, t)] = w.astype(BF16)

    def land_weights(keys, slabs):
        for (i, t), g in zip(keys, slabs):
            if t in (1, 3):
                full[(i, t)] = g.reshape(-1, D)
            else:
                full[(i, t)] = g if g.shape[2] % 128 == 0 else _unshard_cols(g)

    def fwd(kind, i, fn):
        keys = FWD_CARRY.get((kind, i), [])
        out, got = fn(Comm([shards[k] for k in keys], True) if keys else None)
        land_weights(keys, got)
        return out

    def bwd(kind, i, fn):
        keys = BWD_CARRY.get((kind, i), [])
        bufs = [sendbuf[key[:2]] for key in keys]
        halves = [None if len(key) == 2 else (key[2] * (b.shape[1] // 2), b.shape[1] // 2) for key, b in zip(keys, bufs)]
        out, got = fn(Comm(bufs, False, halves, [recv.get(key[:2]) for key in keys]) if keys else None)
        recv.update((key[:2], g) for key, g in zip(keys, got))
        return out

    relu2 = lambda acc: (acc, jnp.square(jnp.maximum(acc, 0.0)))
    add = lambda acc, res: (acc + res,)
    drelu2 = lambda acc, uu: (acc * (2.0 * jnp.maximum(uu.astype(F32), 0.0)),)

    land_weights([(0, 0)], exchange("gather_first", [shards[(0, 0)]], True))
    saved = []
    for i in range(depth):
        x1 = xs
        h1 = rmsnorm_fwd(f"norm_mix{i}", x1, norm_mix[i][None])
        (qkv,) = fwd("qkv", i, lambda c: mm_nn(f"qkv{i}", h1, full[(i, 0)], out_dtype=BF16, tn=512, comm=c))
        if i % 2 == 0:
            o = fwd("attn", i, lambda c: attn_a_fwd(f"attn_a{i}", qkv, a_sinks[i // 2], comm=c))
        else:
            o = fwd("attn", i, lambda c: attn_b_fwd(f"attn_b{i}", qkv, comm=c))
        (x2,) = mm_nn(f"proj{i}", o, full[(i, 1)], out_dtype=F32, extras=(x1,), epilogue=add)[0]
        h2 = rmsnorm_fwd(f"norm_mlp{i}", x2, norm_mlp[i][None])
        u, act = fwd("mlp_in", i, lambda c: mm_nn(f"mlp_in{i}", h2, full[(i, 2)], out_dtype=BF16,
                                                   epilogue=relu2, n_out=2, comm=c))
        (xs,) = fwd("mlp_out", i, lambda c: mm_nn(f"mlp_out{i}", act, full[(i, 3)], out_dtype=F32,
                                                   extras=(x2,), epilogue=add, comm=c))
        saved.append((x1, h1, qkv, o, x2, h2, u, act))

    dx, dxb, d_final, loss_part = loss_head("loss_head", xs, final_norm[None], loss_target[0])
    loss = lax.psum(loss_part[0, 0], ("x", "y", "c"))

    d_mix, d_mlp, d_sinks = [None] * depth, [None] * depth, [None] * 2
    for i in reversed(range(depth)):
        x1, h1, qkv, o, x2, h2, u, act = saved[i]
        sendbuf[(i, 3)] = bwd("g_mlp_out", i, lambda c: mm_tn(f"g_mlp_out{i}", act, dxb, comm=c)).reshape(
            NDEV, DFF // NDEV, D)
        du = bwd("d_act", i, lambda c: mm_nt(f"d_act{i}", dxb, full[(i, 3)], out_dtype=BF16, extras=(u,),
                                             epilogue=drelu2, comm=c))
        sendbuf[(i, 2)] = bwd("g_mlp_in", i, lambda c: mm_tn(f"g_mlp_in{i}", h2, du, col_shards=True, comm=c))
        dh2 = bwd("d_h2", i, lambda c: mm_nt(f"d_h2{i}", du, full[(i, 2)], out_dtype=F32, comm=c))
        dx, dxb, d_mlp[i] = rmsnorm_bwd(f"norm_mlp_bwd{i}", dh2, x2, norm_mlp[i][None], dx)
        sendbuf[(i, 1)] = mm_tn(f"g_proj{i}", o, dxb)[0].reshape(NDEV, D // NDEV, D)
        do = mm_nt(f"d_o{i}", dxb, full[(i, 1)], out_dtype=BF16)[0]
        if i % 2 == 0:
            dq, dk, dv, ds = bwd("attn_bwd", i, lambda c: attn_a_bwd(f"attn_a_bwd{i}", qkv, a_sinks[i // 2],
                                                                       o, do, comm=c))
            d_sinks[i // 2] = ds[0, :NQ_A]
            dqkv = jnp.concatenate([dq, dk, dv], axis=1)
        else:
            dq, dk, dv = bwd("attn_bwd", i, lambda c: attn_b_bwd(f"attn_b_bwd{i}", qkv, do, comm=c))
            dqkv = jnp.concatenate([dq, dk.astype(BF16), dv.astype(BF16)], axis=1)
        if i % 2 == 0:
            sendbuf[(i, 0)] = _shard_cols(mm_tn(f"g_qkv{i}", h1, dqkv, tn=512)[0])
        else:
            sendbuf[(i, 0)] = mm_tn(f"g_qkv{i}", h1, dqkv, col_shards=True)[0]
        dh1 = bwd("d_h1", i, lambda c: mm_nt(f"d_h1{i}", dqkv, full[(i, 0)], out_dtype=F32, comm=c))
        dx, dxb, d_mix[i] = rmsnorm_bwd(f"norm_mix_bwd{i}", dh1, x1, norm_mix[i][None], dx)

    small = jnp.zeros((16, D), F32)
    small = small.at[0:4].set(jnp.concatenate(d_mix, axis=0)).at[4:8].set(jnp.concatenate(d_mlp, axis=0))
    small = small.at[8].set(d_final[0]).at[9, :2 * NQ_A].set(jnp.concatenate(d_sinks))
    recv[(0, 0)], small_parts = exchange("exchange_last", [sendbuf[(0, 0)], small], [False, True])
    assert len(recv) == 4 * depth and len(full) == 4 * depth
    pack = lambda mix, mlp, fin, snk: (jnp.zeros((16, D), F32).at[0:4].set(mix).at[4:8].set(mlp)
                                       .at[8].set(fin).at[9, :2 * NQ_A].set(snk.reshape(-1)))
    sm = adamw("adamw_small", small_parts, pack(norm_mix, norm_mlp, final_norm, a_sinks)[None],
               pack(m_norm_mix, m_norm_mlp, m_final_norm, m_a_sinks)[None],
               pack(v_norm_mix, v_norm_mlp, v_final_norm, v_a_sinks)[None], rows=16)
    unpack = lambda a: (a[0, 0:4], a[0, 4:8], a[0, 8], a[0, 9, :2 * NQ_A].reshape(2, NQ_A))

    def update(kind, layers, t, w, m, v, rows):
        res = None
        for l, i in enumerate(layers):
            res = adamw(f"adamw_{kind}{i}", recv[(i, t)], w, m, v, rows, layer=l, prev=res)
        return res

    upd = [update("a_qkv", (0, 2), 0, a_w_qkv, m_a_w_qkv, v_a_w_qkv, 256),
           update("a_o", (0, 2), 1, a_w_o, m_a_w_o, v_a_w_o, 128),
           None,
           update("b_qkv", (1, 3), 0, b_w_qkv, m_b_w_qkv, v_b_w_qkv, 256),
           update("b_o", (1, 3), 1, b_w_o, m_b_w_o, v_b_w_o, 128),
           None, None,
           update("mlp_in", range(depth), 2, mlp_w_in, m_mlp_w_in, v_mlp_w_in, 256),
           update("mlp_out", range(depth), 3, mlp_w_out, m_mlp_w_out, v_mlp_w_out, 128),
           None]
    outs = []
    for q in range(4):
        mix, mlp, fin, snk = unpack(sm[q])
        small_out = {2: snk, 5: mix, 6: mlp, 9: fin}
        outs.append([small_out[k] if u is None else u[q] for k, u in enumerate(upd)])
    return (loss, dx[None], *outs[0], *outs[1], *outs[2], *outs[3])
```

```python
import functools
import math

import jax
import jax.numpy as jnp
from jax import lax
from jax.experimental import pallas as pl
from jax.experimental.pallas import tpu as pltpu

F32, BF16 = jnp.float32, jnp.bfloat16
T, D, DFF = 4096, 2048, 8192
HD, BLK = 64, 128
NQ_A, NKV_A, GROUP_A = 32, 4, 8
NH_B = 32
NDEV = 8
RMS_EPS = 1e-5
SCALE = 0.125
NEG = -1e30
VMEM_LIMIT = 56 * 1024 * 1024
LR, B1, B2, EPS, WD, STEP = 0.001, 0.9, 0.999, 1e-08, 0.01, 10
MESH = pl.DeviceIdType.MESH


def _cp(sem, vmem=VMEM_LIMIT):
    return pltpu.CompilerParams(dimension_semantics=sem, vmem_limit_bytes=vmem)


class Comm:
    def __init__(self, ins, gather, rows=None, into=None):
        self.ins, self.nt = list(ins), len(ins)
        self.gather = list(gather) if isinstance(gather, (list, tuple)) else [gather] * self.nt
        self.rows = rows or [None] * self.nt
        self.into = into or [None] * self.nt
        self.out_shape = [jax.ShapeDtypeStruct((NDEV,) + (a.shape if g else a.shape[1:]), a.dtype)
                          for a, g in zip(self.ins, self.gather)]
        self.scratch = [pltpu.SemaphoreType.DMA((7 * self.nt,)), pltpu.SemaphoreType.DMA((7 * self.nt,)),
                        pltpu.SemaphoreType.DMA((self.nt,))]

    def _copies(self, src, dst, sems, *kinds):
        send_sems, recv_sems, local_sems = sems
        x, y, c = lax.axis_index("x"), lax.axis_index("y"), lax.axis_index("c")
        me, sib = (x, y, c), (x, y, 1 - c)
        chips = [(1 - x, y), (x, 1 - y), (1 - x, 1 - y)]
        p = {kind: [] for kind in kinds}
        for t in range(self.nt):
            part = (slice(None),) if self.rows[t] is None else (pl.ds(*self.rows[t]),)

            def slot(d, t=t, part=part):
                return dst[t].at[(4 * d[0] + 2 * d[1] + d[2],) + part]

            def add(kind, k, a, b, to, t=t):
                if kind in p:
                    p[kind].append(pltpu.make_async_remote_copy(
                        src_ref=a, dst_ref=b, send_sem=send_sems.at[t * 7 + k],
                        recv_sem=recv_sems.at[t * 7 + k], device_id=to, device_id_type=MESH))

            own = src[t] if self.gather[t] else src[t].at[(4 * x + 2 * y + c,) + part]
            if "local" in p:
                p["local"].append(pltpu.make_async_copy(own, slot(me), local_sems.at[t]))
            if self.gather[t]:
                add("first", 0, own, slot(me), sib)
                add("first_in", 0, own, slot(sib), sib)
                for j, chip in enumerate(chips):
                    there = (*chip, c)
                    add("first", 1 + j, own, slot(me), there)
                    add("chip_in", 1 + j, own, slot(there), there)
                    add("relay", 4 + j, slot(there), slot(there), sib)
                    add("relay_in", 4 + j, own, slot((*chip, 1 - c)), sib)
            else:
                for k in range(NDEV - 1):
                    bits = k + 1
                    peer = (1 - x if bits & 4 else x, 1 - y if bits & 2 else y, 1 - c if bits & 1 else c)
                    out = src[t].at[(4 * peer[0] + 2 * peer[1] + peer[2],) + part]
                    add("first", k, out, slot(me), peer)
                    add("first_in", k, out, slot(peer), peer)
        return [p[kind] for kind in kinds]

    def start(self, src, dst, sems):
        local, first = self._copies(src, dst, sems, "local", "first")
        for cp in local + first:
            cp.start()

    def relay(self, src, dst, sems):
        chip_in, relay = self._copies(src, dst, sems, "chip_in", "relay")
        for arrived, onward in zip(chip_in, relay):
            arrived.wait_recv()
            onward.start()

    def finish(self, src, dst, sems):
        first_in, relay_in, first, relay, local = self._copies(
            src, dst, sems, "first_in", "relay_in", "first", "relay", "local")
        for cp in first_in + relay_in:
            cp.wait_recv()
        for cp in first + relay:
            cp.wait_send()
        for cp in local:
            cp.wait()


def _pcall(name, body, *, grid, in_specs, out_specs, out_shape, scratch_shapes=(), sem, args, comm=None):
    in_specs, out_specs, out_shape = list(in_specs), list(out_specs), list(out_shape)
    scratch_shapes = list(scratch_shapes)
    if comm is None:
        outs = pl.pallas_call(body, grid=grid, in_specs=in_specs, out_specs=out_specs, out_shape=out_shape,
                              scratch_shapes=scratch_shapes, compiler_params=_cp(sem), name=name)(*args)
        return list(outs), []
    ni, no, ns, nt = len(in_specs), len(out_specs), len(scratch_shapes), comm.nt
    steps = math.prod(grid)
    hbm = pl.BlockSpec(memory_space=pltpu.HBM)
    landing = [t for t in range(nt) if comm.into[t] is not None]

    def carrier(*refs):
        ins, cin = refs[:ni], refs[ni:ni + nt]
        refs = refs[:ni + nt] + refs[ni + nt + len(landing):]
        outs, cout = refs[ni + nt:ni + nt + no], refs[ni + nt + no:ni + 2 * nt + no]
        scr, sems = refs[ni + 2 * nt + no:ni + 2 * nt + no + ns], refs[ni + 2 * nt + no + ns:]
        step = functools.reduce(lambda acc, d: acc * grid[d] + pl.program_id(d), range(len(grid)), 0)

        @pl.when(step == 0)
        def _():
            comm.start(cin, cout, sems)

        body(*ins, *outs, *scr)

        @pl.when(step == (7 * steps) // 8)
        def _():
            comm.relay(cin, cout, sems)

        @pl.when(step == steps - 1)
        def _():
            comm.finish(cin, cout, sems)

    outs = pl.pallas_call(
        carrier, grid=grid, in_specs=in_specs + [hbm] * (nt + len(landing)), out_specs=out_specs + [hbm] * nt,
        out_shape=out_shape + comm.out_shape, scratch_shapes=scratch_shapes + comm.scratch,
        input_output_aliases={ni + nt + a: no + t for a, t in enumerate(landing)},
        compiler_params=_cp(("arbitrary",) * len(grid)), name=name)(*args, *comm.ins, *[comm.into[t] for t in landing])
    return list(outs[:no]), list(outs[no:])


def exchange(name, ins, gather):
    comm = Comm(ins, gather)
    hbm = pl.BlockSpec(memory_space=pltpu.HBM)

    def body(*refs):
        src, dst, sems = refs[:comm.nt], refs[comm.nt:2 * comm.nt], refs[2 * comm.nt:]
        comm.start(src, dst, sems)
        comm.relay(src, dst, sems)
        comm.finish(src, dst, sems)

    return pl.pallas_call(body, in_specs=[hbm] * comm.nt, out_specs=[hbm] * comm.nt, out_shape=comm.out_shape,
                          scratch_shapes=comm.scratch, name=name)(*ins)


def _matmul(name, a, b, extras, *, grid, a_spec, b_spec, extra_specs, out_shapes, out_specs,
            contract, epilogue, acc_shape, comm=None):
    nk = grid[2]
    ne, no = len(extras), len(out_shapes)

    def body(a_ref, b_ref, *rest):
        ex, outs, acc = rest[:ne], rest[ne:ne + no], rest[ne + no]
        kk = pl.program_id(2)

        @pl.when(kk == 0)
        def _():
            acc[...] = jnp.zeros_like(acc)

        acc[...] += lax.dot_general(a_ref[...].astype(BF16), b_ref[...].astype(BF16),
                                    (contract, ((), ())), preferred_element_type=F32)

        @pl.when(kk == nk - 1)
        def _():
            res = epilogue(acc[...], *[r[...] for r in ex])
            for r, o in zip(outs, res):
                r[...] = o.astype(r.dtype)

    return _pcall(name, body, grid=grid, in_specs=[a_spec, b_spec, *extra_specs], out_specs=out_specs,
                  out_shape=out_shapes, scratch_shapes=[pltpu.VMEM(acc_shape, F32)],
                  sem=("parallel", "parallel", "arbitrary"), args=(a, b, *extras), comm=comm)


def _ident(acc):
    return (acc,)


def _fit(tile, dim):
    return max(t for t in range(128, min(tile, dim) + 1, 128) if dim % t == 0)


def mm_nn(name, a, w, *, out_dtype, tm=1024, tn=1024, tk=2048, extras=(), epilogue=_ident, n_out=1,
          out_dtypes=None, comm=None):
    m, k = a.shape
    tm, tk = _fit(tm, m), _fit(tk, k)
    if w.ndim == 3:
        tn = w.shape[2]
        n = NDEV * tn
        b_spec = pl.BlockSpec((None, tk, tn), lambda i, j, kk: (j, kk, 0))
    else:
        n = w.shape[1]
        tn = _fit(tn, n)
        b_spec = pl.BlockSpec((tk, tn), lambda i, j, kk: (kk, j))
    out_dtypes = out_dtypes or (out_dtype,) * n_out
    mn = pl.BlockSpec((tm, tn), lambda i, j, kk: (i, j))
    return _matmul(name, a, w, extras, grid=(m // tm, n // tn, k // tk),
                   a_spec=pl.BlockSpec((tm, tk), lambda i, j, kk: (i, kk)), b_spec=b_spec,
                   extra_specs=[mn] * len(extras),
                   out_shapes=[jax.ShapeDtypeStruct((m, n), dt) for dt in out_dtypes],
                   out_specs=[mn] * len(out_dtypes), contract=((1,), (0,)), epilogue=epilogue,
                   acc_shape=(tm, tn), comm=comm)


def mm_nt(name, a, w, *, out_dtype, tm=1024, tn=1024, tk=2048, extras=(), epilogue=_ident, comm=None):
    m, n = a.shape
    k = w.shape[-2]
    tm, to = _fit(tm, m), _fit(tn, k)
    if w.ndim == 3:
        tc = w.shape[2]
        b_spec = pl.BlockSpec((None, to, tc), lambda i, j, kk: (kk, j, 0))
    else:
        tc = _fit(tk, n)
        b_spec = pl.BlockSpec((to, tc), lambda i, j, kk: (j, kk))
    mo = pl.BlockSpec((tm, to), lambda i, j, kk: (i, j))
    outs, got = _matmul(name, a, w, extras, grid=(m // tm, k // to, n // tc),
                        a_spec=pl.BlockSpec((tm, tc), lambda i, j, kk: (i, kk)), b_spec=b_spec,
                        extra_specs=[mo] * len(extras),
                        out_shapes=[jax.ShapeDtypeStruct((m, k), out_dtype)],
                        out_specs=[mo], contract=((1,), (1,)), epilogue=epilogue, acc_shape=(tm, to),
                        comm=comm)
    return outs[0], got


def mm_tn(name, a, b, *, out_dtype=BF16, tm=1024, tn=1024, tk=2048, col_shards=False, comm=None):
    t, k = a.shape
    n = b.shape[1]
    tm, tk = _fit(tm, k), _fit(tk, t)
    if col_shards:
        tn = n // NDEV
        shape, spec = (NDEV, k, tn), pl.BlockSpec((None, tm, tn), lambda i, j, kk: (j, i, 0))
    else:
        tn = _fit(tn, n)
        shape, spec = (k, n), pl.BlockSpec((tm, tn), lambda i, j, kk: (i, j))
    outs, got = _matmul(name, a, b, (), grid=(k // tm, n // tn, t // tk),
                        a_spec=pl.BlockSpec((tk, tm), lambda i, j, kk: (kk, i)),
                        b_spec=pl.BlockSpec((tk, tn), lambda i, j, kk: (kk, j)),
                        extra_specs=[], out_shapes=[jax.ShapeDtypeStruct(shape, out_dtype)],
                        out_specs=[spec],
                        contract=((0,), (0,)), epilogue=_ident, acc_shape=(tm, tn), comm=comm)
    return outs[0], got


ROWS = 256


def rmsnorm_fwd(name, x, gain):
    def body(x_ref, g_ref, h_ref):
        xf = x_ref[...]
        rstd = lax.rsqrt(jnp.mean(xf * xf, axis=-1, keepdims=True) + RMS_EPS)
        h_ref[...] = (xf * rstd * g_ref[...]).astype(BF16)

    row = pl.BlockSpec((ROWS, D), lambda i: (i, 0))
    return pl.pallas_call(body, grid=(T // ROWS,), in_specs=[row, pl.BlockSpec((1, D), lambda i: (0, 0))],
                          out_specs=row, out_shape=jax.ShapeDtypeStruct((T, D), BF16),
                          compiler_params=_cp(("parallel",)), name=name)(x, gain)


def _rms_bwd_math(dh, xf, gain):
    rstd = lax.rsqrt(jnp.mean(xf * xf, axis=-1, keepdims=True) + RMS_EPS)
    xhat = xf * rstd
    dgain = jnp.sum(dh * xhat, axis=0, keepdims=True)
    dxhat = dh * gain
    dx = rstd * (dxhat - xhat * jnp.mean(dxhat * xhat, axis=-1, keepdims=True))
    return dx, dgain


def rmsnorm_bwd(name, dh, x, gain, dres):
    def body(dh_ref, x_ref, g_ref, dres_ref, dx_ref, dxb_ref, dg_ref):
        dx, dgain = _rms_bwd_math(dh_ref[...], x_ref[...], g_ref[...])
        dx = dx + dres_ref[...]
        dx_ref[...] = dx
        dxb_ref[...] = dx.astype(BF16)

        @pl.when(pl.program_id(0) == 0)
        def _():
            dg_ref[...] = jnp.zeros_like(dg_ref)

        dg_ref[...] += dgain

    row = pl.BlockSpec((ROWS, D), lambda i: (i, 0))
    vec = pl.BlockSpec((1, D), lambda i: (0, 0))
    return pl.pallas_call(
        body, grid=(T // ROWS,), in_specs=[row, row, vec, row], out_specs=[row, row, vec],
        out_shape=[jax.ShapeDtypeStruct((T, D), F32), jax.ShapeDtypeStruct((T, D), BF16),
                   jax.ShapeDtypeStruct((1, D), F32)],
        compiler_params=_cp(("arbitrary",)), name=name)(dh, x, gain, dres)


def loss_head(name, x, gain, target):
    def body(x_ref, g_ref, t_ref, dx_ref, dxb_ref, dg_ref, loss_ref):
        xf, gain = x_ref[...], g_ref[...]
        rstd = lax.rsqrt(jnp.mean(xf * xf, axis=-1, keepdims=True) + RMS_EPS)
        err = xf * rstd * gain - t_ref[...]
        part = 0.5 * jnp.sum(jnp.mean(err * err, axis=-1, keepdims=True), axis=0, keepdims=True)
        dx, dgain = _rms_bwd_math(err * (1.0 / D), xf, gain)
        dx_ref[...] = dx
        dxb_ref[...] = dx.astype(BF16)

        @pl.when(pl.program_id(0) == 0)
        def _():
            dg_ref[...] = jnp.zeros_like(dg_ref)
            loss_ref[...] = jnp.zeros_like(loss_ref)

        dg_ref[...] += dgain
        loss_ref[...] += jnp.broadcast_to(part, loss_ref.shape)

    row = pl.BlockSpec((ROWS, D), lambda i: (i, 0))
    vec = pl.BlockSpec((1, D), lambda i: (0, 0))
    return pl.pallas_call(
        body, grid=(T // ROWS,), in_specs=[row, vec, row],
        out_specs=[row, row, vec, pl.BlockSpec((1, 128), lambda i: (0, 0))],
        out_shape=[jax.ShapeDtypeStruct((T, D), F32), jax.ShapeDtypeStruct((T, D), BF16),
                   jax.ShapeDtypeStruct((1, D), F32), jax.ShapeDtypeStruct((1, 128), F32)],
        compiler_params=_cp(("arbitrary",)), name=name)(x, gain, target)


def _a_common(n, g, s_ref, q_ref, kc_ref, kp_ref):
    row = lax.broadcasted_iota(jnp.int32, (GROUP_A * BLK, BLK), 0)
    col = lax.broadcasted_iota(jnp.int32, (GROUP_A * BLK, BLK), 1)
    qi = row & (BLK - 1)
    dist_c = (qi - col).astype(F32)
    valid_c = qi >= col
    valid_p = jnp.logical_and(col > qi, n > 0)
    hidx = lax.broadcasted_iota(jnp.int32, (GROUP_A * BLK, 1), 0) >> 7
    slope = jnp.exp((-math.log(2.0) / 4.0) * (hidx + (GROUP_A * g + 1)).astype(F32))
    sink = jnp.zeros((GROUP_A * BLK, 1), F32)
    for i in range(GROUP_A):
        sink = jnp.where(hidx == i, s_ref[GROUP_A * g + i], sink)
    qg = jnp.concatenate([q_ref[:, (GROUP_A * g + i) * HD:(GROUP_A * g + i + 1) * HD]
                          for i in range(GROUP_A)], axis=0)
    kc = kc_ref[:, g * HD:(g + 1) * HD]
    kp = kp_ref[:, g * HD:(g + 1) * HD]
    nt = (((1,), (1,)), ((), ()))
    sc = lax.dot_general(qg, kc, nt, preferred_element_type=F32) * SCALE - slope * dist_c
    sp = lax.dot_general(qg, kp, nt, preferred_element_type=F32) * SCALE - slope * (dist_c + float(BLK))
    sc = jnp.where(valid_c, sc, NEG)
    sp = jnp.where(valid_p, sp, NEG)
    m = jnp.maximum(jnp.maximum(jnp.max(sc, axis=-1, keepdims=True), jnp.max(sp, axis=-1, keepdims=True)), sink)
    pc = jnp.exp(sc - m)
    pp = jnp.exp(sp - m)
    ps = jnp.exp(sink - m)
    den = jnp.sum(pc, axis=-1, keepdims=True) + jnp.sum(pp, axis=-1, keepdims=True) + ps
    return qg, kc, kp, pc, pp, ps, den, hidx


def attn_a_fwd(name, qkv, sinks, comm=None):
    kcol, vcol = NQ_A * HD // (NKV_A * HD), NQ_A * HD // (NKV_A * HD) + 1

    def body(s_ref, q_ref, kc_ref, kp_ref, vc_ref, vp_ref, o_ref):
        n = pl.program_id(0)
        for g in range(NKV_A):
            qg, kc, kp, pc, pp, ps, den, _ = _a_common(n, g, s_ref, q_ref, kc_ref, kp_ref)
            vc = vc_ref[:, g * HD:(g + 1) * HD]
            vp = vp_ref[:, g * HD:(g + 1) * HD]
            og = (jnp.dot(pc.astype(BF16), vc, preferred_element_type=F32)
                  + jnp.dot(pp.astype(BF16), vp, preferred_element_type=F32)) / den
            for i in range(GROUP_A):
                h = GROUP_A * g + i
                o_ref[:, h * HD:(h + 1) * HD] = og[i * BLK:(i + 1) * BLK].astype(BF16)

    kvw = NKV_A * HD
    prev = lambda n: jnp.maximum(n - 1, 0)
    outs, got = _pcall(
        name, body, grid=(T // BLK,),
        in_specs=[pl.BlockSpec(memory_space=pltpu.SMEM),
                  pl.BlockSpec((BLK, NQ_A * HD), lambda n: (n, 0)),
                  pl.BlockSpec((BLK, kvw), lambda n: (n, kcol)),
                  pl.BlockSpec((BLK, kvw), lambda n: (prev(n), kcol)),
                  pl.BlockSpec((BLK, kvw), lambda n: (n, vcol)),
                  pl.BlockSpec((BLK, kvw), lambda n: (prev(n), vcol))],
        out_specs=[pl.BlockSpec((BLK, NQ_A * HD), lambda n: (n, 0))],
        out_shape=[jax.ShapeDtypeStruct((T, NQ_A * HD), BF16)],
        sem=("parallel",), args=(sinks, qkv, qkv, qkv, qkv, qkv), comm=comm)
    return outs[0], got


def attn_a_bwd(name, qkv, sinks, o, do, comm=None):
    kcol, vcol = NQ_A * HD // (NKV_A * HD), NQ_A * HD // (NKV_A * HD) + 1
    nb = T // BLK
    kvw = NKV_A * HD

    def body(s_ref, q_ref, kc_ref, kp_ref, vc_ref, vp_ref, o_ref, do_ref,
             dq_ref, dk_ref, dv_ref, ds_ref, ck_ref, cv_ref):
        n = pl.program_id(0)

        @pl.when(n == 0)
        def _():
            ds_ref[...] = jnp.zeros_like(ds_ref)
            ck_ref[...] = jnp.zeros_like(ck_ref)
            cv_ref[...] = jnp.zeros_like(cv_ref)

        @pl.when(n == nb)
        def _():
            dk_ref[...] = ck_ref[...].astype(BF16)
            dv_ref[...] = cv_ref[...].astype(BF16)

        @pl.when(n < nb)
        def _():
            lane = lax.broadcasted_iota(jnp.int32, (1, 128), 1)
            dsink_row = jnp.zeros((1, 128), F32)
            tn = (((0,), (0,)), ((), ()))
            nt = (((1,), (1,)), ((), ()))
            for g in range(NKV_A):
                qg, kc, kp, pc, pp, ps, den, hidx = _a_common(n, g, s_ref, q_ref, kc_ref, kp_ref)
                vc = vc_ref[:, g * HD:(g + 1) * HD]
                vp = vp_ref[:, g * HD:(g + 1) * HD]
                cols = [slice((GROUP_A * g + i) * HD, (GROUP_A * g + i + 1) * HD) for i in range(GROUP_A)]
                og = jnp.concatenate([o_ref[:, c] for c in cols], axis=0)
                dog = jnp.concatenate([do_ref[:, c] for c in cols], axis=0)
                delta = jnp.sum(og.astype(F32) * dog.astype(F32), axis=-1, keepdims=True)
                inv = 1.0 / den
                p_c, p_p = pc * inv, pp * inv
                dsc = p_c * (lax.dot_general(dog, vc, nt, preferred_element_type=F32) - delta)
                dsp = p_p * (lax.dot_general(dog, vp, nt, preferred_element_type=F32) - delta)
                dsk = -(ps * inv) * delta
                for i in range(GROUP_A):
                    tot = jnp.sum(jnp.where(hidx == i, dsk, 0.0), axis=0, keepdims=True)
                    dsink_row = dsink_row + jnp.where(lane == GROUP_A * g + i, tot, 0.0)
                dscb, dspb = (dsc * SCALE).astype(BF16), (dsp * SCALE).astype(BF16)
                dqg = (jnp.dot(dscb, kc, preferred_element_type=F32)
                       + jnp.dot(dspb, kp, preferred_element_type=F32))
                for i in range(GROUP_A):
                    dq_ref[:, cols[i]] = dqg[i * BLK:(i + 1) * BLK].astype(BF16)
                gs = slice(g * HD, (g + 1) * HD)
                dk_ref[:, gs] = (ck_ref[:, gs] + lax.dot_general(dspb, qg, tn, preferred_element_type=F32)).astype(BF16)
                dv_ref[:, gs] = (cv_ref[:, gs] + lax.dot_general(p_p.astype(BF16), dog, tn, preferred_element_type=F32)).astype(BF16)
                ck_ref[:, gs] = lax.dot_general(dscb, qg, tn, preferred_element_type=F32)
                cv_ref[:, gs] = lax.dot_general(p_c.astype(BF16), dog, tn, preferred_element_type=F32)
            ds_ref[...] += jnp.broadcast_to(dsink_row, ds_ref.shape)

    cur = lambda n: jnp.minimum(n, nb - 1)
    prev = lambda n: jnp.maximum(jnp.minimum(n, nb - 1) - 1, 0)
    outp = lambda n: jnp.maximum(n - 1, 0)
    qspec = pl.BlockSpec((BLK, NQ_A * HD), lambda n: (cur(n), 0))
    return _pcall(
        name, body, grid=(nb + 1,),
        in_specs=[pl.BlockSpec(memory_space=pltpu.SMEM), qspec,
                  pl.BlockSpec((BLK, kvw), lambda n: (cur(n), kcol)),
                  pl.BlockSpec((BLK, kvw), lambda n: (prev(n), kcol)),
                  pl.BlockSpec((BLK, kvw), lambda n: (cur(n), vcol)),
                  pl.BlockSpec((BLK, kvw), lambda n: (prev(n), vcol)),
                  qspec, qspec],
        out_specs=[qspec, pl.BlockSpec((BLK, kvw), lambda n: (outp(n), 0)),
                   pl.BlockSpec((BLK, kvw), lambda n: (outp(n), 0)),
                   pl.BlockSpec((8, 128), lambda n: (0, 0))],
        out_shape=[jax.ShapeDtypeStruct((T, NQ_A * HD), BF16), jax.ShapeDtypeStruct((T, kvw), BF16),
                   jax.ShapeDtypeStruct((T, kvw), BF16), jax.ShapeDtypeStruct((8, 128), F32)],
        scratch_shapes=[pltpu.VMEM((BLK, kvw), F32), pltpu.VMEM((BLK, kvw), F32)],
        sem=("arbitrary",), args=(sinks, qkv, qkv, qkv, qkv, qkv, o, do), comm=comm)


def _b_logs(z):
    lb = jnp.minimum(z, 0.0) - jnp.log(1.0 + jnp.exp(-jnp.abs(z)))
    return lb, lb - z


def _cumsum_excl(xs, later):
    r = lax.broadcasted_iota(jnp.int32, (BLK, BLK), 0)
    c = lax.broadcasted_iota(jnp.int32, (BLK, BLK), 1)
    tri = (r > c if later else r < c).astype(BF16)
    blocks = [slice(b * BLK, (b + 1) * BLK) for b in range(B_UNROLL)]
    pieces = []
    for x in xs:
        hi = x.astype(BF16)
        lo = (x - hi.astype(F32)).astype(BF16)
        pieces += [hi[:, b] for b in blocks] + [lo[:, b] for b in blocks]
    inside = jnp.dot(jnp.concatenate(pieces, axis=0), tri, preferred_element_type=F32)
    piece = lambda i: inside[i * BLK:(i + 1) * BLK]
    outs, totals = [], []
    for h, x in enumerate(xs):
        base = 2 * B_UNROLL * h
        sums = [jnp.sum(x[:, b], axis=-1, keepdims=True) for b in blocks]
        cols = [functools.reduce(jnp.add, sums[i + 1:] if later else sums[:i],
                                 piece(base + i) + piece(base + B_UNROLL + i)) for i in range(B_UNROLL)]
        outs.append(jnp.concatenate(cols, axis=1))
        totals.append(functools.reduce(jnp.add, sums))
    return outs, totals


B_CUT = 105.0
B_NH = 4
B_HEADS = tuple(slice(h * HD, (h + 1) * HD) for h in range(B_NH))
B_UNROLL = 3
B_WIN = B_UNROLL * BLK
B_GROUPS = -(-(T // BLK) // B_UNROLL)


def _b_window(j0, n):
    first = jnp.maximum(j0 - (B_UNROLL - 1), 0) * BLK
    row = lax.broadcasted_iota(jnp.int32, (BLK, B_WIN), 0)
    col = lax.broadcasted_iota(jnp.int32, (BLK, B_WIN), 1)
    valid = jnp.logical_and(col - row < n * BLK - first, col < (j0 + 1) * BLK - first)
    return pl.ds(pl.multiple_of(first, BLK), B_WIN), valid


def _b_weights(qs, kws, valid, cs):
    nt = (((1,), (1,)), ((), ()))
    zs = [lax.dot_general(q, kw, nt, preferred_element_type=F32) * SCALE for q, kw in zip(qs, kws)]
    logs = [_b_logs(z) for z in zs]
    lms = [jnp.where(valid, lm, 0.0) for _, lm in logs]
    afters, totals = _cumsum_excl(lms, later=True)
    weights = [jnp.where(valid, jnp.exp(lb + c + after), 0.0) for (lb, _), c, after in zip(logs, cs, afters)]
    return [lb for lb, _ in logs], weights, [c + t for c, t in zip(cs, totals)]


def _b_live(cs):
    return functools.reduce(jnp.maximum, [jnp.max(c) for c in cs]) > -B_CUT


def attn_b_fwd(name, qkv, comm=None):
    npair = NH_B // B_NH

    def body(q_ref, k_ref, v_ref, o_ref):
        n = pl.program_id(1)
        qs = [q_ref[:, hs] for hs in B_HEADS]

        def group(j0, st):
            rows, valid = _b_window(j0, n)
            _, weights, cs = _b_weights(qs, [k_ref[rows, hs] for hs in B_HEADS], valid, [c for c, _ in st])
            accs = [acc + jnp.dot(a.astype(BF16), v_ref[rows, hs], preferred_element_type=F32)
                    for (_, acc), a, hs in zip(st, weights, B_HEADS)]
            return tuple(zip(cs, accs))

        zero = (jnp.zeros((BLK, 1), F32), jnp.zeros((BLK, HD), F32))
        _, st = lax.while_loop(lambda cr: jnp.logical_and(cr[0] >= 0, _b_live([c for c, _ in cr[1]])),
                               lambda cr: (cr[0] - B_UNROLL, group(cr[0], cr[1])), (n, (zero,) * B_NH))
        for h, hs in enumerate(B_HEADS):
            o_ref[:, hs] = st[h][1].astype(BF16)

    outs, got = _pcall(
        name, body, grid=(npair, T // BLK),
        in_specs=[pl.BlockSpec((BLK, B_NH * HD), lambda p, n: (n, p)),
                  pl.BlockSpec((T, B_NH * HD), lambda p, n: (0, npair + p)),
                  pl.BlockSpec((T, B_NH * HD), lambda p, n: (0, 2 * npair + p))],
        out_specs=[pl.BlockSpec((BLK, B_NH * HD), lambda p, n: (n, p))],
        out_shape=[jax.ShapeDtypeStruct((T, NH_B * HD), BF16)],
        sem=("parallel", "arbitrary"), args=(qkv, qkv, qkv), comm=comm)
    return outs[0], got


def attn_b_bwd(name, qkv, do, comm=None):
    npair = NH_B // B_NH
    nb = T // BLK

    def body(q_ref, do_ref, k_ref, v_ref, dq_ref, dk_ref, dv_ref, a_s, lb_s):
        n = pl.program_id(1)

        @pl.when(n == 0)
        def _():
            dk_ref[...] = jnp.zeros_like(dk_ref)
            dv_ref[...] = jnp.zeros_like(dv_ref)

        nt = (((1,), (1,)), ((), ()))
        tn = (((0,), (0,)), ((), ()))
        qs = [q_ref[:, hs] for hs in B_HEADS]
        dos = [do_ref[:, hs] for hs in B_HEADS]

        def sweep1(gi, j0, cs):
            rows, valid = _b_window(j0, n)
            lbs, weights, cs = _b_weights(qs, [k_ref[rows, hs] for hs in B_HEADS], valid, cs)
            for h in range(B_NH):
                a_s[h, gi] = weights[h]
                lb_s[h, gi] = lbs[h]
            return tuple(cs)

        _, groups, _ = lax.while_loop(
            lambda cr: jnp.logical_and(cr[0] >= 0, _b_live(cr[2])),
            lambda cr: (cr[0] - B_UNROLL, cr[1] + 1, sweep1(cr[1], cr[0], cr[2])),
            (n, 0, (jnp.zeros((BLK, 1), F32),) * B_NH))

        def sweep2(i, st):
            gi = groups - 1 - i
            rows, valid = _b_window(n - gi * B_UNROLL, n)
            kws, vws = [k_ref[rows, hs] for hs in B_HEADS], [v_ref[rows, hs] for hs in B_HEADS]
            weights, lbs = [a_s[h, gi] for h in range(B_NH)], [lb_s[h, gi] for h in range(B_NH)]
            gs = [a * lax.dot_general(do, vw, nt, preferred_element_type=F32)
                  for a, do, vw in zip(weights, dos, vws)]
            befores, totals = _cumsum_excl(gs, later=False)
            dzbs = []
            for (cg, _), g, before, lb in zip(st, gs, befores, lbs):
                dz = jnp.where(valid, g - (g + cg + before) * jnp.exp(lb), 0.0)
                dzbs.append((dz * SCALE).astype(BF16))
            for h, hs in enumerate(B_HEADS):
                dk_ref[rows, hs] += lax.dot_general(dzbs[h], qs[h], tn, preferred_element_type=F32)
                dv_ref[rows, hs] += lax.dot_general(weights[h].astype(BF16), dos[h], tn, preferred_element_type=F32)
            return tuple((cg + total, dq + jnp.dot(dzb, kw, preferred_element_type=F32))
                         for (cg, dq), total, dzb, kw in zip(st, totals, dzbs, kws))

        zero = (jnp.zeros((BLK, 1), F32), jnp.zeros((BLK, HD), F32))
        st = lax.fori_loop(0, groups, sweep2, (zero,) * B_NH)
        for h, hs in enumerate(B_HEADS):
            dq_ref[:, hs] = st[h][1].astype(BF16)

    tile = pl.BlockSpec((BLK, B_NH * HD), lambda p, n: (n, p))
    full = lambda off: pl.BlockSpec((T, B_NH * HD), lambda p, n: (0, off + p))
    return _pcall(
        name, body, grid=(npair, nb),
        in_specs=[tile, tile, full(npair), full(2 * npair)],
        out_specs=[tile, full(0), full(0)],
        out_shape=[jax.ShapeDtypeStruct((T, NH_B * HD), BF16), jax.ShapeDtypeStruct((T, NH_B * HD), F32),
                   jax.ShapeDtypeStruct((T, NH_B * HD), F32)],
        scratch_shapes=[pltpu.VMEM((B_NH, B_GROUPS, BLK, B_WIN), F32)] * 2,
        sem=("parallel", "arbitrary"), args=(qkv, do, qkv, qkv), comm=comm)


def _adamw_math(w, g, m, v):
    m = B1 * m + (1.0 - B1) * g
    v = B2 * v + (1.0 - B2) * (g * g)
    m_hat = m / (1.0 - B1 ** STEP)
    v_hat = v / (1.0 - B2 ** STEP)
    delta = -LR * (m_hat / (jnp.sqrt(v_hat) + EPS) + WD * w)
    return delta, m, v


def adamw(name, parts, w, m, v, rows, layer=0, prev=None):
    nl, r, c = w.shape

    def body(p_ref, w_ref, m_ref, v_ref, *rest):
        g_ref, d_ref, nm_ref, nv_ref = rest[-4:]
        g = p_ref[0].astype(F32)
        for s in range(1, NDEV):
            g = g + p_ref[s].astype(F32)
        delta, nm, nv = _adamw_math(w_ref[...], g, m_ref[...], v_ref[...])
        g_ref[...] = g
        d_ref[...] = delta
        nm_ref[...] = nm
        nv_ref[...] = nv

    blk = pl.BlockSpec((None, rows, c), lambda i: (layer, i, 0))
    carried = [] if prev is None else list(prev)
    return pl.pallas_call(
        body, grid=(r // rows,),
        in_specs=[pl.BlockSpec((NDEV, rows, c), lambda i: (0, i, 0)), blk, blk, blk]
        + [pl.BlockSpec(memory_space=pl.ANY)] * len(carried),
        out_specs=[blk] * 4, out_shape=[jax.ShapeDtypeStruct((nl, r, c), F32)] * 4,
        input_output_aliases={4 + q: q for q in range(len(carried))},
        compiler_params=_cp(("parallel",)), name=name)(parts, w, m, v, *carried)


def _unshard_cols(g):
    return jnp.transpose(g, (1, 0, 2)).reshape(g.shape[1], -1)


def _shard_cols(w):
    k, n = w.shape
    return jnp.transpose(w.reshape(k, NDEV, n // NDEV), (1, 0, 2))


FWD_CARRY = {
    ("qkv", 0): [(0, 1)], ("attn", 0): [(0, 2)], ("mlp_in", 0): [(0, 3)], ("mlp_out", 0): [(1, 0)],
    ("qkv", 1): [(1, 1)], ("attn", 1): [(1, 2), (1, 3), (2, 0), (2, 1)],
    ("mlp_in", 1): [(2, 2)], ("mlp_out", 1): [(2, 3)],
    ("qkv", 2): [(3, 1)], ("attn", 2): [(3, 0)], ("mlp_in", 2): [(3, 2)], ("mlp_out", 2): [(3, 3)],
}
BWD_CARRY = {
    ("attn_bwd", 3): [(3, 3), (3, 2), (3, 1)],
    ("g_mlp_out", 2): [(3, 0, 0)], ("d_act", 2): [(3, 0, 1)], ("g_mlp_in", 2): [(2, 3, 0)],
    ("d_h2", 2): [(2, 3, 1)], ("attn_bwd", 2): [(2, 2)], ("d_h1", 2): [(2, 1)],
    ("g_mlp_out", 1): [(2, 0)], ("attn_bwd", 1): [(1, 3), (1, 2), (1, 1)],
    ("g_mlp_out", 0): [(1, 0, 0)], ("d_act", 0): [(1, 0, 1)], ("g_mlp_in", 0): [(0, 3, 0)],
    ("d_h2", 0): [(0, 3, 1)], ("attn_bwd", 0): [(0, 2)], ("g_qkv", 0): [(0, 1)], ("d_h1", 0): [(0, 0)],
}


def kernel(x, a_w_qkv, a_w_o, a_sinks, b_w_qkv, b_w_o, norm_mix, norm_mlp, mlp_w_in, mlp_w_out, final_norm, loss_target, m_a_w_qkv, m_a_w_o, m_a_sinks, m_b_w_qkv, m_b_w_o, m_norm_mix, m_norm_mlp, m_mlp_w_in, m_mlp_w_out, m_final_norm, v_a_w_qkv, v_a_w_o, v_a_sinks, v_b_w_qkv, v_b_w_o, v_norm_mix, v_norm_mlp, v_mlp_w_in, v_mlp_w_out, v_final_norm):
    depth = 4
    xs = x[0]

    shards, full, sendbuf, recv = {}, {}, {}, {}
    for i in range(depth):
        j = i // 2
        wq, wo = (a_w_qkv[j], a_w_o[j]) if i % 2 == 0 else (b_w_qkv[j], b_w_o[j])
        for t, w in enumerate((wq, wo, mlp_w_in[i], mlp_w_out[i])):
            shards[(i, t)] = w.astype(BF16)

    def land_weights(keys, slabs):
        for (i, t), g in zip(keys, slabs):
            if t in (1, 3):
                full[(i, t)] = g.reshape(-1, D)
            else:
                full[(i, t)] = g if g.shape[2] % 128 == 0 else _unshard_cols(g)

    def fwd(kind, i, fn):
        keys = FWD_CARRY.get((kind, i), [])
        out, got = fn(Comm([shards[k] for k in keys], True) if keys else None)
        land_weights(keys, got)
        return out

    def bwd(kind, i, fn):
        keys = BWD_CARRY.get((kind, i), [])
        bufs = [sendbuf[key[:2]] for key in keys]
        halves = [None if len(key) == 2 else (key[2] * (b.shape[1] // 2), b.shape[1] // 2) for key, b in zip(keys, bufs)]
        out, got = fn(Comm(bufs, False, halves, [recv.get(key[:2]) for key in keys]) if keys else None)
        recv.update((key[:2], g) for key, g in zip(keys, got))
        return out

    relu2 = lambda acc: (acc, jnp.square(jnp.maximum(acc, 0.0)))
    add = lambda acc, res: (acc + res,)
    drelu2 = lambda acc, uu: (acc * (2.0 * jnp.maximum(uu.astype(F32), 0.0)),)

    land_weights([(0, 0)], exchange("gather_first", [shards[(0, 0)]], True))
    saved = []
    for i in range(depth):
        x1 = xs
        h1 = rmsnorm_fwd(f"norm_mix{i}", x1, norm_mix[i][None])
        (qkv,) = fwd("qkv", i, lambda c: mm_nn(f"qkv{i}", h1, full[(i, 0)], out_dtype=BF16, tn=512, comm=c))
        if i % 2 == 0:
            o = fwd("attn", i, lambda c: attn_a_fwd(f"attn_a{i}", qkv, a_sinks[i // 2], comm=c))
        else:
            o = fwd("attn", i, lambda c: attn_b_fwd(f"attn_b{i}", qkv, comm=c))
        (x2,) = mm_nn(f"proj{i}", o, full[(i, 1)], out_dtype=F32, extras=(x1,), epilogue=add)[0]
        h2 = rmsnorm_fwd(f"norm_mlp{i}", x2, norm_mlp[i][None])
        u, act = fwd("mlp_in", i, lambda c: mm_nn(f"mlp_in{i}", h2, full[(i, 2)], out_dtype=BF16,
                                                   epilogue=relu2, n_out=2, comm=c))
        (xs,) = fwd("mlp_out", i, lambda c: mm_nn(f"mlp_out{i}", act, full[(i, 3)], out_dtype=F32,
                                                   extras=(x2,), epilogue=add, comm=c))
        saved.append((x1, h1, qkv, o, x2, h2, u, act))

    dx, dxb, d_final, loss_part = loss_head("loss_head", xs, final_norm[None], loss_target[0])
    loss = lax.psum(loss_part[0, 0], ("x", "y", "c"))

    d_mix, d_mlp, d_sinks = [None] * depth, [None] * depth, [None] * 2
    for i in reversed(range(depth)):
        x1, h1, qkv, o, x2, h2, u, act = saved[i]
        sendbuf[(i, 3)] = bwd("g_mlp_out", i, lambda c: mm_tn(f"g_mlp_out{i}", act, dxb, comm=c)).reshape(
            NDEV, DFF // NDEV, D)
        du = bwd("d_act", i, lambda c: mm_nt(f"d_act{i}", dxb, full[(i, 3)], out_dtype=BF16, extras=(u,),
                                             epilogue=drelu2, comm=c))
        sendbuf[(i, 2)] = bwd("g_mlp_in", i, lambda c: mm_tn(f"g_mlp_in{i}", h2, du, col_shards=True, comm=c))
        dh2 = bwd("d_h2", i, lambda c: mm_nt(f"d_h2{i}", du, full[(i, 2)], out_dtype=F32, comm=c))
        dx, dxb, d_mlp[i] = rmsnorm_bwd(f"norm_mlp_bwd{i}", dh2, x2, norm_mlp[i][None], dx)
        sendbuf[(i, 1)] = mm_tn(f"g_proj{i}", o, dxb)[0].reshape(NDEV, D // NDEV, D)
        do = mm_nt(f"d_o{i}", dxb, full[(i, 1)], out_dtype=BF16)[0]
        if i % 2 == 0:
            dq, dk, dv, ds = bwd("attn_bwd", i, lambda c: attn_a_bwd(f"attn_a_bwd{i}", qkv, a_sinks[i // 2],
                                                                       o, do, comm=c))
            d_sinks[i // 2] = ds[0, :NQ_A]
            dqkv = jnp.concatenate([dq, dk, dv], axis=1)
        else:
            dq, dk, dv = bwd("attn_bwd", i, lambda c: attn_b_bwd(f"attn_b_bwd{i}", qkv, do, comm=c))
            dqkv = jnp.concatenate([dq, dk.astype(BF16), dv.astype(BF16)], axis=1)
        if i % 2 == 0:
            sendbuf[(i, 0)] = _shard_cols(bwd("g_qkv", i, lambda c: mm_tn(f"g_qkv{i}", h1, dqkv, tn=512, comm=c)))
        else:
            sendbuf[(i, 0)] = bwd("g_qkv", i, lambda c: mm_tn(f"g_qkv{i}", h1, dqkv, col_shards=True, comm=c))
        dh1 = bwd("d_h1", i, lambda c: mm_nt(f"d_h1{i}", dqkv, full[(i, 0)], out_dtype=F32, comm=c))
        dx, dxb, d_mix[i] = rmsnorm_bwd(f"norm_mix_bwd{i}", dh1, x1, norm_mix[i][None], dx)

    small = jnp.zeros((16, D), F32)
    small = small.at[0:4].set(jnp.concatenate(d_mix, axis=0)).at[4:8].set(jnp.concatenate(d_mlp, axis=0))
    small = small.at[8].set(d_final[0]).at[9, :2 * NQ_A].set(jnp.concatenate(d_sinks))
    (small_parts,) = exchange("exchange_last", [small], True)
    assert len(recv) == 4 * depth and len(full) == 4 * depth
    pack = lambda mix, mlp, fin, snk: (jnp.zeros((16, D), F32).at[0:4].set(mix).at[4:8].set(mlp)
                                       .at[8].set(fin).at[9, :2 * NQ_A].set(snk.reshape(-1)))
    sm = adamw("adamw_small", small_parts, pack(norm_mix, norm_mlp, final_norm, a_sinks)[None],
               pack(m_norm_mix, m_norm_mlp, m_final_norm, m_a_sinks)[None],
               pack(v_norm_mix, v_norm_mlp, v_final_norm, v_a_sinks)[None], rows=16)
    unpack = lambda a: (a[0, 0:4], a[0, 4:8], a[0, 8], a[0, 9, :2 * NQ_A].reshape(2, NQ_A))

    def update(kind, layers, t, w, m, v, rows):
        res = None
        for l, i in enumerate(layers):
            res = adamw(f"adamw_{kind}{i}", recv[(i, t)], w, m, v, rows, layer=l, prev=res)
        return res

    upd = [update("a_qkv", (0, 2), 0, a_w_qkv, m_a_w_qkv, v_a_w_qkv, 256),
           update("a_o", (0, 2), 1, a_w_o, m_a_w_o, v_a_w_o, 128),
           None,
           update("b_qkv", (1, 3), 0, b_w_qkv, m_b_w_qkv, v_b_w_qkv, 256),
           update("b_o", (1, 3), 1, b_w_o, m_b_w_o, v_b_w_o, 128),
           None, None,
           update("mlp_in", range(depth), 2, mlp_w_in, m_mlp_w_in, v_mlp_w_in, 256),
           update("mlp_out", range(depth), 3, mlp_w_out, m_mlp_w_out, v_mlp_w_out, 128),
           None]
    outs = []
    for q in range(4):
        mix, mlp, fin, snk = unpack(sm[q])
        small_out = {2: snk, 5: mix, 6: mlp, 9: fin}
        outs.append([small_out[k] if u is None else u[q] for k, u in enumerate(upd)])
    return (loss, dx[None], *outs[0], *outs[1], *outs[2], *outs[3])
```

```python
import functools
import math

import jax
import jax.numpy as jnp
from jax import lax
from jax.experimental import pallas as pl
from jax.experimental.pallas import tpu as pltpu

F32, BF16 = jnp.float32, jnp.bfloat16
T, D, DFF = 4096, 2048, 8192
HD, BLK = 64, 128
NQ_A, NKV_A, GROUP_A = 32, 4, 8
NH_B = 32
NDEV = 8
RMS_EPS = 1e-5
SCALE = 0.125
NEG = -1e30
VMEM_LIMIT = 56 * 1024 * 1024
LR, B1, B2, EPS, WD, STEP = 0.001, 0.9, 0.999, 1e-08, 0.01, 10
MESH = pl.DeviceIdType.MESH


def _cp(sem, vmem=VMEM_LIMIT):
    return pltpu.CompilerParams(dimension_semantics=sem, vmem_limit_bytes=vmem)


class Comm:
    def __init__(self, ins, gather, rows=None, into=None):
        self.ins, self.nt = list(ins), len(ins)
        self.gather = list(gather) if isinstance(gather, (list, tuple)) else [gather] * self.nt
        self.rows = rows or [None] * self.nt
        self.into = into or [None] * self.nt
        self.out_shape = [jax.ShapeDtypeStruct((NDEV,) + (a.shape if g else a.shape[1:]), a.dtype)
                          for a, g in zip(self.ins, self.gather)]
        self.scratch = [pltpu.SemaphoreType.DMA((7 * self.nt,)), pltpu.SemaphoreType.DMA((7 * self.nt,)),
                        pltpu.SemaphoreType.DMA((self.nt,))]

    def _copies(self, src, dst, sems, *kinds):
        send_sems, recv_sems, local_sems = sems
        x, y, c = lax.axis_index("x"), lax.axis_index("y"), lax.axis_index("c")
        me, sib = (x, y, c), (x, y, 1 - c)
        chips = [(1 - x, y), (x, 1 - y), (1 - x, 1 - y)]
        p = {kind: [] for kind in kinds}
        for t in range(self.nt):
            part = (slice(None),) if self.rows[t] is None else (pl.ds(*self.rows[t]),)

            def slot(d, t=t, part=part):
                return dst[t].at[(4 * d[0] + 2 * d[1] + d[2],) + part]

            def add(kind, k, a, b, to, t=t):
                if kind in p:
                    p[kind].append(pltpu.make_async_remote_copy(
                        src_ref=a, dst_ref=b, send_sem=send_sems.at[t * 7 + k],
                        recv_sem=recv_sems.at[t * 7 + k], device_id=to, device_id_type=MESH))

            own = src[t] if self.gather[t] else src[t].at[(4 * x + 2 * y + c,) + part]
            if "local" in p:
                p["local"].append(pltpu.make_async_copy(own, slot(me), local_sems.at[t]))
            if self.gather[t]:
                add("first", 0, own, slot(me), sib)
                add("first_in", 0, own, slot(sib), sib)
                for j, chip in enumerate(chips):
                    there = (*chip, c)
                    add("first", 1 + j, own, slot(me), there)
                    add("chip_in", 1 + j, own, slot(there), there)
                    add("relay", 4 + j, slot(there), slot(there), sib)
                    add("relay_in", 4 + j, own, slot((*chip, 1 - c)), sib)
            else:
                for k in range(NDEV - 1):
                    bits = k + 1
                    peer = (1 - x if bits & 4 else x, 1 - y if bits & 2 else y, 1 - c if bits & 1 else c)
                    out = src[t].at[(4 * peer[0] + 2 * peer[1] + peer[2],) + part]
                    add("first", k, out, slot(me), peer)
                    add("first_in", k, out, slot(peer), peer)
        return [p[kind] for kind in kinds]

    def start(self, src, dst, sems):
        local, first = self._copies(src, dst, sems, "local", "first")
        for cp in local + first:
            cp.start()

    def relay(self, src, dst, sems):
        chip_in, relay = self._copies(src, dst, sems, "chip_in", "relay")
        for arrived, onward in zip(chip_in, relay):
            arrived.wait_recv()
            onward.start()

    def finish(self, src, dst, sems):
        first_in, relay_in, first, relay, local = self._copies(
            src, dst, sems, "first_in", "relay_in", "first", "relay", "local")
        for cp in first_in + relay_in:
            cp.wait_recv()
        for cp in first + relay:
            cp.wait_send()
        for cp in local:
            cp.wait()


def _pcall(name, body, *, grid, in_specs, out_specs, out_shape, scratch_shapes=(), sem, args, comm=None):
    in_specs, out_specs, out_shape = list(in_specs), list(out_specs), list(out_shape)
    scratch_shapes = list(scratch_shapes)
    if comm is None:
        outs = pl.pallas_call(body, grid=grid, in_specs=in_specs, out_specs=out_specs, out_shape=out_shape,
                              scratch_shapes=scratch_shapes, compiler_params=_cp(sem), name=name)(*args)
        return list(outs), []
    ni, no, ns, nt = len(in_specs), len(out_specs), len(scratch_shapes), comm.nt
    steps = math.prod(grid)
    hbm = pl.BlockSpec(memory_space=pltpu.HBM)
    landing = [t for t in range(nt) if comm.into[t] is not None]

    def carrier(*refs):
        ins, cin = refs[:ni], refs[ni:ni + nt]
        refs = refs[:ni + nt] + refs[ni + nt + len(landing):]
        outs, cout = refs[ni + nt:ni + nt + no], refs[ni + nt + no:ni + 2 * nt + no]
        scr, sems = refs[ni + 2 * nt + no:ni + 2 * nt + no + ns], refs[ni + 2 * nt + no + ns:]
        step = functools.reduce(lambda acc, d: acc * grid[d] + pl.program_id(d), range(len(grid)), 0)

        @pl.when(step == 0)
        def _():
            comm.start(cin, cout, sems)

        body(*ins, *outs, *scr)

        @pl.when(step == (7 * steps) // 8)
        def _():
            comm.relay(cin, cout, sems)

        @pl.when(step == steps - 1)
        def _():
            comm.finish(cin, cout, sems)

    outs = pl.pallas_call(
        carrier, grid=grid, in_specs=in_specs + [hbm] * (nt + len(landing)), out_specs=out_specs + [hbm] * nt,
        out_shape=out_shape + comm.out_shape, scratch_shapes=scratch_shapes + comm.scratch,
        input_output_aliases={ni + nt + a: no + t for a, t in enumerate(landing)},
        compiler_params=_cp(("arbitrary",) * len(grid)), name=name)(*args, *comm.ins, *[comm.into[t] for t in landing])
    return list(outs[:no]), list(outs[no:])


def exchange(name, ins, gather):
    comm = Comm(ins, gather)
    hbm = pl.BlockSpec(memory_space=pltpu.HBM)

    def body(*refs):
        src, dst, sems = refs[:comm.nt], refs[comm.nt:2 * comm.nt], refs[2 * comm.nt:]
        comm.start(src, dst, sems)
        comm.relay(src, dst, sems)
        comm.finish(src, dst, sems)

    return pl.pallas_call(body, in_specs=[hbm] * comm.nt, out_specs=[hbm] * comm.nt, out_shape=comm.out_shape,
                          scratch_shapes=comm.scratch, name=name)(*ins)


def _matmul(name, a, b, extras, *, grid, a_spec, b_spec, extra_specs, out_shapes, out_specs,
            contract, epilogue, acc_shape, comm=None):
    nk = grid[2]
    ne, no = len(extras), len(out_shapes)

    def body(a_ref, b_ref, *rest):
        ex, outs, acc = rest[:ne], rest[ne:ne + no], rest[ne + no]
        kk = pl.program_id(2)

        @pl.when(kk == 0)
        def _():
            acc[...] = jnp.zeros_like(acc)

        acc[...] += lax.dot_general(a_ref[...].astype(BF16), b_ref[...].astype(BF16),
                                    (contract, ((), ())), preferred_element_type=F32)

        @pl.when(kk == nk - 1)
        def _():
            res = epilogue(acc[...], *[r[...] for r in ex])
            for r, o in zip(outs, res):
                r[...] = o.astype(r.dtype)

    return _pcall(name, body, grid=grid, in_specs=[a_spec, b_spec, *extra_specs], out_specs=out_specs,
                  out_shape=out_shapes, scratch_shapes=[pltpu.VMEM(acc_shape, F32)],
                  sem=("parallel", "parallel", "arbitrary"), args=(a, b, *extras), comm=comm)


def _ident(acc):
    return (acc,)


def _fit(tile, dim):
    return max(t for t in range(128, min(tile, dim) + 1, 128) if dim % t == 0)


def mm_nn(name, a, w, *, out_dtype, tm=1024, tn=1024, tk=2048, extras=(), epilogue=_ident, n_out=1,
          out_dtypes=None, comm=None):
    m, k = a.shape
    tm, tk = _fit(tm, m), _fit(tk, k)
    if w.ndim == 3:
        tn = w.shape[2]
        n = NDEV * tn
        b_spec = pl.BlockSpec((None, tk, tn), lambda i, j, kk: (j, kk, 0))
    else:
        n = w.shape[1]
        tn = _fit(tn, n)
        b_spec = pl.BlockSpec((tk, tn), lambda i, j, kk: (kk, j))
    out_dtypes = out_dtypes or (out_dtype,) * n_out
    mn = pl.BlockSpec((tm, tn), lambda i, j, kk: (i, j))
    return _matmul(name, a, w, extras, grid=(m // tm, n // tn, k // tk),
                   a_spec=pl.BlockSpec((tm, tk), lambda i, j, kk: (i, kk)), b_spec=b_spec,
                   extra_specs=[mn] * len(extras),
                   out_shapes=[jax.ShapeDtypeStruct((m, n), dt) for dt in out_dtypes],
                   out_specs=[mn] * len(out_dtypes), contract=((1,), (0,)), epilogue=epilogue,
                   acc_shape=(tm, tn), comm=comm)


def mm_nt(name, a, w, *, out_dtype, tm=1024, tn=1024, tk=2048, extras=(), epilogue=_ident, comm=None):
    m, n = a.shape
    k = w.shape[-2]
    tm, to = _fit(tm, m), _fit(tn, k)
    if w.ndim == 3:
        tc = w.shape[2]
        b_spec = pl.BlockSpec((None, to, tc), lambda i, j, kk: (kk, j, 0))
    else:
        tc = _fit(tk, n)
        b_spec = pl.BlockSpec((to, tc), lambda i, j, kk: (j, kk))
    mo = pl.BlockSpec((tm, to), lambda i, j, kk: (i, j))
    outs, got = _matmul(name, a, w, extras, grid=(m // tm, k // to, n // tc),
                        a_spec=pl.BlockSpec((tm, tc), lambda i, j, kk: (i, kk)), b_spec=b_spec,
                        extra_specs=[mo] * len(extras),
                        out_shapes=[jax.ShapeDtypeStruct((m, k), out_dtype)],
                        out_specs=[mo], contract=((1,), (1,)), epilogue=epilogue, acc_shape=(tm, to),
                        comm=comm)
    return outs[0], got


def mm_tn(name, a, b, *, out_dtype=BF16, tm=1024, tn=1024, tk=2048, col_shards=False, comm=None):
    t, k = a.shape
    n = b.shape[1]
    tm, tk = _fit(tm, k), _fit(tk, t)
    if col_shards:
        tn = n // NDEV
        shape, spec = (NDEV, k, tn), pl.BlockSpec((None, tm, tn), lambda i, j, kk: (j, i, 0))
    else:
        tn = _fit(tn, n)
        shape, spec = (k, n), pl.BlockSpec((tm, tn), lambda i, j, kk: (i, j))
    outs, got = _matmul(name, a, b, (), grid=(k // tm, n // tn, t // tk),
                        a_spec=pl.BlockSpec((tk, tm), lambda i, j, kk: (kk, i)),
                        b_spec=pl.BlockSpec((tk, tn), lambda i, j, kk: (kk, j)),
                        extra_specs=[], out_shapes=[jax.ShapeDtypeStruct(shape, out_dtype)],
                        out_specs=[spec],
                        contract=((0,), (0,)), epilogue=_ident, acc_shape=(tm, tn), comm=comm)
    return outs[0], got


ROWS = 256


def rmsnorm_fwd(name, x, gain):
    def body(x_ref, g_ref, h_ref):
        xf = x_ref[...]
        rstd = lax.rsqrt(jnp.mean(xf * xf, axis=-1, keepdims=True) + RMS_EPS)
        h_ref[...] = (xf * rstd * g_ref[...]).astype(BF16)

    row = pl.BlockSpec((ROWS, D), lambda i: (i, 0))
    return pl.pallas_call(body, grid=(T // ROWS,), in_specs=[row, pl.BlockSpec((1, D), lambda i: (0, 0))],
                          out_specs=row, out_shape=jax.ShapeDtypeStruct((T, D), BF16),
                          compiler_params=_cp(("parallel",)), name=name)(x, gain)


def _rms_bwd_math(dh, xf, gain):
    rstd = lax.rsqrt(jnp.mean(xf * xf, axis=-1, keepdims=True) + RMS_EPS)
    xhat = xf * rstd
    dgain = jnp.sum(dh * xhat, axis=0, keepdims=True)
    dxhat = dh * gain
    dx = rstd * (dxhat - xhat * jnp.mean(dxhat * xhat, axis=-1, keepdims=True))
    return dx, dgain


def rmsnorm_bwd(name, dh, x, gain, dres):
    def body(dh_ref, x_ref, g_ref, dres_ref, dx_ref, dxb_ref, dg_ref):
        dx, dgain = _rms_bwd_math(dh_ref[...], x_ref[...], g_ref[...])
        dx = dx + dres_ref[...]
        dx_ref[...] = dx
        dxb_ref[...] = dx.astype(BF16)

        @pl.when(pl.program_id(0) == 0)
        def _():
            dg_ref[...] = jnp.zeros_like(dg_ref)

        dg_ref[...] += dgain

    row = pl.BlockSpec((ROWS, D), lambda i: (i, 0))
    vec = pl.BlockSpec((1, D), lambda i: (0, 0))
    return pl.pallas_call(
        body, grid=(T // ROWS,), in_specs=[row, row, vec, row], out_specs=[row, row, vec],
        out_shape=[jax.ShapeDtypeStruct((T, D), F32), jax.ShapeDtypeStruct((T, D), BF16),
                   jax.ShapeDtypeStruct((1, D), F32)],
        compiler_params=_cp(("arbitrary",)), name=name)(dh, x, gain, dres)


def loss_head(name, x, gain, target):
    def body(x_ref, g_ref, t_ref, dx_ref, dxb_ref, dg_ref, loss_ref):
        xf, gain = x_ref[...], g_ref[...]
        rstd = lax.rsqrt(jnp.mean(xf * xf, axis=-1, keepdims=True) + RMS_EPS)
        err = xf * rstd * gain - t_ref[...]
        part = 0.5 * jnp.sum(jnp.mean(err * err, axis=-1, keepdims=True), axis=0, keepdims=True)
        dx, dgain = _rms_bwd_math(err * (1.0 / D), xf, gain)
        dx_ref[...] = dx
        dxb_ref[...] = dx.astype(BF16)

        @pl.when(pl.program_id(0) == 0)
        def _():
            dg_ref[...] = jnp.zeros_like(dg_ref)
            loss_ref[...] = jnp.zeros_like(loss_ref)

        dg_ref[...] += dgain
        loss_ref[...] += jnp.broadcast_to(part, loss_ref.shape)

    row = pl.BlockSpec((ROWS, D), lambda i: (i, 0))
    vec = pl.BlockSpec((1, D), lambda i: (0, 0))
    return pl.pallas_call(
        body, grid=(T // ROWS,), in_specs=[row, vec, row],
        out_specs=[row, row, vec, pl.BlockSpec((1, 128), lambda i: (0, 0))],
        out_shape=[jax.ShapeDtypeStruct((T, D), F32), jax.ShapeDtypeStruct((T, D), BF16),
                   jax.ShapeDtypeStruct((1, D), F32), jax.ShapeDtypeStruct((1, 128), F32)],
        compiler_params=_cp(("arbitrary",)), name=name)(x, gain, target)


def _a_common(n, g, s_ref, q_ref, kc_ref, kp_ref):
    row = lax.broadcasted_iota(jnp.int32, (GROUP_A * BLK, BLK), 0)
    col = lax.broadcasted_iota(jnp.int32, (GROUP_A * BLK, BLK), 1)
    qi = row & (BLK - 1)
    dist_c = (qi - col).astype(F32)
    valid_c = qi >= col
    valid_p = jnp.logical_and(col > qi, n > 0)
    hidx = lax.broadcasted_iota(jnp.int32, (GROUP_A * BLK, 1), 0) >> 7
    slope = jnp.exp((-math.log(2.0) / 4.0) * (hidx + (GROUP_A * g + 1)).astype(F32))
    sink = jnp.zeros((GROUP_A * BLK, 1), F32)
    for i in range(GROUP_A):
        sink = jnp.where(hidx == i, s_ref[GROUP_A * g + i], sink)
    qg = jnp.concatenate([q_ref[:, (GROUP_A * g + i) * HD:(GROUP_A * g + i + 1) * HD]
                          for i in range(GROUP_A)], axis=0)
    kc = kc_ref[:, g * HD:(g + 1) * HD]
    kp = kp_ref[:, g * HD:(g + 1) * HD]
    nt = (((1,), (1,)), ((), ()))
    sc = lax.dot_general(qg, kc, nt, preferred_element_type=F32) * SCALE - slope * dist_c
    sp = lax.dot_general(qg, kp, nt, preferred_element_type=F32) * SCALE - slope * (dist_c + float(BLK))
    sc = jnp.where(valid_c, sc, NEG)
    sp = jnp.where(valid_p, sp, NEG)
    m = jnp.maximum(jnp.maximum(jnp.max(sc, axis=-1, keepdims=True), jnp.max(sp, axis=-1, keepdims=True)), sink)
    pc = jnp.exp(sc - m)
    pp = jnp.exp(sp - m)
    ps = jnp.exp(sink - m)
    den = jnp.sum(pc, axis=-1, keepdims=True) + jnp.sum(pp, axis=-1, keepdims=True) + ps
    return qg, kc, kp, pc, pp, ps, den, hidx


def attn_a_fwd(name, qkv, sinks, comm=None):
    kcol, vcol = NQ_A * HD // (NKV_A * HD), NQ_A * HD // (NKV_A * HD) + 1

    def body(s_ref, q_ref, kc_ref, kp_ref, vc_ref, vp_ref, o_ref):
        n = pl.program_id(0)
        for g in range(NKV_A):
            qg, kc, kp, pc, pp, ps, den, _ = _a_common(n, g, s_ref, q_ref, kc_ref, kp_ref)
            vc = vc_ref[:, g * HD:(g + 1) * HD]
            vp = vp_ref[:, g * HD:(g + 1) * HD]
            og = (jnp.dot(pc.astype(BF16), vc, preferred_element_type=F32)
                  + jnp.dot(pp.astype(BF16), vp, preferred_element_type=F32)) / den
            for i in range(GROUP_A):
                h = GROUP_A * g + i
                o_ref[:, h * HD:(h + 1) * HD] = og[i * BLK:(i + 1) * BLK].astype(BF16)

    kvw = NKV_A * HD
    prev = lambda n: jnp.maximum(n - 1, 0)
    outs, got = _pcall(
        name, body, grid=(T // BLK,),
        in_specs=[pl.BlockSpec(memory_space=pltpu.SMEM),
                  pl.BlockSpec((BLK, NQ_A * HD), lambda n: (n, 0)),
                  pl.BlockSpec((BLK, kvw), lambda n: (n, kcol)),
                  pl.BlockSpec((BLK, kvw), lambda n: (prev(n), kcol)),
                  pl.BlockSpec((BLK, kvw), lambda n: (n, vcol)),
                  pl.BlockSpec((BLK, kvw), lambda n: (prev(n), vcol))],
        out_specs=[pl.BlockSpec((BLK, NQ_A * HD), lambda n: (n, 0))],
        out_shape=[jax.ShapeDtypeStruct((T, NQ_A * HD), BF16)],
        sem=("parallel",), args=(sinks, qkv, qkv, qkv, qkv, qkv), comm=comm)
    return outs[0], got


def attn_a_bwd(name, qkv, sinks, o, do, comm=None):
    kcol, vcol = NQ_A * HD // (NKV_A * HD), NQ_A * HD // (NKV_A * HD) + 1
    nb = T // BLK
    kvw = NKV_A * HD

    def body(s_ref, q_ref, kc_ref, kp_ref, vc_ref, vp_ref, o_ref, do_ref,
             dq_ref, dk_ref, dv_ref, ds_ref, ck_ref, cv_ref):
        n = pl.program_id(0)

        @pl.when(n == 0)
        def _():
            ds_ref[...] = jnp.zeros_like(ds_ref)
            ck_ref[...] = jnp.zeros_like(ck_ref)
            cv_ref[...] = jnp.zeros_like(cv_ref)

        @pl.when(n == nb)
        def _():
            dk_ref[...] = ck_ref[...].astype(BF16)
            dv_ref[...] = cv_ref[...].astype(BF16)

        @pl.when(n < nb)
        def _():
            lane = lax.broadcasted_iota(jnp.int32, (1, 128), 1)
            dsink_row = jnp.zeros((1, 128), F32)
            tn = (((0,), (0,)), ((), ()))
            nt = (((1,), (1,)), ((), ()))
            for g in range(NKV_A):
                qg, kc, kp, pc, pp, ps, den, hidx = _a_common(n, g, s_ref, q_ref, kc_ref, kp_ref)
                vc = vc_ref[:, g * HD:(g + 1) * HD]
                vp = vp_ref[:, g * HD:(g + 1) * HD]
                cols = [slice((GROUP_A * g + i) * HD, (GROUP_A * g + i + 1) * HD) for i in range(GROUP_A)]
                og = jnp.concatenate([o_ref[:, c] for c in cols], axis=0)
                dog = jnp.concatenate([do_ref[:, c] for c in cols], axis=0)
                delta = jnp.sum(og.astype(F32) * dog.astype(F32), axis=-1, keepdims=True)
                inv = 1.0 / den
                p_c, p_p = pc * inv, pp * inv
                dsc = p_c * (lax.dot_general(dog, vc, nt, preferred_element_type=F32) - delta)
                dsp = p_p * (lax.dot_general(dog, vp, nt, preferred_element_type=F32) - delta)
                dsk = -(ps * inv) * delta
                for i in range(GROUP_A):
                    tot = jnp.sum(jnp.where(hidx == i, dsk, 0.0), axis=0, keepdims=True)
                    dsink_row = dsink_row + jnp.where(lane == GROUP_A * g + i, tot, 0.0)
                dscb, dspb = (dsc * SCALE).astype(BF16), (dsp * SCALE).astype(BF16)
                dqg = (jnp.dot(dscb, kc, preferred_element_type=F32)
                       + jnp.dot(dspb, kp, preferred_element_type=F32))
                for i in range(GROUP_A):
                    dq_ref[:, cols[i]] = dqg[i * BLK:(i + 1) * BLK].astype(BF16)
                gs = slice(g * HD, (g + 1) * HD)
                dk_ref[:, gs] = (ck_ref[:, gs] + lax.dot_general(dspb, qg, tn, preferred_element_type=F32)).astype(BF16)
                dv_ref[:, gs] = (cv_ref[:, gs] + lax.dot_general(p_p.astype(BF16), dog, tn, preferred_element_type=F32)).astype(BF16)
                ck_ref[:, gs] = lax.dot_general(dscb, qg, tn, preferred_element_type=F32)
                cv_ref[:, gs] = lax.dot_general(p_c.astype(BF16), dog, tn, preferred_element_type=F32)
            ds_ref[...] += jnp.broadcast_to(dsink_row, ds_ref.shape)

    cur = lambda n: jnp.minimum(n, nb - 1)
    prev = lambda n: jnp.maximum(jnp.minimum(n, nb - 1) - 1, 0)
    outp = lambda n: jnp.maximum(n - 1, 0)
    qspec = pl.BlockSpec((BLK, NQ_A * HD), lambda n: (cur(n), 0))
    return _pcall(
        name, body, grid=(nb + 1,),
        in_specs=[pl.BlockSpec(memory_space=pltpu.SMEM), qspec,
                  pl.BlockSpec((BLK, kvw), lambda n: (cur(n), kcol)),
                  pl.BlockSpec((BLK, kvw), lambda n: (prev(n), kcol)),
                  pl.BlockSpec((BLK, kvw), lambda n: (cur(n), vcol)),
                  pl.BlockSpec((BLK, kvw), lambda n: (prev(n), vcol)),
                  qspec, qspec],
        out_specs=[qspec, pl.BlockSpec((BLK, kvw), lambda n: (outp(n), 0)),
                   pl.BlockSpec((BLK, kvw), lambda n: (outp(n), 0)),
                   pl.BlockSpec((8, 128), lambda n: (0, 0))],
        out_shape=[jax.ShapeDtypeStruct((T, NQ_A * HD), BF16), jax.ShapeDtypeStruct((T, kvw), BF16),
                   jax.ShapeDtypeStruct((T, kvw), BF16), jax.ShapeDtypeStruct((8, 128), F32)],
        scratch_shapes=[pltpu.VMEM((BLK, kvw), F32), pltpu.VMEM((BLK, kvw), F32)],
        sem=("arbitrary",), args=(sinks, qkv, qkv, qkv, qkv, qkv, o, do), comm=comm)


def _b_logs(z):
    lb = jnp.minimum(z, 0.0) - jnp.log(1.0 + jnp.exp(-jnp.abs(z)))
    return lb, lb - z


def _cumsum_excl(xs, later):
    r = lax.broadcasted_iota(jnp.int32, (BLK, BLK), 0)
    c = lax.broadcasted_iota(jnp.int32, (BLK, BLK), 1)
    tri = (r > c if later else r < c).astype(BF16)
    blocks = [slice(b * BLK, (b + 1) * BLK) for b in range(B_UNROLL)]
    pieces = []
    for x in xs:
        hi = x.astype(BF16)
        lo = (x - hi.astype(F32)).astype(BF16)
        pieces += [hi[:, b] for b in blocks] + [lo[:, b] for b in blocks]
    inside = jnp.dot(jnp.concatenate(pieces, axis=0), tri, preferred_element_type=F32)
    piece = lambda i: inside[i * BLK:(i + 1) * BLK]
    outs, totals = [], []
    for h, x in enumerate(xs):
        base = 2 * B_UNROLL * h
        sums = [jnp.sum(x[:, b], axis=-1, keepdims=True) for b in blocks]
        cols = [functools.reduce(jnp.add, sums[i + 1:] if later else sums[:i],
                                 piece(base + i) + piece(base + B_UNROLL + i)) for i in range(B_UNROLL)]
        outs.append(jnp.concatenate(cols, axis=1))
        totals.append(functools.reduce(jnp.add, sums))
    return outs, totals


B_CUT = 105.0
B_NH = 4
B_NH_FWD = 8
B_HEADS = tuple(slice(h * HD, (h + 1) * HD) for h in range(B_NH))
B_UNROLL = 3
B_WIN = B_UNROLL * BLK
B_GROUPS = -(-(T // BLK) // B_UNROLL)


def _b_window(j0, n):
    first = jnp.maximum(j0 - (B_UNROLL - 1), 0) * BLK
    row = lax.broadcasted_iota(jnp.int32, (BLK, B_WIN), 0)
    col = lax.broadcasted_iota(jnp.int32, (BLK, B_WIN), 1)
    valid = jnp.logical_and(col - row < n * BLK - first, col < (j0 + 1) * BLK - first)
    return pl.ds(pl.multiple_of(first, BLK), B_WIN), valid


def _b_weights(qs, kws, valid, cs):
    nt = (((1,), (1,)), ((), ()))
    zs = [lax.dot_general(q, kw, nt, preferred_element_type=F32) * SCALE for q, kw in zip(qs, kws)]
    logs = [_b_logs(z) for z in zs]
    lms = [jnp.where(valid, lm, 0.0) for _, lm in logs]
    afters, totals = _cumsum_excl(lms, later=True)
    weights = [jnp.where(valid, jnp.exp(lb + c + after), 0.0) for (lb, _), c, after in zip(logs, cs, afters)]
    return [lb for lb, _ in logs], weights, [c + t for c, t in zip(cs, totals)]


def _b_live(cs):
    return functools.reduce(jnp.maximum, [jnp.max(c) for c in cs]) > -B_CUT


def attn_b_fwd(name, qkv, comm=None):
    nh = B_NH_FWD
    heads = tuple(slice(h * HD, (h + 1) * HD) for h in range(nh))
    npair = NH_B // nh

    def body(q_ref, k_ref, v_ref, o_ref):
        n = pl.program_id(1)
        qs = [q_ref[:, hs] for hs in heads]

        def group(j0, st):
            rows, valid = _b_window(j0, n)
            _, weights, cs = _b_weights(qs, [k_ref[rows, hs] for hs in heads], valid, [c for c, _ in st])
            accs = [acc + jnp.dot(a.astype(BF16), v_ref[rows, hs], preferred_element_type=F32)
                    for (_, acc), a, hs in zip(st, weights, heads)]
            return tuple(zip(cs, accs))

        zero = (jnp.zeros((BLK, 1), F32), jnp.zeros((BLK, HD), F32))
        _, st = lax.while_loop(lambda cr: jnp.logical_and(cr[0] >= 0, _b_live([c for c, _ in cr[1]])),
                               lambda cr: (cr[0] - B_UNROLL, group(cr[0], cr[1])), (n, (zero,) * nh))
        for h, hs in enumerate(heads):
            o_ref[:, hs] = st[h][1].astype(BF16)

    outs, got = _pcall(
        name, body, grid=(npair, T // BLK),
        in_specs=[pl.BlockSpec((BLK, nh * HD), lambda p, n: (n, p)),
                  pl.BlockSpec((T, nh * HD), lambda p, n: (0, npair + p)),
                  pl.BlockSpec((T, nh * HD), lambda p, n: (0, 2 * npair + p))],
        out_specs=[pl.BlockSpec((BLK, nh * HD), lambda p, n: (n, p))],
        out_shape=[jax.ShapeDtypeStruct((T, NH_B * HD), BF16)],
        sem=("parallel", "arbitrary"), args=(qkv, qkv, qkv), comm=comm)
    return outs[0], got


def attn_b_bwd(name, qkv, do, comm=None):
    npair = NH_B // B_NH
    nb = T // BLK

    def body(q_ref, do_ref, k_ref, v_ref, dq_ref, dk_ref, dv_ref, a_s, lb_s):
        n = pl.program_id(1)

        @pl.when(n == 0)
        def _():
            dk_ref[...] = jnp.zeros_like(dk_ref)
            dv_ref[...] = jnp.zeros_like(dv_ref)

        nt = (((1,), (1,)), ((), ()))
        tn = (((0,), (0,)), ((), ()))
        qs = [q_ref[:, hs] for hs in B_HEADS]
        dos = [do_ref[:, hs] for hs in B_HEADS]

        def sweep1(gi, j0, cs):
            rows, valid = _b_window(j0, n)
            lbs, weights, cs = _b_weights(qs, [k_ref[rows, hs] for hs in B_HEADS], valid, cs)
            for h in range(B_NH):
                a_s[h, gi] = weights[h]
                lb_s[h, gi] = lbs[h]
            return tuple(cs)

        _, groups, _ = lax.while_loop(
            lambda cr: jnp.logical_and(cr[0] >= 0, _b_live(cr[2])),
            lambda cr: (cr[0] - B_UNROLL, cr[1] + 1, sweep1(cr[1], cr[0], cr[2])),
            (n, 0, (jnp.zeros((BLK, 1), F32),) * B_NH))

        def sweep2(i, st):
            gi = groups - 1 - i
            rows, valid = _b_window(n - gi * B_UNROLL, n)
            kws, vws = [k_ref[rows, hs] for hs in B_HEADS], [v_ref[rows, hs] for hs in B_HEADS]
            weights, lbs = [a_s[h, gi] for h in range(B_NH)], [lb_s[h, gi] for h in range(B_NH)]
            gs = [a * lax.dot_general(do, vw, nt, preferred_element_type=F32)
                  for a, do, vw in zip(weights, dos, vws)]
            befores, totals = _cumsum_excl(gs, later=False)
            dzbs = []
            for (cg, _), g, before, lb in zip(st, gs, befores, lbs):
                dz = jnp.where(valid, g - (g + cg + before) * jnp.exp(lb), 0.0)
                dzbs.append((dz * SCALE).astype(BF16))
            for h, hs in enumerate(B_HEADS):
                dk_ref[rows, hs] += lax.dot_general(dzbs[h], qs[h], tn, preferred_element_type=F32)
                dv_ref[rows, hs] += lax.dot_general(weights[h].astype(BF16), dos[h], tn, preferred_element_type=F32)
            return tuple((cg + total, dq + jnp.dot(dzb, kw, preferred_element_type=F32))
                         for (cg, dq), total, dzb, kw in zip(st, totals, dzbs, kws))

        zero = (jnp.zeros((BLK, 1), F32), jnp.zeros((BLK, HD), F32))
        st = lax.fori_loop(0, groups, sweep2, (zero,) * B_NH)
        for h, hs in enumerate(B_HEADS):
            dq_ref[:, hs] = st[h][1].astype(BF16)

    tile = pl.BlockSpec((BLK, B_NH * HD), lambda p, n: (n, p))
    full = lambda off: pl.BlockSpec((T, B_NH * HD), lambda p, n: (0, off + p))
    return _pcall(
        name, body, grid=(npair, nb),
        in_specs=[tile, tile, full(npair), full(2 * npair)],
        out_specs=[tile, full(0), full(0)],
        out_shape=[jax.ShapeDtypeStruct((T, NH_B * HD), BF16), jax.ShapeDtypeStruct((T, NH_B * HD), F32),
                   jax.ShapeDtypeStruct((T, NH_B * HD), F32)],
        scratch_shapes=[pltpu.VMEM((B_NH, B_GROUPS, BLK, B_WIN), F32)] * 2,
        sem=("parallel", "arbitrary"), args=(qkv, do, qkv, qkv), comm=comm)


def _adamw_math(w, g, m, v):
    m = B1 * m + (1.0 - B1) * g
    v = B2 * v + (1.0 - B2) * (g * g)
    m_hat = m / (1.0 - B1 ** STEP)
    v_hat = v / (1.0 - B2 ** STEP)
    delta = -LR * (m_hat / (jnp.sqrt(v_hat) + EPS) + WD * w)
    return delta, m, v


def adamw(name, parts, w, m, v, rows, layer=0, prev=None):
    nl, r, c = w.shape

    def body(p_ref, w_ref, m_ref, v_ref, *rest):
        g_ref, d_ref, nm_ref, nv_ref = rest[-4:]
        g = p_ref[0].astype(F32)
        for s in range(1, NDEV):
            g = g + p_ref[s].astype(F32)
        delta, nm, nv = _adamw_math(w_ref[...], g, m_ref[...], v_ref[...])
        g_ref[...] = g
        d_ref[...] = delta
        nm_ref[...] = nm
        nv_ref[...] = nv

    blk = pl.BlockSpec((None, rows, c), lambda i: (layer, i, 0))
    carried = [] if prev is None else list(prev)
    return pl.pallas_call(
        body, grid=(r // rows,),
        in_specs=[pl.BlockSpec((NDEV, rows, c), lambda i: (0, i, 0)), blk, blk, blk]
        + [pl.BlockSpec(memory_space=pl.ANY)] * len(carried),
        out_specs=[blk] * 4, out_shape=[jax.ShapeDtypeStruct((nl, r, c), F32)] * 4,
        input_output_aliases={4 + q: q for q in range(len(carried))},
        compiler_params=_cp(("parallel",)), name=name)(parts, w, m, v, *carried)


def _unshard_cols(g):
    return jnp.transpose(g, (1, 0, 2)).reshape(g.shape[1], -1)


def _shard_cols(w):
    k, n = w.shape
    return jnp.transpose(w.reshape(k, NDEV, n // NDEV), (1, 0, 2))


FWD_CARRY = {
    ("qkv", 0): [(0, 1)], ("attn", 0): [(0, 2)], ("mlp_in", 0): [(0, 3)], ("mlp_out", 0): [(1, 0)],
    ("qkv", 1): [(1, 1)], ("attn", 1): [(1, 2), (1, 3), (2, 0), (2, 1)],
    ("mlp_in", 1): [(2, 2)], ("mlp_out", 1): [(2, 3)],
    ("qkv", 2): [(3, 1)], ("attn", 2): [(3, 0)], ("mlp_in", 2): [(3, 2)], ("mlp_out", 2): [(3, 3)],
}
BWD_CARRY = {
    ("attn_bwd", 3): [(3, 3), (3, 2), (3, 1)],
    ("g_mlp_out", 2): [(3, 0, 0)], ("d_act", 2): [(3, 0, 1)], ("g_mlp_in", 2): [(2, 3, 0)],
    ("d_h2", 2): [(2, 3, 1)], ("attn_bwd", 2): [(2, 2)], ("d_h1", 2): [(2, 1)],
    ("g_mlp_out", 1): [(2, 0)], ("attn_bwd", 1): [(1, 3), (1, 2), (1, 1)],
    ("g_mlp_out", 0): [(1, 0, 0)], ("d_act", 0): [(1, 0, 1)], ("g_mlp_in", 0): [(0, 3, 0)],
    ("d_h2", 0): [(0, 3, 1)], ("attn_bwd", 0): [(0, 2)], ("g_qkv", 0): [(0, 1)], ("d_h1", 0): [(0, 0)],
}


def kernel(x, a_w_qkv, a_w_o, a_sinks, b_w_qkv, b_w_o, norm_mix, norm_mlp, mlp_w_in, mlp_w_out, final_norm, loss_target, m_a_w_qkv, m_a_w_o, m_a_sinks, m_b_w_qkv, m_b_w_o, m_norm_mix, m_norm_mlp, m_mlp_w_in, m_mlp_w_out, m_final_norm, v_a_w_qkv, v_a_w_o, v_a_sinks, v_b_w_qkv, v_b_w_o, v_norm_mix, v_norm_mlp, v_mlp_w_in, v_mlp_w_out, v_final_norm):
    depth = 4
    xs = x[0]

    shards, full, sendbuf, recv = {}, {}, {}, {}
    for i in range(depth):
        j = i // 2
        wq, wo = (a_w_qkv[j], a_w_o[j]) if i % 2 == 0 else (b_w_qkv[j], b_w_o[j])
        for t, w in enumerate((wq, wo, mlp_w_in[i], mlp_w_out[i])):
            shards[(i, t)] = w.astype(BF16)

    def land_weights(keys, slabs):
        for (i, t), g in zip(keys, slabs):
            if t in (1, 3):
                full[(i, t)] = g.reshape(-1, D)
            else:
                full[(i, t)] = g if g.shape[2] % 128 == 0 else _unshard_cols(g)

    def fwd(kind, i, fn):
        keys = FWD_CARRY.get((kind, i), [])
        out, got = fn(Comm([shards[k] for k in keys], True) if keys else None)
        land_weights(keys, got)
        return out

    def bwd(kind, i, fn):
        keys = BWD_CARRY.get((kind, i), [])
        bufs = [sendbuf[key[:2]] for key in keys]
        halves = [None if len(key) == 2 else (key[2] * (b.shape[1] // 2), b.shape[1] // 2) for key, b in zip(keys, bufs)]
        out, got = fn(Comm(bufs, False, halves, [recv.get(key[:2]) for key in keys]) if keys else None)
        recv.update((key[:2], g) for key, g in zip(keys, got))
        return out

    relu2 = lambda acc: (acc, jnp.square(jnp.maximum(acc, 0.0)))
    add = lambda acc, res: (acc + res,)
    drelu2 = lambda acc, uu: (acc * (2.0 * jnp.maximum(uu.astype(F32), 0.0)),)

    land_weights([(0, 0)], exchange("gather_first", [shards[(0, 0)]], True))
    saved = []
    for i in range(depth):
        x1 = xs
        h1 = rmsnorm_fwd(f"norm_mix{i}", x1, norm_mix[i][None])
        (qkv,) = fwd("qkv", i, lambda c: mm_nn(f"qkv{i}", h1, full[(i, 0)], out_dtype=BF16, tn=512, comm=c))
        if i % 2 == 0:
            o = fwd("attn", i, lambda c: attn_a_fwd(f"attn_a{i}", qkv, a_sinks[i // 2], comm=c))
        else:
            o = fwd("attn", i, lambda c: attn_b_fwd(f"attn_b{i}", qkv, comm=c))
        (x2,) = mm_nn(f"proj{i}", o, full[(i, 1)], out_dtype=F32, extras=(x1,), epilogue=add)[0]
        h2 = rmsnorm_fwd(f"norm_mlp{i}", x2, norm_mlp[i][None])
        u, act = fwd("mlp_in", i, lambda c: mm_nn(f"mlp_in{i}", h2, full[(i, 2)], out_dtype=BF16,
                                                   epilogue=relu2, n_out=2, comm=c))
        (xs,) = fwd("mlp_out", i, lambda c: mm_nn(f"mlp_out{i}", act, full[(i, 3)], out_dtype=F32,
                                                   extras=(x2,), epilogue=add, comm=c))
        saved.append((x1, h1, qkv, o, x2, h2, u, act))

    dx, dxb, d_final, loss_part = loss_head("loss_head", xs, final_norm[None], loss_target[0])
    loss = lax.psum(loss_part[0, 0], ("x", "y", "c"))

    d_mix, d_mlp, d_sinks = [None] * depth, [None] * depth, [None] * 2
    for i in reversed(range(depth)):
        x1, h1, qkv, o, x2, h2, u, act = saved[i]
        sendbuf[(i, 3)] = bwd("g_mlp_out", i, lambda c: mm_tn(f"g_mlp_out{i}", act, dxb, comm=c)).reshape(
            NDEV, DFF // NDEV, D)
        du = bwd("d_act", i, lambda c: mm_nt(f"d_act{i}", dxb, full[(i, 3)], out_dtype=BF16, extras=(u,),
                                             epilogue=drelu2, comm=c))
        sendbuf[(i, 2)] = bwd("g_mlp_in", i, lambda c: mm_tn(f"g_mlp_in{i}", h2, du, col_shards=True, comm=c))
        dh2 = bwd("d_h2", i, lambda c: mm_nt(f"d_h2{i}", du, full[(i, 2)], out_dtype=F32, comm=c))
        dx, dxb, d_mlp[i] = rmsnorm_bwd(f"norm_mlp_bwd{i}", dh2, x2, norm_mlp[i][None], dx)
        sendbuf[(i, 1)] = mm_tn(f"g_proj{i}", o, dxb)[0].reshape(NDEV, D // NDEV, D)
        do = mm_nt(f"d_o{i}", dxb, full[(i, 1)], out_dtype=BF16)[0]
        if i % 2 == 0:
            dq, dk, dv, ds = bwd("attn_bwd", i, lambda c: attn_a_bwd(f"attn_a_bwd{i}", qkv, a_sinks[i // 2],
                                                                       o, do, comm=c))
            d_sinks[i // 2] = ds[0, :NQ_A]
            dqkv = jnp.concatenate([dq, dk, dv], axis=1)
        else:
            dq, dk, dv = bwd("attn_bwd", i, lambda c: attn_b_bwd(f"attn_b_bwd{i}", qkv, do, comm=c))
            dqkv = jnp.concatenate([dq, dk.astype(BF16), dv.astype(BF16)], axis=1)
        if i % 2 == 0:
            sendbuf[(i, 0)] = _shard_cols(bwd("g_qkv", i, lambda c: mm_tn(f"g_qkv{i}", h1, dqkv, tn=512, comm=c)))
        else:
            sendbuf[(i, 0)] = bwd("g_qkv", i, lambda c: mm_tn(f"g_qkv{i}", h1, dqkv, col_shards=True, comm=c))
        dh1 = bwd("d_h1", i, lambda c: mm_nt(f"d_h1{i}", dqkv, full[(i, 0)], out_dtype=F32, comm=c))
        dx, dxb, d_mix[i] = rmsnorm_bwd(f"norm_mix_bwd{i}", dh1, x1, norm_mix[i][None], dx)

    small = jnp.zeros((16, D), F32)
    small = small.at[0:4].set(jnp.concatenate(d_mix, axis=0)).at[4:8].set(jnp.concatenate(d_mlp, axis=0))
    small = small.at[8].set(d_final[0]).at[9, :2 * NQ_A].set(jnp.concatenate(d_sinks))
    (small_parts,) = exchange("exchange_last", [small], True)
    assert len(recv) == 4 * depth and len(full) == 4 * depth
    pack = lambda mix, mlp, fin, snk: (jnp.zeros((16, D), F32).at[0:4].set(mix).at[4:8].set(mlp)
                                       .at[8].set(fin).at[9, :2 * NQ_A].set(snk.reshape(-1)))
    sm = adamw("adamw_small", small_parts, pack(norm_mix, norm_mlp, final_norm, a_sinks)[None],
               pack(m_norm_mix, m_norm_mlp, m_final_norm, m_a_sinks)[None],
               pack(v_norm_mix, v_norm_mlp, v_final_norm, v_a_sinks)[None], rows=16)
    unpack = lambda a: (a[0, 0:4], a[0, 4:8], a[0, 8], a[0, 9, :2 * NQ_A].reshape(2, NQ_A))

    def update(kind, layers, t, w, m, v, rows):
        res = None
        for l, i in enumerate(layers):
            res = adamw(f"adamw_{kind}{i}", recv[(i, t)], w, m, v, rows, layer=l, prev=res)
        return res

    upd = [update("a_qkv", (0, 2), 0, a_w_qkv, m_a_w_qkv, v_a_w_qkv, 256),
           update("a_o", (0, 2), 1, a_w_o, m_a_w_o, v_a_w_o, 128),
           None,
           update("b_qkv", (1, 3), 0, b_w_qkv, m_b_w_qkv, v_b_w_qkv, 256),
           update("b_o", (1, 3), 1, b_w_o, m_b_w_o, v_b_w_o, 128),
           None, None,
           update("mlp_in", range(depth), 2, mlp_w_in, m_mlp_w_in, v_mlp_w_in, 256),
           update("mlp_out", range(depth), 3, mlp_w_out, m_mlp_w_out, v_mlp_w_out, 128),
           None]
    outs = []
    for q in range(4):
        mix, mlp, fin, snk = unpack(sm[q])
        small_out = {2: snk, 5: mix, 6: mlp, 9: fin}
        outs.append([small_out[k] if u is None else u[q] for k, u in enumerate(upd)])
    return (loss, dx[None], *outs[0], *outs[1], *outs[2], *outs[3])
```

```python
import functools
import math

import jax
import jax.numpy as jnp
from jax import lax
from jax.experimental import pallas as pl
from jax.experimental.pallas import tpu as pltpu

F32, BF16 = jnp.float32, jnp.bfloat16
T, D, DFF = 4096, 2048, 8192
HD, BLK = 64, 128
NQ_A, NKV_A, GROUP_A = 32, 4, 8
NH_B = 32
NDEV = 8
RMS_EPS = 1e-5
SCALE = 0.125
NEG = -1e30
VMEM_LIMIT = 56 * 1024 * 1024
LR, B1, B2, EPS, WD, STEP = 0.001, 0.9, 0.999, 1e-08, 0.01, 10
MESH = pl.DeviceIdType.MESH


def _cp(sem, vmem=VMEM_LIMIT):
    return pltpu.CompilerParams(dimension_semantics=sem, vmem_limit_bytes=vmem)


class Comm:
    def __init__(self, ins, gather, rows=None, into=None):
        self.ins, self.nt = list(ins), len(ins)
        self.gather = list(gather) if isinstance(gather, (list, tuple)) else [gather] * self.nt
        self.rows = rows or [None] * self.nt
        self.into = into or [None] * self.nt
        self.out_shape = [jax.ShapeDtypeStruct((NDEV,) + (a.shape if g else a.shape[1:]), a.dtype)
                          for a, g in zip(self.ins, self.gather)]
        self.scratch = [pltpu.SemaphoreType.DMA((7 * self.nt,)), pltpu.SemaphoreType.DMA((7 * self.nt,)),
                        pltpu.SemaphoreType.DMA((self.nt,))]

    def _copies(self, src, dst, sems, *kinds):
        send_sems, recv_sems, local_sems = sems
        x, y, c = lax.axis_index("x"), lax.axis_index("y"), lax.axis_index("c")
        me, sib = (x, y, c), (x, y, 1 - c)
        chips = [(1 - x, y), (x, 1 - y), (1 - x, 1 - y)]
        p = {kind: [] for kind in kinds}
        for t in range(self.nt):
            part = (slice(None),) if self.rows[t] is None else (pl.ds(*self.rows[t]),)

            def slot(d, t=t, part=part):
                return dst[t].at[(4 * d[0] + 2 * d[1] + d[2],) + part]

            def add(kind, k, a, b, to, t=t):
                if kind in p:
                    p[kind].append(pltpu.make_async_remote_copy(
                        src_ref=a, dst_ref=b, send_sem=send_sems.at[t * 7 + k],
                        recv_sem=recv_sems.at[t * 7 + k], device_id=to, device_id_type=MESH))

            own = src[t] if self.gather[t] else src[t].at[(4 * x + 2 * y + c,) + part]
            if "local" in p:
                p["local"].append(pltpu.make_async_copy(own, slot(me), local_sems.at[t]))
            if self.gather[t]:
                add("first", 0, own, slot(me), sib)
                add("first_in", 0, own, slot(sib), sib)
                for j, chip in enumerate(chips):
                    there = (*chip, c)
                    add("first", 1 + j, own, slot(me), there)
                    add("chip_in", 1 + j, own, slot(there), there)
                    add("relay", 4 + j, slot(there), slot(there), sib)
                    add("relay_in", 4 + j, own, slot((*chip, 1 - c)), sib)
            else:
                for k in range(NDEV - 1):
                    bits = k + 1
                    peer = (1 - x if bits & 4 else x, 1 - y if bits & 2 else y, 1 - c if bits & 1 else c)
                    out = src[t].at[(4 * peer[0] + 2 * peer[1] + peer[2],) + part]
                    add("first", k, out, slot(me), peer)
                    add("first_in", k, out, slot(peer), peer)
        return [p[kind] for kind in kinds]

    def start(self, src, dst, sems):
        local, first = self._copies(src, dst, sems, "local", "first")
        for cp in local + first:
            cp.start()

    def relay(self, src, dst, sems):
        chip_in, relay = self._copies(src, dst, sems, "chip_in", "relay")
        for arrived, onward in zip(chip_in, relay):
            arrived.wait_recv()
            onward.start()

    def finish(self, src, dst, sems):
        first_in, relay_in, first, relay, local = self._copies(
            src, dst, sems, "first_in", "relay_in", "first", "relay", "local")
        for cp in first_in + relay_in:
            cp.wait_recv()
        for cp in first + relay:
            cp.wait_send()
        for cp in local:
            cp.wait()


def _pcall(name, body, *, grid, in_specs, out_specs, out_shape, scratch_shapes=(), sem, args, comm=None):
    in_specs, out_specs, out_shape = list(in_specs), list(out_specs), list(out_shape)
    scratch_shapes = list(scratch_shapes)
    if comm is None:
        outs = pl.pallas_call(body, grid=grid, in_specs=in_specs, out_specs=out_specs, out_shape=out_shape,
                              scratch_shapes=scratch_shapes, compiler_params=_cp(sem), name=name)(*args)
        return list(outs), []
    ni, no, ns, nt = len(in_specs), len(out_specs), len(scratch_shapes), comm.nt
    steps = math.prod(grid)
    hbm = pl.BlockSpec(memory_space=pltpu.HBM)
    landing = [t for t in range(nt) if comm.into[t] is not None]

    def carrier(*refs):
        ins, cin = refs[:ni], refs[ni:ni + nt]
        refs = refs[:ni + nt] + refs[ni + nt + len(landing):]
        outs, cout = refs[ni + nt:ni + nt + no], refs[ni + nt + no:ni + 2 * nt + no]
        scr, sems = refs[ni + 2 * nt + no:ni + 2 * nt + no + ns], refs[ni + 2 * nt + no + ns:]
        step = functools.reduce(lambda acc, d: acc * grid[d] + pl.program_id(d), range(len(grid)), 0)

        @pl.when(step == 0)
        def _():
            comm.start(cin, cout, sems)

        body(*ins, *outs, *scr)

        @pl.when(step == (7 * steps) // 8)
        def _():
            comm.relay(cin, cout, sems)

        @pl.when(step == steps - 1)
        def _():
            comm.finish(cin, cout, sems)

    outs = pl.pallas_call(
        carrier, grid=grid, in_specs=in_specs + [hbm] * (nt + len(landing)), out_specs=out_specs + [hbm] * nt,
        out_shape=out_shape + comm.out_shape, scratch_shapes=scratch_shapes + comm.scratch,
        input_output_aliases={ni + nt + a: no + t for a, t in enumerate(landing)},
        compiler_params=_cp(("arbitrary",) * len(grid)), name=name)(*args, *comm.ins, *[comm.into[t] for t in landing])
    return list(outs[:no]), list(outs[no:])


def exchange(name, ins, gather):
    comm = Comm(ins, gather)
    hbm = pl.BlockSpec(memory_space=pltpu.HBM)

    def body(*refs):
        src, dst, sems = refs[:comm.nt], refs[comm.nt:2 * comm.nt], refs[2 * comm.nt:]
        comm.start(src, dst, sems)
        comm.relay(src, dst, sems)
        comm.finish(src, dst, sems)

    return pl.pallas_call(body, in_specs=[hbm] * comm.nt, out_specs=[hbm] * comm.nt, out_shape=comm.out_shape,
                          scratch_shapes=comm.scratch, name=name)(*ins)


def _matmul(name, a, b, extras, *, grid, a_spec, b_spec, extra_specs, out_shapes, out_specs,
            contract, epilogue, acc_shape, comm=None):
    nk = grid[2]
    ne, no = len(extras), len(out_shapes)

    def body(a_ref, b_ref, *rest):
        ex, outs, acc = rest[:ne], rest[ne:ne + no], rest[ne + no]
        kk = pl.program_id(2)

        @pl.when(kk == 0)
        def _():
            acc[...] = jnp.zeros_like(acc)

        acc[...] += lax.dot_general(a_ref[...].astype(BF16), b_ref[...].astype(BF16),
                                    (contract, ((), ())), preferred_element_type=F32)

        @pl.when(kk == nk - 1)
        def _():
            res = epilogue(acc[...], *[r[...] for r in ex])
            for r, o in zip(outs, res):
                r[...] = o.astype(r.dtype)

    return _pcall(name, body, grid=grid, in_specs=[a_spec, b_spec, *extra_specs], out_specs=out_specs,
                  out_shape=out_shapes, scratch_shapes=[pltpu.VMEM(acc_shape, F32)],
                  sem=("parallel", "parallel", "arbitrary"), args=(a, b, *extras), comm=comm)


def _ident(acc):
    return (acc,)


def _fit(tile, dim):
    return max(t for t in range(128, min(tile, dim) + 1, 128) if dim % t == 0)


def mm_nn(name, a, w, *, out_dtype, tm=1024, tn=1024, tk=2048, extras=(), epilogue=_ident, n_out=1,
          out_dtypes=None, comm=None):
    m, k = a.shape
    tm, tk = _fit(tm, m), _fit(tk, k)
    if w.ndim == 3:
        tn = w.shape[2]
        n = NDEV * tn
        b_spec = pl.BlockSpec((None, tk, tn), lambda i, j, kk: (j, kk, 0))
    else:
        n = w.shape[1]
        tn = _fit(tn, n)
        b_spec = pl.BlockSpec((tk, tn), lambda i, j, kk: (kk, j))
    out_dtypes = out_dtypes or (out_dtype,) * n_out
    mn = pl.BlockSpec((tm, tn), lambda i, j, kk: (i, j))
    return _matmul(name, a, w, extras, grid=(m // tm, n // tn, k // tk),
                   a_spec=pl.BlockSpec((tm, tk), lambda i, j, kk: (i, kk)), b_spec=b_spec,
                   extra_specs=[mn] * len(extras),
                   out_shapes=[jax.ShapeDtypeStruct((m, n), dt) for dt in out_dtypes],
                   out_specs=[mn] * len(out_dtypes), contract=((1,), (0,)), epilogue=epilogue,
                   acc_shape=(tm, tn), comm=comm)


def mm_nt(name, a, w, *, out_dtype, tm=1024, tn=1024, tk=2048, extras=(), epilogue=_ident, comm=None):
    m, n = a.shape
    k = w.shape[-2]
    tm, to = _fit(tm, m), _fit(tn, k)
    if w.ndim == 3:
        tc = w.shape[2]
        b_spec = pl.BlockSpec((None, to, tc), lambda i, j, kk: (kk, j, 0))
    else:
        tc = _fit(tk, n)
        b_spec = pl.BlockSpec((to, tc), lambda i, j, kk: (j, kk))
    mo = pl.BlockSpec((tm, to), lambda i, j, kk: (i, j))
    outs, got = _matmul(name, a, w, extras, grid=(m // tm, k // to, n // tc),
                        a_spec=pl.BlockSpec((tm, tc), lambda i, j, kk: (i, kk)), b_spec=b_spec,
                        extra_specs=[mo] * len(extras),
                        out_shapes=[jax.ShapeDtypeStruct((m, k), out_dtype)],
                        out_specs=[mo], contract=((1,), (1,)), epilogue=epilogue, acc_shape=(tm, to),
                        comm=comm)
    return outs[0], got


def mm_tn(name, a, b, *, out_dtype=BF16, tm=1024, tn=1024, tk=2048, col_shards=False, comm=None):
    t, k = a.shape
    n = b.shape[1]
    tm, tk = _fit(tm, k), _fit(tk, t)
    if col_shards:
        tn = n // NDEV
        shape, spec = (NDEV, k, tn), pl.BlockSpec((None, tm, tn), lambda i, j, kk: (j, i, 0))
    else:
        tn = _fit(tn, n)
        shape, spec = (k, n), pl.BlockSpec((tm, tn), lambda i, j, kk: (i, j))
    outs, got = _matmul(name, a, b, (), grid=(k // tm, n // tn, t // tk),
                        a_spec=pl.BlockSpec((tk, tm), lambda i, j, kk: (kk, i)),
                        b_spec=pl.BlockSpec((tk, tn), lambda i, j, kk: (kk, j)),
                        extra_specs=[], out_shapes=[jax.ShapeDtypeStruct(shape, out_dtype)],
                        out_specs=[spec],
                        contract=((0,), (0,)), epilogue=_ident, acc_shape=(tm, tn), comm=comm)
    return outs[0], got


ROWS = 256


def rmsnorm_fwd(name, x, gain):
    def body(x_ref, g_ref, h_ref):
        xf = x_ref[...]
        rstd = lax.rsqrt(jnp.mean(xf * xf, axis=-1, keepdims=True) + RMS_EPS)
        h_ref[...] = (xf * rstd * g_ref[...]).astype(BF16)

    row = pl.BlockSpec((ROWS, D), lambda i: (i, 0))
    return pl.pallas_call(body, grid=(T // ROWS,), in_specs=[row, pl.BlockSpec((1, D), lambda i: (0, 0))],
                          out_specs=row, out_shape=jax.ShapeDtypeStruct((T, D), BF16),
                          compiler_params=_cp(("parallel",)), name=name)(x, gain)


def _rms_bwd_math(dh, xf, gain):
    rstd = lax.rsqrt(jnp.mean(xf * xf, axis=-1, keepdims=True) + RMS_EPS)
    xhat = xf * rstd
    dgain = jnp.sum(dh * xhat, axis=0, keepdims=True)
    dxhat = dh * gain
    dx = rstd * (dxhat - xhat * jnp.mean(dxhat * xhat, axis=-1, keepdims=True))
    return dx, dgain


def rmsnorm_bwd(name, dh, x, gain, dres):
    def body(dh_ref, x_ref, g_ref, dres_ref, dx_ref, dxb_ref, dg_ref):
        dx, dgain = _rms_bwd_math(dh_ref[...], x_ref[...], g_ref[...])
        dx = dx + dres_ref[...]
        dx_ref[...] = dx
        dxb_ref[...] = dx.astype(BF16)

        @pl.when(pl.program_id(0) == 0)
        def _():
            dg_ref[...] = jnp.zeros_like(dg_ref)

        dg_ref[...] += dgain

    row = pl.BlockSpec((ROWS, D), lambda i: (i, 0))
    vec = pl.BlockSpec((1, D), lambda i: (0, 0))
    return pl.pallas_call(
        body, grid=(T // ROWS,), in_specs=[row, row, vec, row], out_specs=[row, row, vec],
        out_shape=[jax.ShapeDtypeStruct((T, D), F32), jax.ShapeDtypeStruct((T, D), BF16),
                   jax.ShapeDtypeStruct((1, D), F32)],
        compiler_params=_cp(("arbitrary",)), name=name)(dh, x, gain, dres)


def loss_head(name, x, gain, target):
    def body(x_ref, g_ref, t_ref, dx_ref, dxb_ref, dg_ref, loss_ref):
        xf, gain = x_ref[...], g_ref[...]
        rstd = lax.rsqrt(jnp.mean(xf * xf, axis=-1, keepdims=True) + RMS_EPS)
        err = xf * rstd * gain - t_ref[...]
        part = 0.5 * jnp.sum(jnp.mean(err * err, axis=-1, keepdims=True), axis=0, keepdims=True)
        dx, dgain = _rms_bwd_math(err * (1.0 / D), xf, gain)
        dx_ref[...] = dx
        dxb_ref[...] = dx.astype(BF16)

        @pl.when(pl.program_id(0) == 0)
        def _():
            dg_ref[...] = jnp.zeros_like(dg_ref)
            loss_ref[...] = jnp.zeros_like(loss_ref)

        dg_ref[...] += dgain
        loss_ref[...] += jnp.broadcast_to(part, loss_ref.shape)

    row = pl.BlockSpec((ROWS, D), lambda i: (i, 0))
    vec = pl.BlockSpec((1, D), lambda i: (0, 0))
    return pl.pallas_call(
        body, grid=(T // ROWS,), in_specs=[row, vec, row],
        out_specs=[row, row, vec, pl.BlockSpec((1, 128), lambda i: (0, 0))],
        out_shape=[jax.ShapeDtypeStruct((T, D), F32), jax.ShapeDtypeStruct((T, D), BF16),
                   jax.ShapeDtypeStruct((1, D), F32), jax.ShapeDtypeStruct((1, 128), F32)],
        compiler_params=_cp(("arbitrary",)), name=name)(x, gain, target)


def _a_common(n, g, s_ref, q_ref, kc_ref, kp_ref):
    row = lax.broadcasted_iota(jnp.int32, (GROUP_A * BLK, BLK), 0)
    col = lax.broadcasted_iota(jnp.int32, (GROUP_A * BLK, BLK), 1)
    qi = row & (BLK - 1)
    dist_c = (qi - col).astype(F32)
    valid_c = qi >= col
    valid_p = jnp.logical_and(col > qi, n > 0)
    hidx = lax.broadcasted_iota(jnp.int32, (GROUP_A * BLK, 1), 0) >> 7
    slope = jnp.exp((-math.log(2.0) / 4.0) * (hidx + (GROUP_A * g + 1)).astype(F32))
    sink = jnp.zeros((GROUP_A * BLK, 1), F32)
    for i in range(GROUP_A):
        sink = jnp.where(hidx == i, s_ref[GROUP_A * g + i], sink)
    qg = jnp.concatenate([q_ref[:, (GROUP_A * g + i) * HD:(GROUP_A * g + i + 1) * HD]
                          for i in range(GROUP_A)], axis=0)
    kc = kc_ref[:, g * HD:(g + 1) * HD]
    kp = kp_ref[:, g * HD:(g + 1) * HD]
    nt = (((1,), (1,)), ((), ()))
    sc = lax.dot_general(qg, kc, nt, preferred_element_type=F32) * SCALE - slope * dist_c
    sp = lax.dot_general(qg, kp, nt, preferred_element_type=F32) * SCALE - slope * (dist_c + float(BLK))
    sc = jnp.where(valid_c, sc, NEG)
    sp = jnp.where(valid_p, sp, NEG)
    m = jnp.maximum(jnp.maximum(jnp.max(sc, axis=-1, keepdims=True), jnp.max(sp, axis=-1, keepdims=True)), sink)
    pc = jnp.exp(sc - m)
    pp = jnp.exp(sp - m)
    ps = jnp.exp(sink - m)
    den = jnp.sum(pc, axis=-1, keepdims=True) + jnp.sum(pp, axis=-1, keepdims=True) + ps
    return qg, kc, kp, pc, pp, ps, den, hidx


def attn_a_fwd(name, qkv, sinks, comm=None):
    kcol, vcol = NQ_A * HD // (NKV_A * HD), NQ_A * HD // (NKV_A * HD) + 1

    def body(s_ref, q_ref, kc_ref, kp_ref, vc_ref, vp_ref, o_ref):
        n = pl.program_id(0)
        for g in range(NKV_A):
            qg, kc, kp, pc, pp, ps, den, _ = _a_common(n, g, s_ref, q_ref, kc_ref, kp_ref)
            vc = vc_ref[:, g * HD:(g + 1) * HD]
            vp = vp_ref[:, g * HD:(g + 1) * HD]
            og = (jnp.dot(pc.astype(BF16), vc, preferred_element_type=F32)
                  + jnp.dot(pp.astype(BF16), vp, preferred_element_type=F32)) / den
            for i in range(GROUP_A):
                h = GROUP_A * g + i
                o_ref[:, h * HD:(h + 1) * HD] = og[i * BLK:(i + 1) * BLK].astype(BF16)

    kvw = NKV_A * HD
    prev = lambda n: jnp.maximum(n - 1, 0)
    outs, got = _pcall(
        name, body, grid=(T // BLK,),
        in_specs=[pl.BlockSpec(memory_space=pltpu.SMEM),
                  pl.BlockSpec((BLK, NQ_A * HD), lambda n: (n, 0)),
                  pl.BlockSpec((BLK, kvw), lambda n: (n, kcol)),
                  pl.BlockSpec((BLK, kvw), lambda n: (prev(n), kcol)),
                  pl.BlockSpec((BLK, kvw), lambda n: (n, vcol)),
                  pl.BlockSpec((BLK, kvw), lambda n: (prev(n), vcol))],
        out_specs=[pl.BlockSpec((BLK, NQ_A * HD), lambda n: (n, 0))],
        out_shape=[jax.ShapeDtypeStruct((T, NQ_A * HD), BF16)],
        sem=("parallel",), args=(sinks, qkv, qkv, qkv, qkv, qkv), comm=comm)
    return outs[0], got


def attn_a_bwd(name, qkv, sinks, o, do, comm=None):
    kcol, vcol = NQ_A * HD // (NKV_A * HD), NQ_A * HD // (NKV_A * HD) + 1
    nb = T // BLK
    kvw = NKV_A * HD

    def body(s_ref, q_ref, kc_ref, kp_ref, vc_ref, vp_ref, o_ref, do_ref,
             dq_ref, dk_ref, dv_ref, ds_ref, ck_ref, cv_ref):
        n = pl.program_id(0)

        @pl.when(n == 0)
        def _():
            ds_ref[...] = jnp.zeros_like(ds_ref)
            ck_ref[...] = jnp.zeros_like(ck_ref)
            cv_ref[...] = jnp.zeros_like(cv_ref)

        @pl.when(n == nb)
        def _():
            dk_ref[...] = ck_ref[...].astype(BF16)
            dv_ref[...] = cv_ref[...].astype(BF16)

        @pl.when(n < nb)
        def _():
            lane = lax.broadcasted_iota(jnp.int32, (1, 128), 1)
            dsink_row = jnp.zeros((1, 128), F32)
            tn = (((0,), (0,)), ((), ()))
            nt = (((1,), (1,)), ((), ()))
            for g in range(NKV_A):
                qg, kc, kp, pc, pp, ps, den, hidx = _a_common(n, g, s_ref, q_ref, kc_ref, kp_ref)
                vc = vc_ref[:, g * HD:(g + 1) * HD]
                vp = vp_ref[:, g * HD:(g + 1) * HD]
                cols = [slice((GROUP_A * g + i) * HD, (GROUP_A * g + i + 1) * HD) for i in range(GROUP_A)]
                og = jnp.concatenate([o_ref[:, c] for c in cols], axis=0)
                dog = jnp.concatenate([do_ref[:, c] for c in cols], axis=0)
                delta = jnp.sum(og.astype(F32) * dog.astype(F32), axis=-1, keepdims=True)
                inv = 1.0 / den
                p_c, p_p = pc * inv, pp * inv
                dsc = p_c * (lax.dot_general(dog, vc, nt, preferred_element_type=F32) - delta)
                dsp = p_p * (lax.dot_general(dog, vp, nt, preferred_element_type=F32) - delta)
                dsk = -(ps * inv) * delta
                for i in range(GROUP_A):
                    tot = jnp.sum(jnp.where(hidx == i, dsk, 0.0), axis=0, keepdims=True)
                    dsink_row = dsink_row + jnp.where(lane == GROUP_A * g + i, tot, 0.0)
                dscb, dspb = (dsc * SCALE).astype(BF16), (dsp * SCALE).astype(BF16)
                dqg = (jnp.dot(dscb, kc, preferred_element_type=F32)
                       + jnp.dot(dspb, kp, preferred_element_type=F32))
                for i in range(GROUP_A):
                    dq_ref[:, cols[i]] = dqg[i * BLK:(i + 1) * BLK].astype(BF16)
                gs = slice(g * HD, (g + 1) * HD)
                dk_ref[:, gs] = (ck_ref[:, gs] + lax.dot_general(dspb, qg, tn, preferred_element_type=F32)).astype(BF16)
                dv_ref[:, gs] = (cv_ref[:, gs] + lax.dot_general(p_p.astype(BF16), dog, tn, preferred_element_type=F32)).astype(BF16)
                ck_ref[:, gs] = lax.dot_general(dscb, qg, tn, preferred_element_type=F32)
                cv_ref[:, gs] = lax.dot_general(p_c.astype(BF16), dog, tn, preferred_element_type=F32)
            ds_ref[...] += jnp.broadcast_to(dsink_row, ds_ref.shape)

    cur = lambda n: jnp.minimum(n, nb - 1)
    prev = lambda n: jnp.maximum(jnp.minimum(n, nb - 1) - 1, 0)
    outp = lambda n: jnp.maximum(n - 1, 0)
    qspec = pl.BlockSpec((BLK, NQ_A * HD), lambda n: (cur(n), 0))
    return _pcall(
        name, body, grid=(nb + 1,),
        in_specs=[pl.BlockSpec(memory_space=pltpu.SMEM), qspec,
                  pl.BlockSpec((BLK, kvw), lambda n: (cur(n), kcol)),
                  pl.BlockSpec((BLK, kvw), lambda n: (prev(n), kcol)),
                  pl.BlockSpec((BLK, kvw), lambda n: (cur(n), vcol)),
                  pl.BlockSpec((BLK, kvw), lambda n: (prev(n), vcol)),
                  qspec, qspec],
        out_specs=[qspec, pl.BlockSpec((BLK, kvw), lambda n: (outp(n), 0)),
                   pl.BlockSpec((BLK, kvw), lambda n: (outp(n), 0)),
                   pl.BlockSpec((8, 128), lambda n: (0, 0))],
        out_shape=[jax.ShapeDtypeStruct((T, NQ_A * HD), BF16), jax.ShapeDtypeStruct((T, kvw), BF16),
                   jax.ShapeDtypeStruct((T, kvw), BF16), jax.ShapeDtypeStruct((8, 128), F32)],
        scratch_shapes=[pltpu.VMEM((BLK, kvw), F32), pltpu.VMEM((BLK, kvw), F32)],
        sem=("arbitrary",), args=(sinks, qkv, qkv, qkv, qkv, qkv, o, do), comm=comm)


def _b_logs(z):
    lb = jnp.minimum(z, 0.0) - jnp.log(1.0 + jnp.exp(-jnp.abs(z)))
    return lb, lb - z


def _cumsum_excl(xs, later):
    r = lax.broadcasted_iota(jnp.int32, (BLK, BLK), 0)
    c = lax.broadcasted_iota(jnp.int32, (BLK, BLK), 1)
    tri = (r > c if later else r < c).astype(BF16)
    blocks = [slice(b * BLK, (b + 1) * BLK) for b in range(B_UNROLL)]
    pieces = []
    for x in xs:
        hi = x.astype(BF16)
        lo = (x - hi.astype(F32)).astype(BF16)
        pieces += [hi[:, b] for b in blocks] + [lo[:, b] for b in blocks]
    inside = jnp.dot(jnp.concatenate(pieces, axis=0), tri, preferred_element_type=F32)
    piece = lambda i: inside[i * BLK:(i + 1) * BLK]
    outs, totals = [], []
    for h, x in enumerate(xs):
        base = 2 * B_UNROLL * h
        sums = [jnp.sum(x[:, b], axis=-1, keepdims=True) for b in blocks]
        cols = [functools.reduce(jnp.add, sums[i + 1:] if later else sums[:i],
                                 piece(base + i) + piece(base + B_UNROLL + i)) for i in range(B_UNROLL)]
        outs.append(jnp.concatenate(cols, axis=1))
        totals.append(functools.reduce(jnp.add, sums))
    return outs, totals


B_CUT = 105.0
B_NH = 4
B_NH_FWD = 8
B_HEADS = tuple(slice(h * HD, (h + 1) * HD) for h in range(B_NH))
B_UNROLL = 3
B_WIN = B_UNROLL * BLK
B_GROUPS = -(-(T // BLK) // B_UNROLL)


def _b_window(j0, n):
    first = jnp.maximum(j0 - (B_UNROLL - 1), 0) * BLK
    row = lax.broadcasted_iota(jnp.int32, (BLK, B_WIN), 0)
    col = lax.broadcasted_iota(jnp.int32, (BLK, B_WIN), 1)
    valid = jnp.logical_and(col - row < n * BLK - first, col < (j0 + 1) * BLK - first)
    return pl.ds(pl.multiple_of(first, BLK), B_WIN), valid


def _b_weights(qs, kws, valid, cs):
    nt = (((1,), (1,)), ((), ()))
    zs = [lax.dot_general(q, kw, nt, preferred_element_type=F32) * SCALE for q, kw in zip(qs, kws)]
    logs = [_b_logs(z) for z in zs]
    lms = [jnp.where(valid, lm, 0.0) for _, lm in logs]
    afters, totals = _cumsum_excl(lms, later=True)
    weights = [jnp.where(valid, jnp.exp(lb + c + after), 0.0) for (lb, _), c, after in zip(logs, cs, afters)]
    return [lb for lb, _ in logs], weights, [c + t for c, t in zip(cs, totals)]


def _b_live(cs):
    return functools.reduce(jnp.maximum, [jnp.max(c) for c in cs]) > -B_CUT


def attn_b_fwd(name, qkv, comm=None):
    nh = B_NH_FWD
    heads = tuple(slice(h * HD, (h + 1) * HD) for h in range(nh))
    npair = NH_B // nh

    def body(q_ref, k_ref, v_ref, o_ref):
        n = pl.program_id(1)
        qs = [q_ref[:, hs] for hs in heads]

        def group(j0, st):
            rows, valid = _b_window(j0, n)
            _, weights, cs = _b_weights(qs, [k_ref[rows, hs] for hs in heads], valid, [c for c, _ in st])
            accs = [acc + jnp.dot(a.astype(BF16), v_ref[rows, hs], preferred_element_type=F32)
                    for (_, acc), a, hs in zip(st, weights, heads)]
            return tuple(zip(cs, accs))

        zero = (jnp.zeros((BLK, 1), F32), jnp.zeros((BLK, HD), F32))
        _, st = lax.while_loop(lambda cr: jnp.logical_and(cr[0] >= 0, _b_live([c for c, _ in cr[1]])),
                               lambda cr: (cr[0] - B_UNROLL, group(cr[0], cr[1])), (n, (zero,) * nh))
        for h, hs in enumerate(heads):
            o_ref[:, hs] = st[h][1].astype(BF16)

    outs, got = _pcall(
        name, body, grid=(npair, T // BLK),
        in_specs=[pl.BlockSpec((BLK, nh * HD), lambda p, n: (n, p)),
                  pl.BlockSpec((T, nh * HD), lambda p, n: (0, npair + p)),
                  pl.BlockSpec((T, nh * HD), lambda p, n: (0, 2 * npair + p))],
        out_specs=[pl.BlockSpec((BLK, nh * HD), lambda p, n: (n, p))],
        out_shape=[jax.ShapeDtypeStruct((T, NH_B * HD), BF16)],
        sem=("parallel", "arbitrary"), args=(qkv, qkv, qkv), comm=comm)
    return outs[0], got


def attn_b_bwd(name, qkv, do, comm=None):
    npair = NH_B // B_NH
    nb = T // BLK

    def body(q_ref, do_ref, k_ref, v_ref, dq_ref, dk_ref, dv_ref, a_s, lb_s):
        n = pl.program_id(1)

        @pl.when(n == 0)
        def _():
            dk_ref[...] = jnp.zeros_like(dk_ref)
            dv_ref[...] = jnp.zeros_like(dv_ref)

        nt = (((1,), (1,)), ((), ()))
        tn = (((0,), (0,)), ((), ()))
        qs = [q_ref[:, hs] for hs in B_HEADS]
        dos = [do_ref[:, hs] for hs in B_HEADS]

        def sweep1(gi, j0, cs):
            rows, valid = _b_window(j0, n)
            lbs, weights, cs = _b_weights(qs, [k_ref[rows, hs] for hs in B_HEADS], valid, cs)
            for h in range(B_NH):
                a_s[h, gi] = weights[h]
                lb_s[h, gi] = lbs[h]
            return tuple(cs)

        _, groups, _ = lax.while_loop(
            lambda cr: jnp.logical_and(cr[0] >= 0, _b_live(cr[2])),
            lambda cr: (cr[0] - B_UNROLL, cr[1] + 1, sweep1(cr[1], cr[0], cr[2])),
            (n, 0, (jnp.zeros((BLK, 1), F32),) * B_NH))

        def sweep2(i, st):
            gi = groups - 1 - i
            rows, valid = _b_window(n - gi * B_UNROLL, n)
            kws, vws = [k_ref[rows, hs] for hs in B_HEADS], [v_ref[rows, hs] for hs in B_HEADS]
            weights, lbs = [a_s[h, gi] for h in range(B_NH)], [lb_s[h, gi] for h in range(B_NH)]
            gs = [a * lax.dot_general(do, vw, nt, preferred_element_type=F32)
                  for a, do, vw in zip(weights, dos, vws)]
            befores, totals = _cumsum_excl(gs, later=False)
            dzbs = []
            for (cg, _), g, before, lb in zip(st, gs, befores, lbs):
                dz = jnp.where(valid, g - (g + cg + before) * jnp.exp(lb), 0.0)
                dzbs.append((dz * SCALE).astype(BF16))
            for h, hs in enumerate(B_HEADS):
                dk_ref[rows, hs] += lax.dot_general(dzbs[h], qs[h], tn, preferred_element_type=F32)
                dv_ref[rows, hs] += lax.dot_general(weights[h].astype(BF16), dos[h], tn, preferred_element_type=F32)
            return tuple((cg + total, dq + jnp.dot(dzb, kw, preferred_element_type=F32))
                         for (cg, dq), total, dzb, kw in zip(st, totals, dzbs, kws))

        zero = (jnp.zeros((BLK, 1), F32), jnp.zeros((BLK, HD), F32))
        st = lax.fori_loop(0, groups, sweep2, (zero,) * B_NH)
        for h, hs in enumerate(B_HEADS):
            dq_ref[:, hs] = st[h][1].astype(BF16)

    tile = pl.BlockSpec((BLK, B_NH * HD), lambda p, n: (n, p))
    full = lambda off: pl.BlockSpec((T, B_NH * HD), lambda p, n: (0, off + p))
    return _pcall(
        name, body, grid=(npair, nb),
        in_specs=[tile, tile, full(npair), full(2 * npair)],
        out_specs=[tile, full(0), full(0)],
        out_shape=[jax.ShapeDtypeStruct((T, NH_B * HD), BF16), jax.ShapeDtypeStruct((T, NH_B * HD), F32),
                   jax.ShapeDtypeStruct((T, NH_B * HD), F32)],
        scratch_shapes=[pltpu.VMEM((B_NH, B_GROUPS, BLK, B_WIN), F32)] * 2,
        sem=("parallel", "arbitrary"), args=(qkv, do, qkv, qkv), comm=comm)


def _adamw_math(w, g, m, v):
    m = B1 * m + (1.0 - B1) * g
    v = B2 * v + (1.0 - B2) * (g * g)
    m_hat = m / (1.0 - B1 ** STEP)
    v_hat = v / (1.0 - B2 ** STEP)
    delta = -LR * (m_hat / (jnp.sqrt(v_hat) + EPS) + WD * w)
    return delta, m, v


def adamw(name, parts, w, m, v, rows, layer=0, prev=None):
    nl, r, c = w.shape

    def body(p_ref, w_ref, m_ref, v_ref, *rest):
        g_ref, d_ref, nm_ref, nv_ref = rest[-4:]
        g = p_ref[0].astype(F32)
        for s in range(1, NDEV):
            g = g + p_ref[s].astype(F32)
        delta, nm, nv = _adamw_math(w_ref[...], g, m_ref[...], v_ref[...])
        g_ref[...] = g
        d_ref[...] = delta
        nm_ref[...] = nm
        nv_ref[...] = nv

    blk = pl.BlockSpec((None, rows, c), lambda i: (layer, i, 0))
    carried = [] if prev is None else list(prev)
    return pl.pallas_call(
        body, grid=(r // rows,),
        in_specs=[pl.BlockSpec((NDEV, rows, c), lambda i: (0, i, 0)), blk, blk, blk]
        + [pl.BlockSpec(memory_space=pl.ANY)] * len(carried),
        out_specs=[blk] * 4, out_shape=[jax.ShapeDtypeStruct((nl, r, c), F32)] * 4,
        input_output_aliases={4 + q: q for q in range(len(carried))},
        compiler_params=_cp(("parallel",)), name=name)(parts, w, m, v, *carried)


def _unshard_cols(g):
    return jnp.transpose(g, (1, 0, 2)).reshape(g.shape[1], -1)


def _shard_cols(w):
    k, n = w.shape
    return jnp.transpose(w.reshape(k, NDEV, n // NDEV), (1, 0, 2))


FWD_CARRY = {
    ("qkv", 0): [(0, 1)], ("attn", 0): [(0, 2)], ("mlp_in", 0): [(0, 3)], ("mlp_out", 0): [(1, 0)],
    ("qkv", 1): [(1, 1)], ("attn", 1): [(1, 2), (1, 3), (2, 0), (2, 1)],
    ("mlp_in", 1): [(2, 2)], ("mlp_out", 1): [(3, 1)],
    ("attn", 2): [(2, 3)], ("mlp_in", 2): [(3, 0)],
    ("attn", 3): [(3, 2), (3, 3)],
}
BWD_CARRY = {
    ("attn_bwd", 3): [(3, 3), (3, 2), (3, 1)],
    ("g_mlp_out", 2): [(3, 0, 0)], ("d_act", 2): [(3, 0, 1)], ("g_mlp_in", 2): [(2, 3, 0)],
    ("d_h2", 2): [(2, 3, 1)], ("attn_bwd", 2): [(2, 2)], ("d_h1", 2): [(2, 1)],
    ("g_mlp_out", 1): [(2, 0)], ("attn_bwd", 1): [(1, 3), (1, 2), (1, 1)],
    ("g_mlp_out", 0): [(1, 0, 0)], ("d_act", 0): [(1, 0, 1)], ("g_mlp_in", 0): [(0, 3, 0)],
    ("d_h2", 0): [(0, 3, 1)], ("attn_bwd", 0): [(0, 2)], ("g_qkv", 0): [(0, 1)], ("d_h1", 0): [(0, 0)],
}


def kernel(x, a_w_qkv, a_w_o, a_sinks, b_w_qkv, b_w_o, norm_mix, norm_mlp, mlp_w_in, mlp_w_out, final_norm, loss_target, m_a_w_qkv, m_a_w_o, m_a_sinks, m_b_w_qkv, m_b_w_o, m_norm_mix, m_norm_mlp, m_mlp_w_in, m_mlp_w_out, m_final_norm, v_a_w_qkv, v_a_w_o, v_a_sinks, v_b_w_qkv, v_b_w_o, v_norm_mix, v_norm_mlp, v_mlp_w_in, v_mlp_w_out, v_final_norm):
    depth = 4
    xs = x[0]

    shards, full, sendbuf, recv = {}, {}, {}, {}
    for i in range(depth):
        j = i // 2
        wq, wo = (a_w_qkv[j], a_w_o[j]) if i % 2 == 0 else (b_w_qkv[j], b_w_o[j])
        for t, w in enumerate((wq, wo, mlp_w_in[i], mlp_w_out[i])):
            shards[(i, t)] = w.astype(BF16)

    def land_weights(keys, slabs):
        for (i, t), g in zip(keys, slabs):
            if t in (1, 3):
                full[(i, t)] = g.reshape(-1, D)
            else:
                full[(i, t)] = g if g.shape[2] % 128 == 0 else _unshard_cols(g)

    def fwd(kind, i, fn):
        keys = FWD_CARRY.get((kind, i), [])
        out, got = fn(Comm([shards[k] for k in keys], True) if keys else None)
        land_weights(keys, got)
        return out

    def bwd(kind, i, fn):
        keys = BWD_CARRY.get((kind, i), [])
        bufs = [sendbuf[key[:2]] for key in keys]
        halves = [None if len(key) == 2 else (key[2] * (b.shape[1] // 2), b.shape[1] // 2) for key, b in zip(keys, bufs)]
        out, got = fn(Comm(bufs, False, halves, [recv.get(key[:2]) for key in keys]) if keys else None)
        recv.update((key[:2], g) for key, g in zip(keys, got))
        return out

    relu2 = lambda acc: (acc, jnp.square(jnp.maximum(acc, 0.0)))
    add = lambda acc, res: (acc + res,)
    drelu2 = lambda acc, uu: (acc * (2.0 * jnp.maximum(uu.astype(F32), 0.0)),)

    land_weights([(0, 0)], exchange("gather_first", [shards[(0, 0)]], True))
    saved = []
    for i in range(depth):
        x1 = xs
        h1 = rmsnorm_fwd(f"norm_mix{i}", x1, norm_mix[i][None])
        (qkv,) = fwd("qkv", i, lambda c: mm_nn(f"qkv{i}", h1, full[(i, 0)], out_dtype=BF16, tn=512, comm=c))
        if i % 2 == 0:
            o = fwd("attn", i, lambda c: attn_a_fwd(f"attn_a{i}", qkv, a_sinks[i // 2], comm=c))
        else:
            o = fwd("attn", i, lambda c: attn_b_fwd(f"attn_b{i}", qkv, comm=c))
        (x2,) = mm_nn(f"proj{i}", o, full[(i, 1)], out_dtype=F32, extras=(x1,), epilogue=add)[0]
        h2 = rmsnorm_fwd(f"norm_mlp{i}", x2, norm_mlp[i][None])
        u, act = fwd("mlp_in", i, lambda c: mm_nn(f"mlp_in{i}", h2, full[(i, 2)], out_dtype=BF16,
                                                   epilogue=relu2, n_out=2, comm=c))
        (xs,) = fwd("mlp_out", i, lambda c: mm_nn(f"mlp_out{i}", act, full[(i, 3)], out_dtype=F32,
                                                   extras=(x2,), epilogue=add, comm=c))
        saved.append((x1, h1, qkv, o, x2, h2, u, act))

    dx, dxb, d_final, loss_part = loss_head("loss_head", xs, final_norm[None], loss_target[0])
    loss = lax.psum(loss_part[0, 0], ("x", "y", "c"))

    d_mix, d_mlp, d_sinks = [None] * depth, [None] * depth, [None] * 2
    for i in reversed(range(depth)):
        x1, h1, qkv, o, x2, h2, u, act = saved[i]
        sendbuf[(i, 3)] = bwd("g_mlp_out", i, lambda c: mm_tn(f"g_mlp_out{i}", act, dxb, comm=c)).reshape(
            NDEV, DFF // NDEV, D)
        du = bwd("d_act", i, lambda c: mm_nt(f"d_act{i}", dxb, full[(i, 3)], out_dtype=BF16, extras=(u,),
                                             epilogue=drelu2, comm=c))
        sendbuf[(i, 2)] = bwd("g_mlp_in", i, lambda c: mm_tn(f"g_mlp_in{i}", h2, du, col_shards=True, comm=c))
        dh2 = bwd("d_h2", i, lambda c: mm_nt(f"d_h2{i}", du, full[(i, 2)], out_dtype=F32, comm=c))
        dx, dxb, d_mlp[i] = rmsnorm_bwd(f"norm_mlp_bwd{i}", dh2, x2, norm_mlp[i][None], dx)
        sendbuf[(i, 1)] = mm_tn(f"g_proj{i}", o, dxb)[0].reshape(NDEV, D // NDEV, D)
        do = mm_nt(f"d_o{i}", dxb, full[(i, 1)], out_dtype=BF16)[0]
        if i % 2 == 0:
            dq, dk, dv, ds = bwd("attn_bwd", i, lambda c: attn_a_bwd(f"attn_a_bwd{i}", qkv, a_sinks[i // 2],
                                                                       o, do, comm=c))
            d_sinks[i // 2] = ds[0, :NQ_A]
            dqkv = jnp.concatenate([dq, dk, dv], axis=1)
        else:
            dq, dk, dv = bwd("attn_bwd", i, lambda c: attn_b_bwd(f"attn_b_bwd{i}", qkv, do, comm=c))
            dqkv = jnp.concatenate([dq, dk.astype(BF16), dv.astype(BF16)], axis=1)
        if i % 2 == 0:
            sendbuf[(i, 0)] = _shard_cols(bwd("g_qkv", i, lambda c: mm_tn(f"g_qkv{i}", h1, dqkv, tn=512, comm=c)))
        else:
            sendbuf[(i, 0)] = bwd("g_qkv", i, lambda c: mm_tn(f"g_qkv{i}", h1, dqkv, col_shards=True, comm=c))
        dh1 = bwd("d_h1", i, lambda c: mm_nt(f"d_h1{i}", dqkv, full[(i, 0)], out_dtype=F32, comm=c))
        dx, dxb, d_mix[i] = rmsnorm_bwd(f"norm_mix_bwd{i}", dh1, x1, norm_mix[i][None], dx)

    small = jnp.zeros((16, D), F32)
    small = small.at[0:4].set(jnp.concatenate(d_mix, axis=0)).at[4:8].set(jnp.concatenate(d_mlp, axis=0))
    small = small.at[8].set(d_final[0]).at[9, :2 * NQ_A].set(jnp.concatenate(d_sinks))
    (small_parts,) = exchange("exchange_last", [small], True)
    assert len(recv) == 4 * depth and len(full) == 4 * depth
    pack = lambda mix, mlp, fin, snk: (jnp.zeros((16, D), F32).at[0:4].set(mix).at[4:8].set(mlp)
                                       .at[8].set(fin).at[9, :2 * NQ_A].set(snk.reshape(-1)))
    sm = adamw("adamw_small", small_parts, pack(norm_mix, norm_mlp, final_norm, a_sinks)[None],
               pack(m_norm_mix, m_norm_mlp, m_final_norm, m_a_sinks)[None],
               pack(v_norm_mix, v_norm_mlp, v_final_norm, v_a_sinks)[None], rows=16)
    unpack = lambda a: (a[0, 0:4], a[0, 4:8], a[0, 8], a[0, 9, :2 * NQ_A].reshape(2, NQ_A))

    def update(kind, layers, t, w, m, v, rows):
        res = None
        for l, i in enumerate(layers):
            res = adamw(f"adamw_{kind}{i}", recv[(i, t)], w, m, v, rows, layer=l, prev=res)
        return res

    upd = [update("a_qkv", (0, 2), 0, a_w_qkv, m_a_w_qkv, v_a_w_qkv, 256),
           update("a_o", (0, 2), 1, a_w_o, m_a_w_o, v_a_w_o, 128),
           None,
           update("b_qkv", (1, 3), 0, b_w_qkv, m_b_w_qkv, v_b_w_qkv, 256),
           update("b_o", (1, 3), 1, b_w_o, m_b_w_o, v_b_w_o, 128),
           None, None,
           update("mlp_in", range(depth), 2, mlp_w_in, m_mlp_w_in, v_mlp_w_in, 256),
           update("mlp_out", range(depth), 3, mlp_w_out, m_mlp_w_out, v_mlp_w_out, 128),
           None]
    outs = []
    for q in range(4):
        mix, mlp, fin, snk = unpack(sm[q])
        small_out = {2: snk, 5: mix, 6: mlp, 9: fin}
        outs.append([small_out[k] if u is None else u[q] for k, u in enumerate(upd)])
    return (loss, dx[None], *outs[0], *outs[1], *outs[2], *outs[3])
```

```python
import functools
import math

import jax
import jax.numpy as jnp
from jax import lax
from jax.experimental import pallas as pl
from jax.experimental.pallas import tpu as pltpu

F32, BF16 = jnp.float32, jnp.bfloat16
T, D, DFF = 4096, 2048, 8192
HD, BLK = 64, 128
NQ_A, NKV_A, GROUP_A = 32, 4, 8
NH_B = 32
NDEV = 8
RMS_EPS = 1e-5
SCALE = 0.125
NEG = -1e30
VMEM_LIMIT = 56 * 1024 * 1024
LR, B1, B2, EPS, WD, STEP = 0.001, 0.9, 0.999, 1e-08, 0.01, 10
MESH = pl.DeviceIdType.MESH


def _cp(sem, vmem=VMEM_LIMIT):
    return pltpu.CompilerParams(dimension_semantics=sem, vmem_limit_bytes=vmem)


class Comm:
    def __init__(self, ins, gather, rows=None, into=None):
        self.ins, self.nt = list(ins), len(ins)
        self.gather = list(gather) if isinstance(gather, (list, tuple)) else [gather] * self.nt
        self.rows = rows or [None] * self.nt
        self.into = into or [None] * self.nt
        self.out_shape = [jax.ShapeDtypeStruct((NDEV,) + (a.shape if g else a.shape[1:]), a.dtype)
                          for a, g in zip(self.ins, self.gather)]
        self.scratch = [pltpu.SemaphoreType.DMA((7 * self.nt,)), pltpu.SemaphoreType.DMA((7 * self.nt,)),
                        pltpu.SemaphoreType.DMA((self.nt,))]

    def _copies(self, src, dst, sems, *kinds):
        send_sems, recv_sems, local_sems = sems
        x, y, c = lax.axis_index("x"), lax.axis_index("y"), lax.axis_index("c")
        me, sib = (x, y, c), (x, y, 1 - c)
        chips = [(1 - x, y), (x, 1 - y), (1 - x, 1 - y)]
        p = {kind: [] for kind in kinds}
        for t in range(self.nt):
            part = (slice(None),) if self.rows[t] is None else (pl.ds(*self.rows[t]),)

            def slot(d, t=t, part=part):
                return dst[t].at[(4 * d[0] + 2 * d[1] + d[2],) + part]

            def add(kind, k, a, b, to, t=t):
                if kind in p:
                    p[kind].append(pltpu.make_async_remote_copy(
                        src_ref=a, dst_ref=b, send_sem=send_sems.at[t * 7 + k],
                        recv_sem=recv_sems.at[t * 7 + k], device_id=to, device_id_type=MESH))

            own = src[t] if self.gather[t] else src[t].at[(4 * x + 2 * y + c,) + part]
            if "local" in p:
                p["local"].append(pltpu.make_async_copy(own, slot(me), local_sems.at[t]))
            if self.gather[t]:
                add("first", 0, own, slot(me), sib)
                add("first_in", 0, own, slot(sib), sib)
                for j, chip in enumerate(chips):
                    there = (*chip, c)
                    add("first", 1 + j, own, slot(me), there)
                    add("chip_in", 1 + j, own, slot(there), there)
                    add("relay", 4 + j, slot(there), slot(there), sib)
                    add("relay_in", 4 + j, own, slot((*chip, 1 - c)), sib)
            else:
                for k in range(NDEV - 1):
                    bits = k + 1
                    peer = (1 - x if bits & 4 else x, 1 - y if bits & 2 else y, 1 - c if bits & 1 else c)
                    out = src[t].at[(4 * peer[0] + 2 * peer[1] + peer[2],) + part]
                    add("first", k, out, slot(me), peer)
                    add("first_in", k, out, slot(peer), peer)
        return [p[kind] for kind in kinds]

    def start(self, src, dst, sems):
        local, first = self._copies(src, dst, sems, "local", "first")
        for cp in local + first:
            cp.start()

    def relay(self, src, dst, sems):
        chip_in, relay = self._copies(src, dst, sems, "chip_in", "relay")
        for arrived, onward in zip(chip_in, relay):
            arrived.wait_recv()
            onward.start()

    def finish(self, src, dst, sems):
        first_in, relay_in, first, relay, local = self._copies(
            src, dst, sems, "first_in", "relay_in", "first", "relay", "local")
        for cp in first_in + relay_in:
            cp.wait_recv()
        for cp in first + relay:
            cp.wait_send()
        for cp in local:
            cp.wait()


def _pcall(name, body, *, grid, in_specs, out_specs, out_shape, scratch_shapes=(), sem, args, comm=None):
    in_specs, out_specs, out_shape = list(in_specs), list(out_specs), list(out_shape)
    scratch_shapes = list(scratch_shapes)
    if comm is None:
        outs = pl.pallas_call(body, grid=grid, in_specs=in_specs, out_specs=out_specs, out_shape=out_shape,
                              scratch_shapes=scratch_shapes, compiler_params=_cp(sem), name=name)(*args)
        return list(outs), []
    ni, no, ns, nt = len(in_specs), len(out_specs), len(scratch_shapes), comm.nt
    steps = math.prod(grid)
    hbm = pl.BlockSpec(memory_space=pltpu.HBM)
    landing = [t for t in range(nt) if comm.into[t] is not None]

    def carrier(*refs):
        ins, cin = refs[:ni], refs[ni:ni + nt]
        refs = refs[:ni + nt] + refs[ni + nt + len(landing):]
        outs, cout = refs[ni + nt:ni + nt + no], refs[ni + nt + no:ni + 2 * nt + no]
        scr, sems = refs[ni + 2 * nt + no:ni + 2 * nt + no + ns], refs[ni + 2 * nt + no + ns:]
        step = functools.reduce(lambda acc, d: acc * grid[d] + pl.program_id(d), range(len(grid)), 0)

        @pl.when(step == 0)
        def _():
            comm.start(cin, cout, sems)

        body(*ins, *outs, *scr)

        @pl.when(step == (7 * steps) // 8)
        def _():
            comm.relay(cin, cout, sems)

        @pl.when(step == steps - 1)
        def _():
            comm.finish(cin, cout, sems)

    outs = pl.pallas_call(
        carrier, grid=grid, in_specs=in_specs + [hbm] * (nt + len(landing)), out_specs=out_specs + [hbm] * nt,
        out_shape=out_shape + comm.out_shape, scratch_shapes=scratch_shapes + comm.scratch,
        input_output_aliases={ni + nt + a: no + t for a, t in enumerate(landing)},
        compiler_params=_cp(("arbitrary",) * len(grid)), name=name)(*args, *comm.ins, *[comm.into[t] for t in landing])
    return list(outs[:no]), list(outs[no:])


def exchange(name, ins, gather):
    comm = Comm(ins, gather)
    hbm = pl.BlockSpec(memory_space=pltpu.HBM)

    def body(*refs):
        src, dst, sems = refs[:comm.nt], refs[comm.nt:2 * comm.nt], refs[2 * comm.nt:]
        comm.start(src, dst, sems)
        comm.relay(src, dst, sems)
        comm.finish(src, dst, sems)

    return pl.pallas_call(body, in_specs=[hbm] * comm.nt, out_specs=[hbm] * comm.nt, out_shape=comm.out_shape,
                          scratch_shapes=comm.scratch, name=name)(*ins)


def _matmul(name, a, b, extras, *, grid, a_spec, b_spec, extra_specs, out_shapes, out_specs,
            contract, epilogue, acc_shape, comm=None):
    nk = grid[2]
    ne, no = len(extras), len(out_shapes)

    def body(a_ref, b_ref, *rest):
        ex, outs, acc = rest[:ne], rest[ne:ne + no], rest[ne + no]
        kk = pl.program_id(2)

        @pl.when(kk == 0)
        def _():
            acc[...] = jnp.zeros_like(acc)

        acc[...] += lax.dot_general(a_ref[...].astype(BF16), b_ref[...].astype(BF16),
                                    (contract, ((), ())), preferred_element_type=F32)

        @pl.when(kk == nk - 1)
        def _():
            res = epilogue(acc[...], *[r[...] for r in ex])
            for r, o in zip(outs, res):
                r[...] = o.astype(r.dtype)

    return _pcall(name, body, grid=grid, in_specs=[a_spec, b_spec, *extra_specs], out_specs=out_specs,
                  out_shape=out_shapes, scratch_shapes=[pltpu.VMEM(acc_shape, F32)],
                  sem=("parallel", "parallel", "arbitrary"), args=(a, b, *extras), comm=comm)


def _ident(acc):
    return (acc,)


def _fit(tile, dim):
    return max(t for t in range(128, min(tile, dim) + 1, 128) if dim % t == 0)


def mm_nn(name, a, w, *, out_dtype, tm=1024, tn=1024, tk=2048, extras=(), epilogue=_ident, n_out=1,
          out_dtypes=None, comm=None):
    m, k = a.shape
    tm, tk = _fit(tm, m), _fit(tk, k)
    if w.ndim == 3:
        tn = w.shape[2]
        n = NDEV * tn
        b_spec = pl.BlockSpec((None, tk, tn), lambda i, j, kk: (j, kk, 0))
    else:
        n = w.shape[1]
        tn = _fit(tn, n)
        b_spec = pl.BlockSpec((tk, tn), lambda i, j, kk: (kk, j))
    out_dtypes = out_dtypes or (out_dtype,) * n_out
    mn = pl.BlockSpec((tm, tn), lambda i, j, kk: (i, j))
    return _matmul(name, a, w, extras, grid=(m // tm, n // tn, k // tk),
                   a_spec=pl.BlockSpec((tm, tk), lambda i, j, kk: (i, kk)), b_spec=b_spec,
                   extra_specs=[mn] * len(extras),
                   out_shapes=[jax.ShapeDtypeStruct((m, n), dt) for dt in out_dtypes],
                   out_specs=[mn] * len(out_dtypes), contract=((1,), (0,)), epilogue=epilogue,
                   acc_shape=(tm, tn), comm=comm)


def mm_nt(name, a, w, *, out_dtype, tm=1024, tn=1024, tk=2048, extras=(), epilogue=_ident, comm=None):
    m, n = a.shape
    k = w.shape[-2]
    tm, to = _fit(tm, m), _fit(tn, k)
    if w.ndim == 3:
        tc = w.shape[2]
        b_spec = pl.BlockSpec((None, to, tc), lambda i, j, kk: (kk, j, 0))
    else:
        tc = _fit(tk, n)
        b_spec = pl.BlockSpec((to, tc), lambda i, j, kk: (j, kk))
    mo = pl.BlockSpec((tm, to), lambda i, j, kk: (i, j))
    outs, got = _matmul(name, a, w, extras, grid=(m // tm, k // to, n // tc),
                        a_spec=pl.BlockSpec((tm, tc), lambda i, j, kk: (i, kk)), b_spec=b_spec,
                        extra_specs=[mo] * len(extras),
                        out_shapes=[jax.ShapeDtypeStruct((m, k), out_dtype)],
                        out_specs=[mo], contract=((1,), (1,)), epilogue=epilogue, acc_shape=(tm, to),
                        comm=comm)
    return outs[0], got


def mm_tn(name, a, b, *, out_dtype=BF16, tm=1024, tn=1024, tk=2048, col_shards=False, comm=None):
    t, k = a.shape
    n = b.shape[1]
    tm, tk = _fit(tm, k), _fit(tk, t)
    if col_shards:
        tn = n // NDEV
        shape, spec = (NDEV, k, tn), pl.BlockSpec((None, tm, tn), lambda i, j, kk: (j, i, 0))
    else:
        tn = _fit(tn, n)
        shape, spec = (k, n), pl.BlockSpec((tm, tn), lambda i, j, kk: (i, j))
    outs, got = _matmul(name, a, b, (), grid=(k // tm, n // tn, t // tk),
                        a_spec=pl.BlockSpec((tk, tm), lambda i, j, kk: (kk, i)),
                        b_spec=pl.BlockSpec((tk, tn), lambda i, j, kk: (kk, j)),
                        extra_specs=[], out_shapes=[jax.ShapeDtypeStruct(shape, out_dtype)],
                        out_specs=[spec],
                        contract=((0,), (0,)), epilogue=_ident, acc_shape=(tm, tn), comm=comm)
    return outs[0], got


ROWS = 256


def rmsnorm_fwd(name, x, gain):
    def body(x_ref, g_ref, h_ref):
        xf = x_ref[...]
        rstd = lax.rsqrt(jnp.mean(xf * xf, axis=-1, keepdims=True) + RMS_EPS)
        h_ref[...] = (xf * rstd * g_ref[...]).astype(BF16)

    row = pl.BlockSpec((ROWS, D), lambda i: (i, 0))
    return pl.pallas_call(body, grid=(T // ROWS,), in_specs=[row, pl.BlockSpec((1, D), lambda i: (0, 0))],
                          out_specs=row, out_shape=jax.ShapeDtypeStruct((T, D), BF16),
                          compiler_params=_cp(("parallel",)), name=name)(x, gain)


def _rms_bwd_math(dh, xf, gain):
    rstd = lax.rsqrt(jnp.mean(xf * xf, axis=-1, keepdims=True) + RMS_EPS)
    xhat = xf * rstd
    dgain = jnp.sum(dh * xhat, axis=0, keepdims=True)
    dxhat = dh * gain
    dx = rstd * (dxhat - xhat * jnp.mean(dxhat * xhat, axis=-1, keepdims=True))
    return dx, dgain


def rmsnorm_bwd(name, dh, x, gain, dres):
    def body(dh_ref, x_ref, g_ref, dres_ref, dx_ref, dxb_ref, dg_ref):
        dx, dgain = _rms_bwd_math(dh_ref[...], x_ref[...], g_ref[...])
        dx = dx + dres_ref[...]
        dx_ref[...] = dx
        dxb_ref[...] = dx.astype(BF16)

        @pl.when(pl.program_id(0) == 0)
        def _():
            dg_ref[...] = jnp.zeros_like(dg_ref)

        dg_ref[...] += dgain

    row = pl.BlockSpec((ROWS, D), lambda i: (i, 0))
    vec = pl.BlockSpec((1, D), lambda i: (0, 0))
    return pl.pallas_call(
        body, grid=(T // ROWS,), in_specs=[row, row, vec, row], out_specs=[row, row, vec],
        out_shape=[jax.ShapeDtypeStruct((T, D), F32), jax.ShapeDtypeStruct((T, D), BF16),
                   jax.ShapeDtypeStruct((1, D), F32)],
        compiler_params=_cp(("arbitrary",)), name=name)(dh, x, gain, dres)


def loss_head(name, x, gain, target):
    def body(x_ref, g_ref, t_ref, dx_ref, dxb_ref, dg_ref, loss_ref):
        xf, gain = x_ref[...], g_ref[...]
        rstd = lax.rsqrt(jnp.mean(xf * xf, axis=-1, keepdims=True) + RMS_EPS)
        err = xf * rstd * gain - t_ref[...]
        part = 0.5 * jnp.sum(jnp.mean(err * err, axis=-1, keepdims=True), axis=0, keepdims=True)
        dx, dgain = _rms_bwd_math(err * (1.0 / D), xf, gain)
        dx_ref[...] = dx
        dxb_ref[...] = dx.astype(BF16)

        @pl.when(pl.program_id(0) == 0)
        def _():
            dg_ref[...] = jnp.zeros_like(dg_ref)
            loss_ref[...] = jnp.zeros_like(loss_ref)

        dg_ref[...] += dgain
        loss_ref[...] += jnp.broadcast_to(part, loss_ref.shape)

    row = pl.BlockSpec((ROWS, D), lambda i: (i, 0))
    vec = pl.BlockSpec((1, D), lambda i: (0, 0))
    return pl.pallas_call(
        body, grid=(T // ROWS,), in_specs=[row, vec, row],
        out_specs=[row, row, vec, pl.BlockSpec((1, 128), lambda i: (0, 0))],
        out_shape=[jax.ShapeDtypeStruct((T, D), F32), jax.ShapeDtypeStruct((T, D), BF16),
                   jax.ShapeDtypeStruct((1, D), F32), jax.ShapeDtypeStruct((1, 128), F32)],
        compiler_params=_cp(("arbitrary",)), name=name)(x, gain, target)


def _a_common(n, g, s_ref, q_ref, kc_ref, kp_ref):
    row = lax.broadcasted_iota(jnp.int32, (GROUP_A * BLK, BLK), 0)
    col = lax.broadcasted_iota(jnp.int32, (GROUP_A * BLK, BLK), 1)
    qi = row & (BLK - 1)
    dist_c = (qi - col).astype(F32)
    valid_c = qi >= col
    valid_p = jnp.logical_and(col > qi, n > 0)
    hidx = lax.broadcasted_iota(jnp.int32, (GROUP_A * BLK, 1), 0) >> 7
    slope = jnp.exp((-math.log(2.0) / 4.0) * (hidx + (GROUP_A * g + 1)).astype(F32))
    sink = jnp.zeros((GROUP_A * BLK, 1), F32)
    for i in range(GROUP_A):
        sink = jnp.where(hidx == i, s_ref[GROUP_A * g + i], sink)
    qg = jnp.concatenate([q_ref[:, (GROUP_A * g + i) * HD:(GROUP_A * g + i + 1) * HD]
                          for i in range(GROUP_A)], axis=0)
    kc = kc_ref[:, g * HD:(g + 1) * HD]
    kp = kp_ref[:, g * HD:(g + 1) * HD]
    nt = (((1,), (1,)), ((), ()))
    sc = lax.dot_general(qg, kc, nt, preferred_element_type=F32) * SCALE - slope * dist_c
    sp = lax.dot_general(qg, kp, nt, preferred_element_type=F32) * SCALE - slope * (dist_c + float(BLK))
    sc = jnp.where(valid_c, sc, NEG)
    sp = jnp.where(valid_p, sp, NEG)
    m = jnp.maximum(jnp.maximum(jnp.max(sc, axis=-1, keepdims=True), jnp.max(sp, axis=-1, keepdims=True)), sink)
    pc = jnp.exp(sc - m)
    pp = jnp.exp(sp - m)
    ps = jnp.exp(sink - m)
    den = jnp.sum(pc, axis=-1, keepdims=True) + jnp.sum(pp, axis=-1, keepdims=True) + ps
    return qg, kc, kp, pc, pp, ps, den, hidx


def attn_a_fwd(name, qkv, sinks, comm=None):
    kcol, vcol = NQ_A * HD // (NKV_A * HD), NQ_A * HD // (NKV_A * HD) + 1

    def body(s_ref, q_ref, kc_ref, kp_ref, vc_ref, vp_ref, o_ref):
        n = pl.program_id(0)
        for g in range(NKV_A):
            qg, kc, kp, pc, pp, ps, den, _ = _a_common(n, g, s_ref, q_ref, kc_ref, kp_ref)
            vc = vc_ref[:, g * HD:(g + 1) * HD]
            vp = vp_ref[:, g * HD:(g + 1) * HD]
            og = (jnp.dot(pc.astype(BF16), vc, preferred_element_type=F32)
                  + jnp.dot(pp.astype(BF16), vp, preferred_element_type=F32)) / den
            for i in range(GROUP_A):
                h = GROUP_A * g + i
                o_ref[:, h * HD:(h + 1) * HD] = og[i * BLK:(i + 1) * BLK].astype(BF16)

    kvw = NKV_A * HD
    prev = lambda n: jnp.maximum(n - 1, 0)
    outs, got = _pcall(
        name, body, grid=(T // BLK,),
        in_specs=[pl.BlockSpec(memory_space=pltpu.SMEM),
                  pl.BlockSpec((BLK, NQ_A * HD), lambda n: (n, 0)),
                  pl.BlockSpec((BLK, kvw), lambda n: (n, kcol)),
                  pl.BlockSpec((BLK, kvw), lambda n: (prev(n), kcol)),
                  pl.BlockSpec((BLK, kvw), lambda n: (n, vcol)),
                  pl.BlockSpec((BLK, kvw), lambda n: (prev(n), vcol))],
        out_specs=[pl.BlockSpec((BLK, NQ_A * HD), lambda n: (n, 0))],
        out_shape=[jax.ShapeDtypeStruct((T, NQ_A * HD), BF16)],
        sem=("parallel",), args=(sinks, qkv, qkv, qkv, qkv, qkv), comm=comm)
    return outs[0], got


def attn_a_bwd(name, qkv, sinks, o, do, comm=None):
    kcol, vcol = NQ_A * HD // (NKV_A * HD), NQ_A * HD // (NKV_A * HD) + 1
    nb = T // BLK
    kvw = NKV_A * HD

    def body(s_ref, q_ref, kc_ref, kp_ref, vc_ref, vp_ref, o_ref, do_ref,
             dq_ref, dk_ref, dv_ref, ds_ref, ck_ref, cv_ref):
        n = pl.program_id(0)

        @pl.when(n == 0)
        def _():
            ds_ref[...] = jnp.zeros_like(ds_ref)
            ck_ref[...] = jnp.zeros_like(ck_ref)
            cv_ref[...] = jnp.zeros_like(cv_ref)

        @pl.when(n == nb)
        def _():
            dk_ref[...] = ck_ref[...].astype(BF16)
            dv_ref[...] = cv_ref[...].astype(BF16)

        @pl.when(n < nb)
        def _():
            lane = lax.broadcasted_iota(jnp.int32, (1, 128), 1)
            dsink_row = jnp.zeros((1, 128), F32)
            tn = (((0,), (0,)), ((), ()))
            nt = (((1,), (1,)), ((), ()))
            for g in range(NKV_A):
                qg, kc, kp, pc, pp, ps, den, hidx = _a_common(n, g, s_ref, q_ref, kc_ref, kp_ref)
                vc = vc_ref[:, g * HD:(g + 1) * HD]
                vp = vp_ref[:, g * HD:(g + 1) * HD]
                cols = [slice((GROUP_A * g + i) * HD, (GROUP_A * g + i + 1) * HD) for i in range(GROUP_A)]
                og = jnp.concatenate([o_ref[:, c] for c in cols], axis=0)
                dog = jnp.concatenate([do_ref[:, c] for c in cols], axis=0)
                delta = jnp.sum(og.astype(F32) * dog.astype(F32), axis=-1, keepdims=True)
                inv = 1.0 / den
                p_c, p_p = pc * inv, pp * inv
                dsc = p_c * (lax.dot_general(dog, vc, nt, preferred_element_type=F32) - delta)
                dsp = p_p * (lax.dot_general(dog, vp, nt, preferred_element_type=F32) - delta)
                dsk = -(ps * inv) * delta
                for i in range(GROUP_A):
                    tot = jnp.sum(jnp.where(hidx == i, dsk, 0.0), axis=0, keepdims=True)
                    dsink_row = dsink_row + jnp.where(lane == GROUP_A * g + i, tot, 0.0)
                dscb, dspb = (dsc * SCALE).astype(BF16), (dsp * SCALE).astype(BF16)
                dqg = (jnp.dot(dscb, kc, preferred_element_type=F32)
                       + jnp.dot(dspb, kp, preferred_element_type=F32))
                for i in range(GROUP_A):
                    dq_ref[:, cols[i]] = dqg[i * BLK:(i + 1) * BLK].astype(BF16)
                gs = slice(g * HD, (g + 1) * HD)
                dk_ref[:, gs] = (ck_ref[:, gs] + lax.dot_general(dspb, qg, tn, preferred_element_type=F32)).astype(BF16)
                dv_ref[:, gs] = (cv_ref[:, gs] + lax.dot_general(p_p.astype(BF16), dog, tn, preferred_element_type=F32)).astype(BF16)
                ck_ref[:, gs] = lax.dot_general(dscb, qg, tn, preferred_element_type=F32)
                cv_ref[:, gs] = lax.dot_general(p_c.astype(BF16), dog, tn, preferred_element_type=F32)
            ds_ref[...] += jnp.broadcast_to(dsink_row, ds_ref.shape)

    cur = lambda n: jnp.minimum(n, nb - 1)
    prev = lambda n: jnp.maximum(jnp.minimum(n, nb - 1) - 1, 0)
    outp = lambda n: jnp.maximum(n - 1, 0)
    qspec = pl.BlockSpec((BLK, NQ_A * HD), lambda n: (cur(n), 0))
    return _pcall(
        name, body, grid=(nb + 1,),
        in_specs=[pl.BlockSpec(memory_space=pltpu.SMEM), qspec,
                  pl.BlockSpec((BLK, kvw), lambda n: (cur(n), kcol)),
                  pl.BlockSpec((BLK, kvw), lambda n: (prev(n), kcol)),
                  pl.BlockSpec((BLK, kvw), lambda n: (cur(n), vcol)),
                  pl.BlockSpec((BLK, kvw), lambda n: (prev(n), vcol)),
                  qspec, qspec],
        out_specs=[qspec, pl.BlockSpec((BLK, kvw), lambda n: (outp(n), 0)),
                   pl.BlockSpec((BLK, kvw), lambda n: (outp(n), 0)),
                   pl.BlockSpec((8, 128), lambda n: (0, 0))],
        out_shape=[jax.ShapeDtypeStruct((T, NQ_A * HD), BF16), jax.ShapeDtypeStruct((T, kvw), BF16),
                   jax.ShapeDtypeStruct((T, kvw), BF16), jax.ShapeDtypeStruct((8, 128), F32)],
        scratch_shapes=[pltpu.VMEM((BLK, kvw), F32), pltpu.VMEM((BLK, kvw), F32)],
        sem=("arbitrary",), args=(sinks, qkv, qkv, qkv, qkv, qkv, o, do), comm=comm)


def _b_logs(z):
    lb = jnp.minimum(z, 0.0) - jnp.log(1.0 + jnp.exp(-jnp.abs(z)))
    return lb, lb - z


def _cumsum_excl(xs, later):
    r = lax.broadcasted_iota(jnp.int32, (BLK, BLK), 0)
    c = lax.broadcasted_iota(jnp.int32, (BLK, BLK), 1)
    tri = (r > c if later else r < c).astype(BF16)
    blocks = [slice(b * BLK, (b + 1) * BLK) for b in range(B_UNROLL)]
    pieces = []
    for x in xs:
        hi = x.astype(BF16)
        lo = (x - hi.astype(F32)).astype(BF16)
        pieces += [hi[:, b] for b in blocks] + [lo[:, b] for b in blocks]
    inside = jnp.dot(jnp.concatenate(pieces, axis=0), tri, preferred_element_type=F32)
    piece = lambda i: inside[i * BLK:(i + 1) * BLK]
    outs, totals = [], []
    for h, x in enumerate(xs):
        base = 2 * B_UNROLL * h
        sums = [jnp.sum(x[:, b], axis=-1, keepdims=True) for b in blocks]
        cols = [functools.reduce(jnp.add, sums[i + 1:] if later else sums[:i],
                                 piece(base + i) + piece(base + B_UNROLL + i)) for i in range(B_UNROLL)]
        outs.append(jnp.concatenate(cols, axis=1))
        totals.append(functools.reduce(jnp.add, sums))
    return outs, totals


B_CUT = 105.0
B_NH = 4
B_NH_FWD = 8
B_HEADS = tuple(slice(h * HD, (h + 1) * HD) for h in range(B_NH))
B_UNROLL = 3
B_WIN = B_UNROLL * BLK
B_GROUPS = -(-(T // BLK) // B_UNROLL)


def _b_window(j0, n):
    first = jnp.maximum(j0 - (B_UNROLL - 1), 0) * BLK
    row = lax.broadcasted_iota(jnp.int32, (BLK, B_WIN), 0)
    col = lax.broadcasted_iota(jnp.int32, (BLK, B_WIN), 1)
    valid = jnp.logical_and(col - row < n * BLK - first, col < (j0 + 1) * BLK - first)
    return pl.ds(pl.multiple_of(first, BLK), B_WIN), valid


def _b_weights(qs, kws, valid, cs):
    nt = (((1,), (1,)), ((), ()))
    zs = [lax.dot_general(q, kw, nt, preferred_element_type=F32) * SCALE for q, kw in zip(qs, kws)]
    logs = [_b_logs(z) for z in zs]
    lms = [jnp.where(valid, lm, 0.0) for _, lm in logs]
    afters, totals = _cumsum_excl(lms, later=True)
    weights = [jnp.where(valid, jnp.exp(lb + c + after), 0.0) for (lb, _), c, after in zip(logs, cs, afters)]
    return [lb for lb, _ in logs], weights, [c + t for c, t in zip(cs, totals)]


def _b_live(cs):
    return functools.reduce(jnp.maximum, [jnp.max(c) for c in cs]) > -B_CUT


def attn_b_fwd(name, qkv, comm=None):
    nh = B_NH_FWD
    heads = tuple(slice(h * HD, (h + 1) * HD) for h in range(nh))
    npair = NH_B // nh

    def body(q_ref, k_ref, v_ref, o_ref):
        n = pl.program_id(1)
        qs = [q_ref[:, hs] for hs in heads]

        def group(j0, st):
            rows, valid = _b_window(j0, n)
            _, weights, cs = _b_weights(qs, [k_ref[rows, hs] for hs in heads], valid, [c for c, _ in st])
            accs = [acc + jnp.dot(a.astype(BF16), v_ref[rows, hs], preferred_element_type=F32)
                    for (_, acc), a, hs in zip(st, weights, heads)]
            return tuple(zip(cs, accs))

        zero = (jnp.zeros((BLK, 1), F32), jnp.zeros((BLK, HD), F32))
        _, st = lax.while_loop(lambda cr: jnp.logical_and(cr[0] >= 0, _b_live([c for c, _ in cr[1]])),
                               lambda cr: (cr[0] - B_UNROLL, group(cr[0], cr[1])), (n, (zero,) * nh))
        for h, hs in enumerate(heads):
            o_ref[:, hs] = st[h][1].astype(BF16)

    outs, got = _pcall(
        name, body, grid=(npair, T // BLK),
        in_specs=[pl.BlockSpec((BLK, nh * HD), lambda p, n: (n, p)),
                  pl.BlockSpec((T, nh * HD), lambda p, n: (0, npair + p)),
                  pl.BlockSpec((T, nh * HD), lambda p, n: (0, 2 * npair + p))],
        out_specs=[pl.BlockSpec((BLK, nh * HD), lambda p, n: (n, p))],
        out_shape=[jax.ShapeDtypeStruct((T, NH_B * HD), BF16)],
        sem=("parallel", "arbitrary"), args=(qkv, qkv, qkv), comm=comm)
    return outs[0], got


def attn_b_bwd(name, qkv, do, comm=None):
    npair = NH_B // B_NH
    nb = T // BLK

    def body(q_ref, do_ref, k_ref, v_ref, dq_ref, dk_out, dv_out, a_s, lb_s, dk_ref, dv_ref):
        n = pl.program_id(1)

        @pl.when(n == 0)
        def _():
            dk_ref[...] = jnp.zeros_like(dk_ref)
            dv_ref[...] = jnp.zeros_like(dv_ref)

        nt = (((1,), (1,)), ((), ()))
        tn = (((0,), (0,)), ((), ()))
        qs = [q_ref[:, hs] for hs in B_HEADS]
        dos = [do_ref[:, hs] for hs in B_HEADS]

        def sweep1(gi, j0, cs):
            rows, valid = _b_window(j0, n)
            lbs, weights, cs = _b_weights(qs, [k_ref[rows, hs] for hs in B_HEADS], valid, cs)
            for h in range(B_NH):
                a_s[h, gi] = weights[h]
                lb_s[h, gi] = lbs[h]
            return tuple(cs)

        _, groups, _ = lax.while_loop(
            lambda cr: jnp.logical_and(cr[0] >= 0, _b_live(cr[2])),
            lambda cr: (cr[0] - B_UNROLL, cr[1] + 1, sweep1(cr[1], cr[0], cr[2])),
            (n, 0, (jnp.zeros((BLK, 1), F32),) * B_NH))

        def sweep2(i, st):
            gi = groups - 1 - i
            rows, valid = _b_window(n - gi * B_UNROLL, n)
            kws, vws = [k_ref[rows, hs] for hs in B_HEADS], [v_ref[rows, hs] for hs in B_HEADS]
            weights, lbs = [a_s[h, gi] for h in range(B_NH)], [lb_s[h, gi] for h in range(B_NH)]
            gs = [a * lax.dot_general(do, vw, nt, preferred_element_type=F32)
                  for a, do, vw in zip(weights, dos, vws)]
            befores, totals = _cumsum_excl(gs, later=False)
            dzbs = []
            for (cg, _), g, before, lb in zip(st, gs, befores, lbs):
                dz = jnp.where(valid, g - (g + cg + before) * jnp.exp(lb), 0.0)
                dzbs.append((dz * SCALE).astype(BF16))
            for h, hs in enumerate(B_HEADS):
                dk_ref[rows, hs] += lax.dot_general(dzbs[h], qs[h], tn, preferred_element_type=F32)
                dv_ref[rows, hs] += lax.dot_general(weights[h].astype(BF16), dos[h], tn, preferred_element_type=F32)
            return tuple((cg + total, dq + jnp.dot(dzb, kw, preferred_element_type=F32))
                         for (cg, dq), total, dzb, kw in zip(st, totals, dzbs, kws))

        zero = (jnp.zeros((BLK, 1), F32), jnp.zeros((BLK, HD), F32))
        st = lax.fori_loop(0, groups, sweep2, (zero,) * B_NH)
        for h, hs in enumerate(B_HEADS):
            dq_ref[:, hs] = st[h][1].astype(BF16)

        @pl.when(n == nb - 1)
        def _():
            dk_out[...] = dk_ref[...].astype(BF16)
            dv_out[...] = dv_ref[...].astype(BF16)

    tile = pl.BlockSpec((BLK, B_NH * HD), lambda p, n: (n, p))
    full = lambda off: pl.BlockSpec((T, B_NH * HD), lambda p, n: (0, off + p))
    return _pcall(
        name, body, grid=(npair, nb),
        in_specs=[tile, tile, full(npair), full(2 * npair)],
        out_specs=[tile, full(0), full(0)],
        out_shape=[jax.ShapeDtypeStruct((T, NH_B * HD), BF16)] * 3,
        scratch_shapes=[pltpu.VMEM((B_NH, B_GROUPS, BLK, B_WIN), F32)] * 2 + [pltpu.VMEM((T, B_NH * HD), F32)] * 2,
        sem=("parallel", "arbitrary"), args=(qkv, do, qkv, qkv), comm=comm)


def _adamw_math(w, g, m, v):
    m = B1 * m + (1.0 - B1) * g
    v = B2 * v + (1.0 - B2) * (g * g)
    m_hat = m / (1.0 - B1 ** STEP)
    v_hat = v / (1.0 - B2 ** STEP)
    delta = -LR * (m_hat / (jnp.sqrt(v_hat) + EPS) + WD * w)
    return delta, m, v


def adamw(name, parts, w, m, v, rows, layer=0, prev=None):
    nl, r, c = w.shape

    def body(p_ref, w_ref, m_ref, v_ref, *rest):
        g_ref, d_ref, nm_ref, nv_ref = rest[-4:]
        g = p_ref[0].astype(F32)
        for s in range(1, NDEV):
            g = g + p_ref[s].astype(F32)
        delta, nm, nv = _adamw_math(w_ref[...], g, m_ref[...], v_ref[...])
        g_ref[...] = g
        d_ref[...] = delta
        nm_ref[...] = nm
        nv_ref[...] = nv

    blk = pl.BlockSpec((None, rows, c), lambda i: (layer, i, 0))
    carried = [] if prev is None else list(prev)
    return pl.pallas_call(
        body, grid=(r // rows,),
        in_specs=[pl.BlockSpec((NDEV, rows, c), lambda i: (0, i, 0)), blk, blk, blk]
        + [pl.BlockSpec(memory_space=pl.ANY)] * len(carried),
        out_specs=[blk] * 4, out_shape=[jax.ShapeDtypeStruct((nl, r, c), F32)] * 4,
        input_output_aliases={4 + q: q for q in range(len(carried))},
        compiler_params=_cp(("parallel",)), name=name)(parts, w, m, v, *carried)


def _unshard_cols(g):
    return jnp.transpose(g, (1, 0, 2)).reshape(g.shape[1], -1)


def _shard_cols(w):
    k, n = w.shape
    return jnp.transpose(w.reshape(k, NDEV, n // NDEV), (1, 0, 2))


FWD_CARRY = {
    ("qkv", 0): [(0, 1)], ("attn", 0): [(0, 2)], ("mlp_in", 0): [(0, 3)], ("mlp_out", 0): [(1, 0)],
    ("qkv", 1): [(1, 1)], ("attn", 1): [(1, 2), (1, 3), (2, 0), (2, 1)],
    ("mlp_in", 1): [(2, 2)], ("mlp_out", 1): [(3, 1)],
    ("attn", 2): [(2, 3)], ("mlp_in", 2): [(3, 0)],
    ("attn", 3): [(3, 2), (3, 3)],
}
BWD_CARRY = {
    ("attn_bwd", 3): [(3, 3), (3, 2), (3, 1)],
    ("g_mlp_out", 2): [(3, 0, 0)], ("d_act", 2): [(3, 0, 1)], ("g_mlp_in", 2): [(2, 3, 0)],
    ("d_h2", 2): [(2, 3, 1)], ("attn_bwd", 2): [(2, 2)], ("d_h1", 2): [(2, 1)],
    ("g_mlp_out", 1): [(2, 0)], ("attn_bwd", 1): [(1, 3), (1, 2), (1, 1)],
    ("g_mlp_out", 0): [(1, 0, 0)], ("d_act", 0): [(1, 0, 1)], ("g_mlp_in", 0): [(0, 3, 0)],
    ("d_h2", 0): [(0, 3, 1)], ("attn_bwd", 0): [(0, 2)], ("g_qkv", 0): [(0, 1)], ("d_h1", 0): [(0, 0)],
}


def kernel(x, a_w_qkv, a_w_o, a_sinks, b_w_qkv, b_w_o, norm_mix, norm_mlp, mlp_w_in, mlp_w_out, final_norm, loss_target, m_a_w_qkv, m_a_w_o, m_a_sinks, m_b_w_qkv, m_b_w_o, m_norm_mix, m_norm_mlp, m_mlp_w_in, m_mlp_w_out, m_final_norm, v_a_w_qkv, v_a_w_o, v_a_sinks, v_b_w_qkv, v_b_w_o, v_norm_mix, v_norm_mlp, v_mlp_w_in, v_mlp_w_out, v_final_norm):
    depth = 4
    xs = x[0]

    shards, full, sendbuf, recv = {}, {}, {}, {}
    for i in range(depth):
        j = i // 2
        wq, wo = (a_w_qkv[j], a_w_o[j]) if i % 2 == 0 else (b_w_qkv[j], b_w_o[j])
        for t, w in enumerate((wq, wo, mlp_w_in[i], mlp_w_out[i])):
            shards[(i, t)] = w.astype(BF16)

    def land_weights(keys, slabs):
        for (i, t), g in zip(keys, slabs):
            if t in (1, 3):
                full[(i, t)] = g.reshape(-1, D)
            else:
                full[(i, t)] = g if g.shape[2] % 128 == 0 else _unshard_cols(g)

    def fwd(kind, i, fn):
        keys = FWD_CARRY.get((kind, i), [])
        out, got = fn(Comm([shards[k] for k in keys], True) if keys else None)
        land_weights(keys, got)
        return out

    def bwd(kind, i, fn):
        keys = BWD_CARRY.get((kind, i), [])
        bufs = [sendbuf[key[:2]] for key in keys]
        halves = [None if len(key) == 2 else (key[2] * (b.shape[1] // 2), b.shape[1] // 2) for key, b in zip(keys, bufs)]
        out, got = fn(Comm(bufs, False, halves, [recv.get(key[:2]) for key in keys]) if keys else None)
        recv.update((key[:2], g) for key, g in zip(keys, got))
        return out

    relu2 = lambda acc: (acc, jnp.square(jnp.maximum(acc, 0.0)))
    add = lambda acc, res: (acc + res,)
    drelu2 = lambda acc, uu: (acc * (2.0 * jnp.maximum(uu.astype(F32), 0.0)),)

    land_weights([(0, 0)], exchange("gather_first", [shards[(0, 0)]], True))
    saved = []
    for i in range(depth):
        x1 = xs
        h1 = rmsnorm_fwd(f"norm_mix{i}", x1, norm_mix[i][None])
        (qkv,) = fwd("qkv", i, lambda c: mm_nn(f"qkv{i}", h1, full[(i, 0)], out_dtype=BF16, tn=512, comm=c))
        if i % 2 == 0:
            o = fwd("attn", i, lambda c: attn_a_fwd(f"attn_a{i}", qkv, a_sinks[i // 2], comm=c))
        else:
            o = fwd("attn", i, lambda c: attn_b_fwd(f"attn_b{i}", qkv, comm=c))
        (x2,) = mm_nn(f"proj{i}", o, full[(i, 1)], out_dtype=F32, extras=(x1,), epilogue=add)[0]
        h2 = rmsnorm_fwd(f"norm_mlp{i}", x2, norm_mlp[i][None])
        u, act = fwd("mlp_in", i, lambda c: mm_nn(f"mlp_in{i}", h2, full[(i, 2)], out_dtype=BF16,
                                                   epilogue=relu2, n_out=2, comm=c))
        (xs,) = fwd("mlp_out", i, lambda c: mm_nn(f"mlp_out{i}", act, full[(i, 3)], out_dtype=F32,
                                                   extras=(x2,), epilogue=add, comm=c))
        saved.append((x1, h1, qkv, o, x2, h2, u, act))

    dx, dxb, d_final, loss_part = loss_head("loss_head", xs, final_norm[None], loss_target[0])
    loss = lax.psum(loss_part[0, 0], ("x", "y", "c"))

    d_mix, d_mlp, d_sinks = [None] * depth, [None] * depth, [None] * 2
    for i in reversed(range(depth)):
        x1, h1, qkv, o, x2, h2, u, act = saved[i]
        sendbuf[(i, 3)] = bwd("g_mlp_out", i, lambda c: mm_tn(f"g_mlp_out{i}", act, dxb, comm=c)).reshape(
            NDEV, DFF // NDEV, D)
        du = bwd("d_act", i, lambda c: mm_nt(f"d_act{i}", dxb, full[(i, 3)], out_dtype=BF16, extras=(u,),
                                             epilogue=drelu2, comm=c))
        sendbuf[(i, 2)] = bwd("g_mlp_in", i, lambda c: mm_tn(f"g_mlp_in{i}", h2, du, col_shards=True, comm=c))
        dh2 = bwd("d_h2", i, lambda c: mm_nt(f"d_h2{i}", du, full[(i, 2)], out_dtype=F32, comm=c))
        dx, dxb, d_mlp[i] = rmsnorm_bwd(f"norm_mlp_bwd{i}", dh2, x2, norm_mlp[i][None], dx)
        sendbuf[(i, 1)] = mm_tn(f"g_proj{i}", o, dxb)[0].reshape(NDEV, D // NDEV, D)
        do = mm_nt(f"d_o{i}", dxb, full[(i, 1)], out_dtype=BF16)[0]
        if i % 2 == 0:
            dq, dk, dv, ds = bwd("attn_bwd", i, lambda c: attn_a_bwd(f"attn_a_bwd{i}", qkv, a_sinks[i // 2],
                                                                       o, do, comm=c))
            d_sinks[i // 2] = ds[0, :NQ_A]
            dqkv = jnp.concatenate([dq, dk, dv], axis=1)
        else:
            dq, dk, dv = bwd("attn_bwd", i, lambda c: attn_b_bwd(f"attn_b_bwd{i}", qkv, do, comm=c))
            dqkv = jnp.concatenate([dq, dk, dv], axis=1)
        if i % 2 == 0:
            sendbuf[(i, 0)] = _shard_cols(bwd("g_qkv", i, lambda c: mm_tn(f"g_qkv{i}", h1, dqkv, tn=512, comm=c)))
        else:
            sendbuf[(i, 0)] = bwd("g_qkv", i, lambda c: mm_tn(f"g_qkv{i}", h1, dqkv, col_shards=True, comm=c))
        dh1 = bwd("d_h1", i, lambda c: mm_nt(f"d_h1{i}", dqkv, full[(i, 0)], out_dtype=F32, comm=c))
        dx, dxb, d_mix[i] = rmsnorm_bwd(f"norm_mix_bwd{i}", dh1, x1, norm_mix[i][None], dx)

    small = jnp.zeros((16, D), F32)
    small = small.at[0:4].set(jnp.concatenate(d_mix, axis=0)).at[4:8].set(jnp.concatenate(d_mlp, axis=0))
    small = small.at[8].set(d_final[0]).at[9, :2 * NQ_A].set(jnp.concatenate(d_sinks))
    (small_parts,) = exchange("exchange_last", [small], True)
    assert len(recv) == 4 * depth and len(full) == 4 * depth
    pack = lambda mix, mlp, fin, snk: (jnp.zeros((16, D), F32).at[0:4].set(mix).at[4:8].set(mlp)
                                       .at[8].set(fin).at[9, :2 * NQ_A].set(snk.reshape(-1)))
    sm = adamw("adamw_small", small_parts, pack(norm_mix, norm_mlp, final_norm, a_sinks)[None],
               pack(m_norm_mix, m_norm_mlp, m_final_norm, m_a_sinks)[None],
               pack(v_norm_mix, v_norm_mlp, v_final_norm, v_a_sinks)[None], rows=16)
    unpack = lambda a: (a[0, 0:4], a[0, 4:8], a[0, 8], a[0, 9, :2 * NQ_A].reshape(2, NQ_A))

    def update(kind, layers, t, w, m, v, rows):
        res = None
        for l, i in enumerate(layers):
            res = adamw(f"adamw_{kind}{i}", recv[(i, t)], w, m, v, rows, layer=l, prev=res)
        return res

    upd = [update("a_qkv", (0, 2), 0, a_w_qkv, m_a_w_qkv, v_a_w_qkv, 256),
           update("a_o", (0, 2), 1, a_w_o, m_a_w_o, v_a_w_o, 128),
           None,
           update("b_qkv", (1, 3), 0, b_w_qkv, m_b_w_qkv, v_b_w_qkv, 256),
           update("b_o", (1, 3), 1, b_w_o, m_b_w_o, v_b_w_o, 128),
           None, None,
           update("mlp_in", range(depth), 2, mlp_w_in, m_mlp_w_in, v_mlp_w_in, 256),
           update("mlp_out", range(depth), 3, mlp_w_out, m_mlp_w_out, v_mlp_w_out, 128),
           None]
    outs = []
    for q in range(4):
        mix, mlp, fin, snk = unpack(sm[q])
        small_out = {2: snk, 5: mix, 6: mlp, 9: fin}
        outs.append([small_out[k] if u is None else u[q] for k, u in enumerate(upd)])
    return (loss, dx[None], *outs[0], *outs[1], *outs[2], *outs[3])
```
